```python
import jax, jax.numpy as jnp
from jax import lax
import numpy as np

D_MODEL = 1024
BATCH = 8
SEQ = 4096
DEPTH = 2

CTX_LEN = 256
GRID_W = 64
W_LRU = 1024
LRU_HEADS = 16
LRU_HEAD_DIM = W_LRU // LRU_HEADS
CONV_W = 4
LRU_C = 8.0
W_FFT = 512
FFT_GROUPS = 4
FFT_GROUP_DIM = W_FFT // FFT_GROUPS
W_POOL = 512
POOL_WINDOWS = (2, 4, 8, 16)
POOL_GROUPS = len(POOL_WINDOWS)
POOL_GROUP_DIM = W_POOL // POOL_GROUPS
N_BRANCH = 3
IN_COLS = 2 * W_LRU + 2 * W_FFT + 2 * W_POOL + N_BRANCH * D_MODEL
SPLIT_POINTS = tuple(int(v) for v in np.cumsum([W_LRU, W_LRU, W_FFT, W_FFT, W_POOL, W_POOL]))
RMS_EPS = 1e-6
POS_BASE = 10000.0

kernel_name = "hybrid_rglru_fourier_pool_dit"


def rms_norm(x, g):
    xf = x.astype(jnp.float32)
    y = xf * lax.rsqrt(jnp.mean(xf * xf, axis=-1, keepdims=True) + RMS_EPS)
    return (y * g.astype(jnp.float32)).astype(x.dtype)


def sincos_2d(n_tokens, d, dtype):
    rows = n_tokens // GRID_W
    gr, gc = jnp.meshgrid(jnp.arange(rows), jnp.arange(GRID_W), indexing="ij")
    quarter = d // 4
    omega = 1.0 / (POS_BASE ** (jnp.arange(quarter, dtype=jnp.float32) / quarter))

    def emb(p):
        ang = p.reshape(-1).astype(jnp.float32)[:, None] * omega[None, :]
        return jnp.concatenate([jnp.sin(ang), jnp.cos(ang)], axis=-1)

    return jnp.concatenate([emb(gr), emb(gc)], axis=-1).astype(dtype)


def in_split(h, w_in):
    z = h @ w_in
    u_lru, z_lru, u_fft, z_fft, u_pool, z_pool, gate_logits = jnp.split(z, SPLIT_POINTS, axis=-1)
    return u_lru, z_lru, u_fft, z_fft, u_pool, z_pool, gate_logits


def depthwise_conv(u, w, b):
    pad = CONV_W // 2
    out = lax.conv_general_dilated(
        u, w[:, None, :], window_strides=(1,), padding=[(pad, CONV_W - 1 - pad)],
        dimension_numbers=("NWC", "WIO", "NWC"), feature_group_count=u.shape[-1])
    return out + b


def block_diag(u, w, b):
    bsz, t, _ = u.shape
    uh = u.reshape(bsz, t, LRU_HEADS, LRU_HEAD_DIM)
    return jnp.einsum("bthi,hij->bthj", uh, w).reshape(bsz, t, W_LRU) + b


def lru_coeffs(xc, wa, ba, wx, bx, lam):
    r = jax.nn.sigmoid(block_diag(xc, wa, ba)).astype(jnp.float32)
    i = jax.nn.sigmoid(block_diag(xc, wx, bx)).astype(jnp.float32)
    log_a = -LRU_C * r * jax.nn.softplus(-lam.astype(jnp.float32))
    a = jnp.exp(log_a)
    b = jnp.sqrt(-jnp.expm1(2.0 * log_a)) * (i * xc.astype(jnp.float32))
    return a, b


def linear_scan(a, b, h0):
    def combine(lft, rgt):
        return lft[0] * rgt[0], rgt[0] * lft[1] + rgt[1]
    acc_a, acc_h = lax.associative_scan(combine, (a, b), axis=1)
    return acc_h + acc_a * h0[:, None, :]


def lru_direction(xc_ctx, xc_lat, wa, ba, wx, bx, lam, reverse):
    flip = (lambda t: jnp.flip(t, axis=1)) if reverse else (lambda t: t)
    a_c, b_c = lru_coeffs(flip(xc_ctx), wa, ba, wx, bx, lam)
    h_c = linear_scan(a_c, b_c, jnp.zeros((xc_ctx.shape[0], W_LRU), jnp.float32))
    a_l, b_l = lru_coeffs(flip(xc_lat), wa, ba, wx, bx, lam)
    h_l = linear_scan(a_l, b_l, h_c[:, -1])
    return flip(h_c), flip(h_l)


def fourier_mix(u, w):
    bsz, t, _ = u.shape
    ug = u.reshape(bsz, t, FFT_GROUPS, FFT_GROUP_DIM).astype(jnp.float32)
    f = jnp.fft.fft2(ug, axes=(1, 3), norm="ortho").real.astype(u.dtype)
    return jnp.einsum("btgi,gij->btgj", f, w).reshape(bsz, t, W_FFT)


def pool_mix(u, w, scale):
    bsz, t, _ = u.shape
    uf = u.astype(jnp.float32)
    cs = jnp.concatenate([jnp.zeros_like(uf[:, :1]), jnp.cumsum(uf, axis=1)], axis=1)
    pos = jnp.arange(t)
    parts = []
    for g, win in enumerate(POOL_WINDOWS):
        sl = slice(g * POOL_GROUP_DIM, (g + 1) * POOL_GROUP_DIM)
        lo = jnp.clip(pos - win // 2, 0, t)
        hi = jnp.clip(pos + win - win // 2, 0, t)
        csg = cs[..., sl]
        cnt = (hi - lo).astype(jnp.float32)[None, :, None]
        mean = (jnp.take(csg, hi, axis=1) - jnp.take(csg, lo, axis=1)) / cnt
        parts.append(mean - uf[..., sl])
    p = jnp.concatenate(parts, axis=-1).astype(u.dtype).reshape(bsz, t, POOL_GROUPS, POOL_GROUP_DIM)
    y = jnp.einsum("btgi,gij->btgj", p, w).reshape(bsz, t, W_POOL)
    return y * scale


def branch_merge(y_lru, z_lru, y_fft, z_fft, y_pool, z_pool, gate_logits, proj_a, proj_b, proj_c, w_out):
    ya = (y_lru * jax.nn.silu(z_lru)) @ proj_a
    yb = (y_fft * jax.nn.silu(z_fft)) @ proj_b
    yc = (y_pool * jax.nn.silu(z_pool)) @ proj_c
    g = jax.nn.sigmoid(gate_logits.reshape(gate_logits.shape[:-1] + (N_BRANCH, D_MODEL)))
    m = g[..., 0, :] * ya + g[..., 1, :] * yb + g[..., 2, :] * yc
    return m @ w_out


def setup_inputs(seed: int = 0) -> dict:
    key = jax.random.key(seed)
    ks = jax.random.split(key, 24)
    f32 = jnp.float32
    nrm = lambda k, shape, s: jax.random.normal(k, shape, f32) * s
    u = jax.random.uniform(ks[14], (DEPTH, 2, W_LRU), f32, minval=0.9, maxval=0.999)
    s = u ** (1.0 / LRU_C)
    lam = jnp.log(s) - jnp.log1p(-s)
    return {
        "x": nrm(ks[0], (BATCH, SEQ, D_MODEL), 1.0),
        "c": nrm(ks[1], (BATCH, D_MODEL), 1.0),
        "ctx": nrm(ks[2], (BATCH, CTX_LEN, D_MODEL), 1.0),
        "c_ctx": nrm(ks[3], (D_MODEL,), 1.0),
        "norm_g": 1.0 + nrm(ks[4], (DEPTH, D_MODEL), 0.02),
        "ada_w": nrm(ks[5], (DEPTH, D_MODEL, 3 * D_MODEL), 0.5 * D_MODEL ** -0.5),
        "ada_b": nrm(ks[6], (DEPTH, 3 * D_MODEL), 0.01),
        "w_in": nrm(ks[7], (DEPTH, D_MODEL, IN_COLS), D_MODEL ** -0.5),
        "conv_w": nrm(ks[8], (DEPTH, CONV_W, W_LRU), CONV_W ** -0.5),
        "conv_b": nrm(ks[9], (DEPTH, W_LRU), 0.01),
        "lru_wa": nrm(ks[10], (DEPTH, 2, LRU_HEADS, LRU_HEAD_DIM, LRU_HEAD_DIM), LRU_HEAD_DIM ** -0.5),
        "lru_ba": nrm(ks[11], (DEPTH, 2, W_LRU), 0.01),
        "lru_wx": nrm(ks[12], (DEPTH, 2, LRU_HEADS, LRU_HEAD_DIM, LRU_HEAD_DIM), LRU_HEAD_DIM ** -0.5),
        "lru_bx": nrm(ks[13], (DEPTH, 2, W_LRU), 0.01),
        "lru_lam": lam,
        "fft_w": nrm(ks[15], (DEPTH, FFT_GROUPS, FFT_GROUP_DIM, FFT_GROUP_DIM), FFT_GROUP_DIM ** -0.5),
        "pool_w": nrm(ks[16], (DEPTH, POOL_GROUPS, POOL_GROUP_DIM, POOL_GROUP_DIM), POOL_GROUP_DIM ** -0.5),
        "pool_scale": 1.0 + nrm(ks[17], (DEPTH, W_POOL), 0.02),
        "proj_a": nrm(ks[18], (DEPTH, W_LRU, D_MODEL), W_LRU ** -0.5),
        "proj_b": nrm(ks[19], (DEPTH, W_FFT, D_MODEL), W_FFT ** -0.5),
        "proj_c": nrm(ks[20], (DEPTH, W_POOL, D_MODEL), W_POOL ** -0.5),
        "w_out": nrm(ks[21], (DEPTH, D_MODEL, D_MODEL), D_MODEL ** -0.5),
        "final_g": 1.0 + nrm(ks[22], (D_MODEL,), 0.02),
    }


def reference(x, c, ctx, c_ctx, norm_g, ada_w, ada_b, w_in, conv_w, conv_b, lru_wa, lru_ba, lru_wx, lru_bx,
              lru_lam, fft_w, pool_w, pool_scale, proj_a, proj_b, proj_c, w_out, final_g):
    n_lat = x.shape[1]
    x = x + sincos_2d(n_lat, x.shape[-1], x.dtype)[None]
    for l in range(DEPTH):
        last = l == DEPTH - 1
        mod_lat = jax.nn.silu(c) @ ada_w[l] + ada_b[l]
        mod_ctx = jax.nn.silu(c_ctx) @ ada_w[l] + ada_b[l]
        sh_l, sc_l, gt_l = jnp.split(mod_lat, 3, axis=-1)
        sh_c, sc_c, gt_c = jnp.split(mod_ctx, 3, axis=-1)
        h_lat = rms_norm(x, norm_g[l]) * (1.0 + sc_l[:, None]) + sh_l[:, None]
        h_ctx = rms_norm(ctx, norm_g[l]) * (1.0 + sc_c) + sh_c
        ul_lru, zl_lru, ul_fft, zl_fft, ul_pool, zl_pool, gl = in_split(h_lat, w_in[l])
        uc_lru, zc_lru, uc_fft, zc_fft, uc_pool, zc_pool, gc = in_split(h_ctx, w_in[l])
        xc_lat = depthwise_conv(ul_lru, conv_w[l], conv_b[l])
        xc_ctx = depthwise_conv(uc_lru, conv_w[l], conv_b[l])
        hc_f, hl_f = lru_direction(xc_ctx, xc_lat, lru_wa[l, 0], lru_ba[l, 0], lru_wx[l, 0], lru_bx[l, 0],
                                   lru_lam[l, 0], False)
        hc_b, hl_b = lru_direction(xc_ctx, xc_lat, lru_wa[l, 1], lru_ba[l, 1], lru_wx[l, 1], lru_bx[l, 1],
                                   lru_lam[l, 1], True)
        y_lru_lat = (hl_f + hl_b).astype(x.dtype)
        out_lat = branch_merge(y_lru_lat, zl_lru, fourier_mix(ul_fft, fft_w[l]), zl_fft,
                               pool_mix(ul_pool, pool_w[l], pool_scale[l]), zl_pool, gl,
                               proj_a[l], proj_b[l], proj_c[l], w_out[l])
        if not last:
            y_lru_ctx = (hc_f + hc_b).astype(ctx.dtype)
            out_ctx = branch_merge(y_lru_ctx, zc_lru, fourier_mix(uc_fft, fft_w[l]), zc_fft,
                                   pool_mix(uc_pool, pool_w[l], pool_scale[l]), zc_pool, gc,
                                   proj_a[l], proj_b[l], proj_c[l], w_out[l])
            ctx = ctx + gt_c * out_ctx
        x = x + gt_l[:, None] * out_lat
    return rms_norm(x, final_g)
```

```python
import functools

import numpy as np
import jax
import jax.numpy as jnp
from jax import lax
from jax.experimental import pallas as pl
from jax.experimental.pallas import tpu as pltpu

F32 = jnp.float32
BF16 = jnp.bfloat16

GRID_WIDTH = 64
LRU_POWER = 8.0
NORM_EPS = 1e-6
POSITION_BASE = 10000.0
POOL_WINDOW_SIZES = (2, 4, 8, 16)
SUBLANES = 8
LANES = 128
DFT_RADIX = 64
POOL_HALO_ROWS = 64
TOKEN_TILE_ROWS = 512
LRU_CHUNK_STEPS = 128
LRU_GATE_WIDTH = 256
SCAN_UNROLL = 8
VMEM_LIMIT_BYTES = 56 * 1024 * 1024


def _half_tanh_sigmoid(x_half):
    return 0.5 * (1.0 + jnp.tanh(x_half))


def _sigmoid(x):
    return _half_tanh_sigmoid(0.5 * x)


def _silu(x):
    return x * _sigmoid(x)


def _params(n_axes=1):
    return pltpu.CompilerParams(dimension_semantics=("arbitrary",) * n_axes, vmem_limit_bytes=VMEM_LIMIT_BYTES)


def _mod_body(cc_ref, w_ref, b_ref, o_ref):
    s = _silu(cc_ref[...]).astype(BF16)
    o_ref[0] = jnp.dot(s, w_ref[0], preferred_element_type=F32) + b_ref[0]


def _modulation(cc, ada_w, ada_b):
    depth, d, n = ada_w.shape
    tn = 1024
    return pl.pallas_call(
        _mod_body,
        grid=(depth, n // tn),
        in_specs=[
            pl.BlockSpec((2 * SUBLANES, d), lambda l, i: (0, 0)),
            pl.BlockSpec((1, d, tn), lambda l, i: (l, 0, i)),
            pl.BlockSpec((1, 1, tn), lambda l, i: (l, 0, i)),
        ],
        out_specs=pl.BlockSpec((1, 2 * SUBLANES, tn), lambda l, i: (l, 0, i)),
        out_shape=jax.ShapeDtypeStruct((depth, 2 * SUBLANES, n), F32),
        compiler_params=_params(2),
        name="adaln_mod",
    )(cc, ada_w.astype(BF16), ada_b.reshape(depth, 1, n))


def _token_specs(first, n_lat_tiles, tt, d, bsz):
    if first:
        return [
            pl.BlockSpec((bsz, tt, d), lambda j: (0, jnp.minimum(j, n_lat_tiles - 1), 0)),
            pl.BlockSpec((bsz, tt, d), lambda j: (0, jnp.maximum(j - n_lat_tiles, 0), 0)),
            pl.BlockSpec((tt, 1, d), lambda j: (jnp.minimum(j, n_lat_tiles - 1), 0, 0)),
        ]
    return [pl.BlockSpec((tt * bsz, d), lambda j: (j, 0))]


def _load_tokens(first, is_lat, tok_refs, xs_ref):
    if not first:
        (x_ref,) = tok_refs
        rows, d = x_ref.shape
        return x_ref[...].reshape(rows // SUBLANES, SUBLANES, d)
    x_ref, ctx_ref, pos_ref = tok_refs

    @pl.when(is_lat)
    def _():
        xs_ref[...] = jnp.transpose(x_ref[...], (1, 0, 2)) + pos_ref[...]

    @pl.when(jnp.logical_not(is_lat))
    def _():
        xs_ref[...] = jnp.transpose(ctx_ref[...], (1, 0, 2))

    return xs_ref[...]


def _norm_mod(x3, g, shift, scale):
    ms = jnp.mean(x3 * x3, axis=-1, keepdims=True)
    y = x3 * lax.rsqrt(ms + NORM_EPS) * g
    h = y * (1.0 + scale) + shift
    tt, b, d = x3.shape
    return h.reshape(tt * b, d).astype(BF16)


def _k1_body(first, n_lat_tiles, w_lru, w_fft, *refs):
    n_tok = 3 if first else 1
    tok_refs = refs[:n_tok]
    mod_ref, g_ref, wu_ref, cd_ref, ulru_ref, v_ref, upool_ref = refs[n_tok:n_tok + 7]
    xs_ref = refs[n_tok + 7] if first else None
    j = pl.program_id(0)
    x3 = _load_tokens(first, j < n_lat_tiles, tok_refs, xs_ref)
    h = _norm_mod(x3, g_ref[...], mod_ref[0, 0], mod_ref[0, 1])
    u = jnp.dot(h, wu_ref[...], preferred_element_type=F32)
    ulru_ref[...] = u[:, :w_lru].astype(BF16)
    ufft = u[:, w_lru:w_lru + w_fft].astype(BF16)
    v_ref[...] = jnp.dot(ufft, cd_ref[...], preferred_element_type=F32)
    upool_ref[...] = u[:, w_lru + w_fft:].astype(BF16)


def _k1(tokens, mod_k, g, wu, cd, first, n_all, n_lat_tiles, w_lru, w_fft, w_pool):
    d = g.shape[-1]
    bsz = SUBLANES
    r = TOKEN_TILE_ROWS
    tt = r // bsz
    const2 = lambda j: (0, 0)
    in_specs = _token_specs(first, n_lat_tiles, tt, d, bsz) + [
        pl.BlockSpec((1, 3, bsz, d), lambda j: (jnp.where(j < n_lat_tiles, 0, 1), 0, 0, 0)),
        pl.BlockSpec((1, d), const2),
        pl.BlockSpec(wu.shape, const2, pipeline_mode=pl.Buffered(1)),
        pl.BlockSpec(cd.shape, const2, pipeline_mode=pl.Buffered(1)),
    ]
    return pl.pallas_call(
        functools.partial(_k1_body, first, n_lat_tiles, w_lru, w_fft),
        grid=(n_all // r,),
        in_specs=in_specs,
        out_specs=[
            pl.BlockSpec((r, w_lru), lambda j: (j, 0)),
            pl.BlockSpec((r, 2 * w_fft), lambda j: (j, 0)),
            pl.BlockSpec((r, w_pool), lambda j: (j, 0)),
        ],
        out_shape=[
            jax.ShapeDtypeStruct((n_all, w_lru), BF16),
            jax.ShapeDtypeStruct((n_all, 2 * w_fft), F32),
            jax.ShapeDtypeStruct((n_all, w_pool), BF16),
        ],
        scratch_shapes=[pltpu.VMEM((tt, bsz, d), F32)] if first else [],
        compiler_params=_params(),
        name="norm_uproj",
    )(*tokens, mod_k, g, wu, cd)


def _lru_chunk(d, j, n_lat_chunks, n_ctx_chunks):
    in_ctx = j < n_ctx_chunks
    jj = j - n_ctx_chunks
    ctx_ck = n_lat_chunks + jnp.where(d == 0, j, n_ctx_chunks - 1 - j)
    lat_ck = jnp.where(d == 0, jj, n_lat_chunks - 1 - jj)
    return jnp.where(in_ctx, ctx_ck, lat_ck)


def _lru_body(n_lat_chunks, n_ctx_chunks, steps, u_ref, up_ref, un_ref, cw_ref, cb_ref, wg_ref, ba_ref,
              bx_ref, lam_ref, o_ref, a_sc, b_sc, h_sc, state):
    d = pl.program_id(1)
    j = pl.program_id(2)
    ck = _lru_chunk(d, j, n_lat_chunks, n_ctx_chunks)
    first = jnp.logical_or(ck == 0, ck == n_lat_chunks)
    last = jnp.logical_or(ck == n_lat_chunks - 1, ck == n_lat_chunks + n_ctx_chunks - 1)
    rows, cb = u_ref.shape

    @pl.when(j == 0)
    def _():
        state[...] = jnp.zeros_like(state)

    u = u_ref[...].astype(F32)
    up = up_ref[...].astype(F32) * jnp.where(first, 0.0, 1.0).astype(F32)
    un = un_ref[...].astype(F32)[:SUBLANES] * jnp.where(last, 0.0, 1.0).astype(F32)
    ext = jnp.concatenate([up, u, un], axis=0)
    cwh = 0.5 * cw_ref[...]
    xh = 0.5 * cb_ref[...] + cwh[0:1] * ext[0:rows]
    for k in range(1, 4):
        xh = xh + cwh[k:k + 1] * ext[k * SUBLANES:k * SUBLANES + rows]

    lam = lam_ref[0]
    half_rate = (-0.5 * LRU_POWER) * (jnp.maximum(-lam, 0.0) + jnp.log1p(jnp.exp(-jnp.abs(lam))))
    half_ba = 0.5 * ba_ref[0]
    half_bx = 0.5 * bx_ref[0]
    gw = wg_ref.shape[2]
    for g in range(cb // gw):
        sl = slice(g * gw, (g + 1) * gw)
        xg = xh[:, sl]
        gates = jnp.dot(xg.astype(BF16), wg_ref[0, g], preferred_element_type=F32)
        log_a = (1.0 + jnp.tanh(gates[:, :gw] + half_ba[:, sl])) * half_rate[:, sl]
        a = jnp.exp(log_a)
        q = jnp.tanh(log_a) * (-1.0 - a * a)
        gain = jnp.where(q > 0.0, q * lax.rsqrt(q), 0.0)
        a_sc[:, sl] = a
        b_sc[:, sl] = gain * ((1.0 + jnp.tanh(gates[:, gw:] + half_bx[:, sl])) * xg)

    n_blocks = steps // SCAN_UNROLL
    block_rows = SCAN_UNROLL * SUBLANES

    def scan(reverse):
        def block(i, h):
            base = pl.multiple_of((n_blocks - 1 - i if reverse else i) * block_rows, block_rows)
            for k in (range(SCAN_UNROLL - 1, -1, -1) if reverse else range(SCAN_UNROLL)):
                r0 = base + k * SUBLANES
                h = a_sc[pl.ds(r0, SUBLANES), :] * h + b_sc[pl.ds(r0, SUBLANES), :]
                h_sc[pl.ds(r0, SUBLANES), :] = h
            return h

        state[...] = lax.fori_loop(0, n_blocks, block, state[...])

    @pl.when(d == 0)
    def _():
        scan(False)

    @pl.when(d == 1)
    def _():
        scan(True)

    o_ref[0] = h_sc[...].astype(BF16)


def _lru(u_lru, conv_w, conv_b, wg, ba, bx, lam, n_lat_rows, n_ctx_rows, steps):
    n_all, w = u_lru.shape
    rc = steps * SUBLANES
    halo = 2 * SUBLANES
    cb = w
    n_lat_chunks = n_lat_rows // rc
    n_ctx_chunks = n_ctx_rows // rc
    n_chunks = n_lat_chunks + n_ctx_chunks
    gw = wg.shape[2]
    ck = functools.partial(_lru_chunk, n_lat_chunks=n_lat_chunks, n_ctx_chunks=n_ctx_chunks)
    body = functools.partial(_lru_body, n_lat_chunks, n_ctx_chunks, steps)
    return pl.pallas_call(
        body,
        grid=(w // cb, 2, n_chunks),
        in_specs=[
            pl.BlockSpec((rc, cb), lambda c, d, j: (ck(d, j), c)),
            pl.BlockSpec((halo, cb), lambda c, d, j: (jnp.maximum(ck(d, j) * (rc // halo) - 1, 0), c)),
            pl.BlockSpec((halo, cb),
                         lambda c, d, j: (jnp.minimum((ck(d, j) + 1) * (rc // halo), n_all // halo - 1), c)),
            pl.BlockSpec((4, cb), lambda c, d, j: (0, c)),
            pl.BlockSpec((1, cb), lambda c, d, j: (0, c)),
            pl.BlockSpec((1, cb // gw, gw, 2 * gw), lambda c, d, j: (d, c, 0, 0)),
            pl.BlockSpec((1, 1, cb), lambda c, d, j: (d, 0, c)),
            pl.BlockSpec((1, 1, cb), lambda c, d, j: (d, 0, c)),
            pl.BlockSpec((1, 1, cb), lambda c, d, j: (d, 0, c)),
        ],
        out_specs=pl.BlockSpec((1, rc, cb), lambda c, d, j: (d, ck(d, j), c)),
        out_shape=jax.ShapeDtypeStruct((2, n_all, w), BF16),
        scratch_shapes=[
            pltpu.VMEM((rc, cb), F32),
            pltpu.VMEM((rc, cb), F32),
            pltpu.VMEM((rc, cb), F32),
            pltpu.VMEM((SUBLANES, cb), F32),
        ],
        compiler_params=_params(3),
        name="rglru_scan",
    )(u_lru, u_lru, u_lru, conv_w, conv_b.reshape(1, w), wg, ba.reshape(2, 1, w), bx.reshape(2, 1, w),
      lam.reshape(2, 1, w))


def _dft_constants(n2):
    n1 = DFT_RADIX
    n = n1 * n2
    eye = jnp.eye(SUBLANES, dtype=F32)
    k2 = np.arange(n2)
    ang1 = 2.0 * np.pi * ((k2[:, None] * k2[None, :]) % n2) / n2
    c1 = jnp.kron(jnp.asarray(np.cos(ang1) / np.sqrt(n2), F32), eye)
    s1 = jnp.kron(jnp.asarray(np.sin(ang1) / np.sqrt(n2), F32), eye)
    m1 = jnp.block([[c1, s1], [-s1, c1]])
    k1 = np.arange(n1)
    ang2 = 2.0 * np.pi * ((k1[:, None] * k1[None, :]) % n1) / n1
    m2 = jnp.concatenate([jnp.kron(jnp.asarray(np.cos(ang2) / np.sqrt(n1), F32), eye),
                          jnp.kron(jnp.asarray(np.sin(ang2) / np.sqrt(n1), F32), eye)], axis=1)
    angt = 2.0 * np.pi * ((k1[:, None] * k2[None, :]) % n) / n
    shape = (n1, n2, SUBLANES, LANES)

    def table(vals):
        return jnp.broadcast_to(jnp.asarray(vals, F32)[:, :, None, None], shape).reshape(n1, n2 * SUBLANES, LANES)

    return m1.astype(BF16), m2.astype(BF16), table(np.cos(angt)), table(np.sin(angt))


def _fft1_body(w_fft, v_ref, m1_ref, twc_ref, tws_ref, z_ref):
    n2 = v_ref.shape[0]
    half = n2 * SUBLANES
    v = v_ref[...].reshape(half, 2 * w_fft)
    xs = jnp.concatenate([v[:, :w_fft], v[:, w_fft:]], axis=0).astype(BF16)
    r = jnp.dot(m1_ref[...], xs, preferred_element_type=F32)
    rr, ri = r[:half], r[half:]
    reps = w_fft // LANES
    cw = jnp.concatenate([twc_ref[0]] * reps, axis=1)
    sw = jnp.concatenate([tws_ref[0]] * reps, axis=1)
    zr = rr * cw + ri * sw
    zi = ri * cw - rr * sw
    z_ref[...] = jnp.concatenate([zr, zi], axis=1).reshape(n2, 1, SUBLANES, 2 * w_fft)


def _fft2_body(w_fft, z_ref, m2_ref, y_ref):
    rows = DFT_RADIX * SUBLANES
    z = z_ref[...].reshape(rows, 2 * w_fft)
    zs = jnp.concatenate([z[:, :w_fft], z[:, w_fft:]], axis=0).astype(BF16)
    y = jnp.dot(m2_ref[...], zs, preferred_element_type=F32)
    y_ref[...] = y.reshape(DFT_RADIX, 1, SUBLANES, w_fft)


def _position_dft(v4, n2, block0, w_fft):
    m1, m2, twc, tws = _dft_constants(n2)
    n1 = DFT_RADIX
    c2 = 2 * w_fft
    z = pl.pallas_call(
        functools.partial(_fft1_body, w_fft),
        grid=(n1,),
        in_specs=[
            pl.BlockSpec((n2, 1, SUBLANES, c2), lambda a: (block0, a, 0, 0)),
            pl.BlockSpec(m1.shape, lambda a: (0, 0)),
            pl.BlockSpec((1, n2 * SUBLANES, LANES), lambda a: (a, 0, 0)),
            pl.BlockSpec((1, n2 * SUBLANES, LANES), lambda a: (a, 0, 0)),
        ],
        out_specs=pl.BlockSpec((n2, 1, SUBLANES, c2), lambda a: (0, a, 0, 0)),
        out_shape=jax.ShapeDtypeStruct((n2, n1, SUBLANES, c2), F32),
        compiler_params=_params(),
        name="dft_stage1",
    )(v4, m1, twc, tws)
    y = pl.pallas_call(
        functools.partial(_fft2_body, w_fft),
        grid=(n2,),
        in_specs=[
            pl.BlockSpec((1, n1, SUBLANES, c2), lambda c: (c, 0, 0, 0)),
            pl.BlockSpec(m2.shape, lambda c: (0, 0)),
        ],
        out_specs=pl.BlockSpec((n1, 1, SUBLANES, w_fft), lambda c: (0, c, 0, 0)),
        out_shape=jax.ShapeDtypeStruct((n1, n2, SUBLANES, w_fft), F32),
        compiler_params=_params(),
        name="dft_stage2",
    )(z, m2)
    return y.reshape(n1 * n2 * SUBLANES, w_fft)


def _pool_minus_token(ue, rows, t0, t_seq):
    halo = POOL_HALO_ROWS
    tvec = t0 + lax.broadcasted_iota(jnp.int32, (rows, LANES), 0) // SUBLANES
    parts = []
    for g, win in enumerate(POOL_WINDOW_SIZES):
        col = ue[:, g * LANES:(g + 1) * LANES]
        n = col.shape[0]
        acc = col[0:n - SUBLANES] + col[SUBLANES:n]
        e0 = SUBLANES
        span = 1
        while 2 * span < win:
            sh = span * SUBLANES
            n = acc.shape[0]
            acc = acc[0:n - 2 * sh] + acc[2 * sh:n]
            e0 += sh
            span *= 2
        wsum = acc[halo - e0:halo - e0 + rows]
        half = win // 2
        cnt = jnp.minimum(tvec + half, t_seq) - jnp.maximum(tvec - half, 0)
        parts.append(wsum / cnt.astype(F32) - col[halo:halo + rows])
    return jnp.concatenate(parts, axis=1)


def _k3_body(cfg, *refs):
    first, last, n_lat_tiles, n_tiles, t_lat, t_ctx, w_lru, w_fft, w_pool = cfg
    n_tok = 3 if first else 1
    tok_refs = refs[:n_tok]
    refs = list(refs[n_tok:])
    mod_ref, g_ref, hf_ref, hb_ref, yl_ref = refs[:5]
    refs = refs[5:]
    yc_ref = None if last else refs.pop(0)
    (up_ref, upp_ref, upn_ref, wz_ref, pa_ref, pb_ref, pc_ref, wo_ref, fw_ref, pw_ref, ps_ref, fg_ref,
     o_ref) = refs[:13]
    xs_ref = refs[13] if first else None
    j = pl.program_id(0)
    is_lat = j < n_lat_tiles
    x3 = _load_tokens(first, is_lat, tok_refs, xs_ref)
    tt, bsz, d = x3.shape
    rows = tt * bsz
    h = _norm_mod(x3, g_ref[...], mod_ref[0, 0], mod_ref[0, 1])

    z_lru = jnp.dot(h, wz_ref[:, 0:w_lru], preferred_element_type=F32)
    y_lru = hf_ref[0].astype(F32) + hb_ref[0].astype(F32)
    ya = jnp.dot((y_lru * _silu(z_lru)).astype(BF16), pa_ref[...], preferred_element_type=F32)

    c0 = w_lru
    z_fft = jnp.dot(h, wz_ref[:, c0:c0 + w_fft], preferred_element_type=F32)
    f = yl_ref[...] if last else jnp.where(is_lat, yl_ref[...], yc_ref[...])
    y_fft = jnp.dot(f.astype(BF16), fw_ref[...], preferred_element_type=F32)
    yb = jnp.dot((y_fft * _silu(z_fft)).astype(BF16), pb_ref[...], preferred_element_type=F32)

    c0 += w_fft
    z_pool = jnp.dot(h, wz_ref[:, c0:c0 + w_pool], preferred_element_type=F32)
    seq_first = jnp.logical_or(j == 0, j == n_lat_tiles)
    seq_last = jnp.logical_or(j == n_lat_tiles - 1, j == n_tiles - 1)
    ue = jnp.concatenate([
        upp_ref[...].astype(F32) * jnp.where(seq_first, 0.0, 1.0).astype(F32),
        up_ref[...].astype(F32),
        upn_ref[...].astype(F32) * jnp.where(seq_last, 0.0, 1.0).astype(F32)], axis=0)
    t0 = jnp.where(is_lat, j, j - n_lat_tiles) * tt
    t_seq = jnp.where(is_lat, t_lat, t_ctx)
    p = _pool_minus_token(ue, rows, t0, t_seq)
    y_pool = jnp.dot(p.astype(BF16), pw_ref[...], preferred_element_type=F32) * ps_ref[...]
    yc = jnp.dot((y_pool * _silu(z_pool)).astype(BF16), pc_ref[...], preferred_element_type=F32)

    c0 += w_pool
    m = _sigmoid(jnp.dot(h, wz_ref[:, c0:c0 + d], preferred_element_type=F32)) * ya
    m = m + _sigmoid(jnp.dot(h, wz_ref[:, c0 + d:c0 + 2 * d], preferred_element_type=F32)) * yb
    m = m + _sigmoid(jnp.dot(h, wz_ref[:, c0 + 2 * d:c0 + 3 * d], preferred_element_type=F32)) * yc
    out = jnp.dot(m.astype(BF16), wo_ref[...], preferred_element_type=F32)
    xn = x3 + mod_ref[0, 2] * out.reshape(tt, bsz, d)
    if last:
        ms = jnp.mean(xn * xn, axis=-1, keepdims=True)
        xn = xn * lax.rsqrt(ms + NORM_EPS) * fg_ref[...]
        o_ref[...] = jnp.transpose(xn, (1, 0, 2))
    else:
        o_ref[...] = xn.reshape(rows, d)


def _k3(tokens, mod_k, g, h_lru, y_lat, y_ctx, u_pool, wz, pa, pb, pc, wo, fw, pw, ps, fg, first, last,
        n_lat_tiles, t_lat, t_ctx):
    d = g.shape[-1]
    bsz = SUBLANES
    r = TOKEN_TILE_ROWS
    tt = r // bsz
    n_all = u_pool.shape[0]
    n_tiles = n_all // r
    grid_tiles = n_lat_tiles if last else n_tiles
    w_lru = h_lru.shape[2]
    w_fft = y_lat.shape[1]
    w_pool = u_pool.shape[1]
    hr = r // POOL_HALO_ROWS
    n_halo = n_all // POOL_HALO_ROWS
    cfg = (first, last, n_lat_tiles, n_tiles, t_lat, t_ctx, w_lru, w_fft, w_pool)
    const2 = lambda j: (0, 0)
    in_specs = _token_specs(first, n_lat_tiles, tt, d, bsz) + [
        pl.BlockSpec((1, 3, bsz, d), lambda j: (jnp.where(j < n_lat_tiles, 0, 1), 0, 0, 0)),
        pl.BlockSpec((1, d), const2),
        pl.BlockSpec((1, r, w_lru), lambda j: (0, j, 0)),
        pl.BlockSpec((1, r, w_lru), lambda j: (1, j, 0)),
        pl.BlockSpec((r, w_fft), lambda j: (jnp.minimum(j, n_lat_tiles - 1), 0)),
    ]
    args = list(tokens) + [mod_k, g, h_lru, h_lru, y_lat]
    if not last:
        in_specs.append(pl.BlockSpec((r, w_fft), lambda j: (jnp.maximum(j - n_lat_tiles, 0), 0)))
        args.append(y_ctx)
    in_specs += [
        pl.BlockSpec((r, w_pool), lambda j: (j, 0)),
        pl.BlockSpec((POOL_HALO_ROWS, w_pool), lambda j: (jnp.maximum(j * hr - 1, 0), 0)),
        pl.BlockSpec((POOL_HALO_ROWS, w_pool), lambda j: (jnp.minimum((j + 1) * hr, n_halo - 1), 0)),
    ]
    args += [u_pool, u_pool, u_pool]
    for wgt in (wz, pa, pb, pc, wo, fw, pw, ps, fg):
        in_specs.append(pl.BlockSpec(wgt.shape, const2, pipeline_mode=pl.Buffered(1)))
        args.append(wgt)
    if last:
        out_spec = pl.BlockSpec((bsz, tt, d), lambda j: (0, j, 0))
        out_shape = jax.ShapeDtypeStruct((bsz, t_lat, d), F32)
    else:
        out_spec = pl.BlockSpec((r, d), lambda j: (j, 0))
        out_shape = jax.ShapeDtypeStruct((n_all, d), F32)
    return pl.pallas_call(
        functools.partial(_k3_body, cfg),
        grid=(grid_tiles,),
        in_specs=in_specs,
        out_specs=out_spec,
        out_shape=out_shape,
        scratch_shapes=[pltpu.VMEM((tt, bsz, d), F32)] if first else [],
        compiler_params=_params(),
        name="merge_residual",
    )(*args)


def _position_code(n_tokens, d):
    rows = n_tokens // GRID_WIDTH
    gr, gc = jnp.meshgrid(jnp.arange(rows), jnp.arange(GRID_WIDTH), indexing="ij")
    quarter = d // 4
    omega = 1.0 / (POSITION_BASE ** (jnp.arange(quarter, dtype=F32) / quarter))

    def emb(p):
        ang = p.reshape(-1).astype(F32)[:, None] * omega[None, :]
        return jnp.concatenate([jnp.sin(ang), jnp.cos(ang)], axis=-1)

    return jnp.concatenate([emb(gr), emb(gc)], axis=-1).astype(F32)


def _block_diag(w):
    g, n, _ = w.shape
    eye = jnp.eye(g, dtype=w.dtype)
    return jnp.einsum("gij,gk->gikj", w, eye).reshape(g * n, g * n)


def _channel_dft_matrix(groups, n):
    k = np.arange(n)
    ang = 2.0 * np.pi * ((k[:, None] * k[None, :]) % n) / n
    eye = jnp.eye(groups, dtype=F32)
    cd = jnp.concatenate([jnp.kron(eye, jnp.asarray(np.cos(ang) / np.sqrt(n), F32)),
                          jnp.kron(eye, jnp.asarray(-np.sin(ang) / np.sqrt(n), F32))], axis=1)
    return cd.astype(BF16)


def kernel(x, c, ctx, c_ctx, norm_g, ada_w, ada_b, w_in, conv_w, conv_b, lru_wa, lru_ba, lru_wx, lru_bx, lru_lam,
           fft_w, pool_w, pool_scale, proj_a, proj_b, proj_c, w_out, final_g):
    bsz, t_lat, d = x.shape
    t_ctx = ctx.shape[1]
    depth = w_in.shape[0]
    w_lru = conv_w.shape[-1]
    fft_groups, fft_dim = fft_w.shape[1], fft_w.shape[2]
    w_fft = fft_groups * fft_dim
    w_pool = pool_w.shape[1] * pool_w.shape[2]
    heads, head_dim = lru_wa.shape[2], lru_wa.shape[3]
    assert bsz == SUBLANES and fft_dim == LANES and pool_w.shape[2] == LANES
    assert t_lat % DFT_RADIX == 0 and t_ctx % DFT_RADIX == 0 and (t_lat // DFT_RADIX) % (t_ctx // DFT_RADIX) == 0
    n_lat_rows, n_ctx_rows = t_lat * bsz, t_ctx * bsz
    n_all = n_lat_rows + n_ctx_rows
    assert n_lat_rows % TOKEN_TILE_ROWS == 0 and n_ctx_rows % TOKEN_TILE_ROWS == 0
    n_lat_tiles = n_lat_rows // TOKEN_TILE_ROWS
    lru_steps = min(LRU_CHUNK_STEPS, t_ctx)
    heads_per_group = LRU_GATE_WIDTH // head_dim

    pos3 = _position_code(t_lat, d).reshape(t_lat, 1, d)
    cc = jnp.concatenate([c, jnp.broadcast_to(c_ctx[None, :], (bsz, d))], axis=0)
    mod = _modulation(cc, ada_w, ada_b)
    cd = _channel_dft_matrix(fft_groups, fft_dim)
    s1, s2, s3, s4, s5, s6 = (w_lru, 2 * w_lru, 2 * w_lru + w_fft, 2 * w_lru + 2 * w_fft,
                              2 * w_lru + 2 * w_fft + w_pool, 2 * w_lru + 2 * w_fft + 2 * w_pool)
    n2_lat, n2_ctx = t_lat // DFT_RADIX, t_ctx // DFT_RADIX

    def gate_w(w):
        wd = w.reshape(2 * heads // heads_per_group, heads_per_group, head_dim, head_dim)
        return jax.vmap(_block_diag)(wd).reshape(2, heads // heads_per_group, LRU_GATE_WIDTH, LRU_GATE_WIDTH)

    tokens = (x, ctx, pos3)
    out = None
    for l in range(depth):
        first, last = l == 0, l == depth - 1
        mod_k = mod[l].reshape(2, bsz, 3, d).transpose(0, 2, 1, 3)
        g = norm_g[l].reshape(1, d)
        wl = w_in[l].astype(BF16)
        wu = jnp.concatenate([wl[:, 0:s1], wl[:, s2:s3], wl[:, s4:s5]], axis=1)
        wz = jnp.concatenate([wl[:, s1:s2], wl[:, s3:s4], wl[:, s5:s6], wl[:, s6:]], axis=1)
        u_lru, v, u_pool = _k1(tokens, mod_k, g, wu, cd, first, n_all, n_lat_tiles, w_lru, w_fft, w_pool)

        wg = jnp.concatenate([gate_w(lru_wa[l]), gate_w(lru_wx[l])], axis=-1).astype(BF16)
        h_lru = _lru(u_lru, conv_w[l], conv_b[l], wg, lru_ba[l], lru_bx[l], lru_lam[l], n_lat_rows, n_ctx_rows,
                     lru_steps)

        v4 = v.reshape(n_all // (DFT_RADIX * bsz), DFT_RADIX, bsz, 2 * w_fft)
        y_lat = _position_dft(v4, n2_lat, 0, w_fft)
        y_ctx = None if last else _position_dft(v4, n2_ctx, n2_lat // n2_ctx, w_fft)

        out = _k3(tokens, mod_k, g, h_lru, y_lat, y_ctx, u_pool, wz,
                  proj_a[l].astype(BF16), proj_b[l].astype(BF16), proj_c[l].astype(BF16), w_out[l].astype(BF16),
                  _block_diag(fft_w[l]).astype(BF16), _block_diag(pool_w[l]).astype(BF16),
                  pool_scale[l].reshape(1, w_pool), final_g.reshape(1, d), first, last, n_lat_tiles, t_lat, t_ctx)
        tokens = (out,)
    return out
```

```python
import functools

import numpy as np
import jax
import jax.numpy as jnp
from jax import lax
from jax.experimental import pallas as pl
from jax.experimental.pallas import tpu as pltpu

F32 = jnp.float32
BF16 = jnp.bfloat16

GRID_WIDTH = 64
LRU_POWER = 8.0
NORM_EPS = 1e-6
POSITION_BASE = 10000.0
POOL_WINDOW_SIZES = (2, 4, 8, 16)
CONV_TAPS = 4
SUBLANES = 8
LANES = 128
DFT_RADIX = 64
POOL_HALO_ROWS = 64
TOKEN_TILE_ROWS = 512
LRU_CHUNK_STEPS = 128
LRU_GATE_WIDTH = 256
SCAN_UNROLL = 8
VMEM_LIMIT_BYTES = 56 * 1024 * 1024


def _half_tanh_sigmoid(x_half):
    return 0.5 * (1.0 + jnp.tanh(x_half))


def _sigmoid(x):
    return _half_tanh_sigmoid(0.5 * x)


def _silu(x):
    return x * _sigmoid(x)


def _params(n_axes=1):
    return pltpu.CompilerParams(dimension_semantics=("arbitrary",) * n_axes, vmem_limit_bytes=VMEM_LIMIT_BYTES)


def _resident(shape, index_map):
    return pl.BlockSpec(shape, index_map, pipeline_mode=pl.Buffered(1))


def _mod_body(cc_ref, w_ref, b_ref, o_ref):
    s = _silu(cc_ref[...]).astype(BF16)
    o_ref[0] = jnp.dot(s, w_ref[0], preferred_element_type=F32) + b_ref[0]


def _modulation(cc, ada_w, ada_b):
    depth, d, n = ada_w.shape
    tn = 1024
    return pl.pallas_call(
        _mod_body,
        grid=(depth, n // tn),
        in_specs=[
            pl.BlockSpec((2 * SUBLANES, d), lambda l, i: (0, 0)),
            pl.BlockSpec((1, d, tn), lambda l, i: (l, 0, i)),
            pl.BlockSpec((1, 1, tn), lambda l, i: (l, 0, i)),
        ],
        out_specs=pl.BlockSpec((1, 2 * SUBLANES, tn), lambda l, i: (l, 0, i)),
        out_shape=jax.ShapeDtypeStruct((depth, 2 * SUBLANES, n), F32),
        compiler_params=_params(2),
        name="adaln_mod",
    )(cc, ada_w.astype(BF16), ada_b.reshape(depth, 1, n))


def _token_specs(first, n_lat_tiles, n_tiles, tt, d, bsz):
    n_ctx_tiles = n_tiles - n_lat_tiles
    if first:
        return [
            pl.BlockSpec((bsz, tt, d), lambda j: (0, jnp.minimum(j, n_lat_tiles - 1), 0)),
            pl.BlockSpec((bsz, tt, d), lambda j: (0, jnp.clip(j - n_lat_tiles, 0, n_ctx_tiles - 1), 0)),
            pl.BlockSpec((tt, 1, d), lambda j: (jnp.minimum(j, n_lat_tiles - 1), 0, 0)),
        ]
    return [pl.BlockSpec((tt * bsz, d), lambda j: (jnp.minimum(j, n_tiles - 1), 0))]


def _load_tokens(first, is_lat, tok_refs, xs_ref):
    if not first:
        (x_ref,) = tok_refs
        rows, d = x_ref.shape
        return x_ref[...].reshape(rows // SUBLANES, SUBLANES, d)
    x_ref, ctx_ref, pos_ref = tok_refs

    @pl.when(is_lat)
    def _():
        xs_ref[...] = jnp.transpose(x_ref[...], (1, 0, 2)) + pos_ref[...]

    @pl.when(jnp.logical_not(is_lat))
    def _():
        xs_ref[...] = jnp.transpose(ctx_ref[...], (1, 0, 2))

    return xs_ref[...]


def _norm_mod(x3, g, shift, scale):
    ms = jnp.mean(x3 * x3, axis=-1, keepdims=True)
    y = x3 * lax.rsqrt(ms + NORM_EPS) * g
    h = y * (1.0 + scale) + shift
    tt, b, d = x3.shape
    return h.reshape(tt * b, d).astype(BF16)


def _seq_edges(p, n_lat_tiles, n_tiles):
    is_first = jnp.logical_or(p == 0, p == n_lat_tiles)
    is_last = jnp.logical_or(p == n_lat_tiles - 1, p == n_tiles - 1)
    return is_first, is_last


def _k1_body(first, n_lat_tiles, n_tiles, *refs):
    n_tok = 3 if first else 1
    tok_refs = refs[:n_tok]
    (mod_ref, g_ref, wl_ref, wf_ref, wp_ref, cd_ref, cw_ref, cb_ref, xh_ref, v_ref, upool_ref,
     uprev, tail) = refs[n_tok:n_tok + 13]
    xs_ref = refs[n_tok + 13] if first else None
    j = pl.program_id(0)
    rows = uprev.shape[0]
    tail_rows = tail.shape[0]

    @pl.when(j == 0)
    def _():
        uprev[...] = jnp.zeros_like(uprev)
        tail[...] = jnp.zeros_like(tail)

    x3 = _load_tokens(first, j < n_lat_tiles, tok_refs, xs_ref)
    tt, bsz, _ = x3.shape
    h = _norm_mod(x3, g_ref[...], mod_ref[0, 0], mod_ref[0, 1])
    u = jnp.dot(h, wl_ref[...], preferred_element_type=F32)
    ufft = jnp.dot(h, wf_ref[...], preferred_element_type=F32).astype(BF16)
    v = jnp.dot(ufft, cd_ref[...], preferred_element_type=F32)
    w = v.shape[1] // 2
    v4 = jnp.stack([v[:, :w].reshape(tt, bsz, w), v[:, w:].reshape(tt, bsz, w)], axis=1)
    v_ref[...] = v4.reshape(2 * rows, w).astype(BF16)
    upool_ref[...] = jnp.dot(h, wp_ref[...], preferred_element_type=F32).astype(BF16)

    p_first, p_last = _seq_edges(j - 1, n_lat_tiles, n_tiles)
    up = uprev[...]
    before = jnp.where(p_first, 0.0, tail[...])
    after = jnp.where(p_last, 0.0, u[0:SUBLANES, :])
    ext = jnp.concatenate([before, up, after], axis=0)
    cwh = 0.5 * cw_ref[...]
    xh = 0.5 * cb_ref[...] + cwh[0:1] * ext[0:rows]
    for k in range(1, CONV_TAPS):
        xh = xh + cwh[k:k + 1] * ext[k * SUBLANES:k * SUBLANES + rows]
    xh_ref[...] = xh.astype(BF16)

    tail[...] = up[rows - tail_rows:rows, :]
    uprev[...] = u


def _k1(tokens, mod_k, g, wl, cd, conv_w, conv_b, first, n_all, n_lat_tiles, w_lru, w_fft, w_pool):
    d = g.shape[-1]
    bsz = SUBLANES
    r = TOKEN_TILE_ROWS
    tt = r // bsz
    n_tiles = n_all // r
    const2 = lambda j: (0, 0)
    off_fft, off_pool = 2 * w_lru, 2 * w_lru + 2 * w_fft
    assert off_fft % w_fft == 0 and off_pool % w_pool == 0
    clamp = lambda j: (jnp.minimum(j, n_tiles - 1), 0)
    in_specs = _token_specs(first, n_lat_tiles, n_tiles, tt, d, bsz) + [
        pl.BlockSpec((1, 3, bsz, d), lambda j: (jnp.where(j < n_lat_tiles, 0, 1), 0, 0, 0)),
        pl.BlockSpec((1, d), const2),
        _resident((d, w_lru), const2),
        _resident((d, w_fft), lambda j: (0, off_fft // w_fft)),
        _resident((d, w_pool), lambda j: (0, off_pool // w_pool)),
        _resident(cd.shape, const2),
        pl.BlockSpec((CONV_TAPS, w_lru), const2),
        pl.BlockSpec((1, w_lru), const2),
    ]
    scratch = [pltpu.VMEM((r, w_lru), F32), pltpu.VMEM(((CONV_TAPS - 2) * SUBLANES, w_lru), F32)]
    if first:
        scratch.append(pltpu.VMEM((tt, bsz, d), F32))
    return pl.pallas_call(
        functools.partial(_k1_body, first, n_lat_tiles, n_tiles),
        grid=(n_tiles + 1,),
        in_specs=in_specs,
        out_specs=[
            pl.BlockSpec((r, w_lru), lambda j: (jnp.maximum(j - 1, 0), 0)),
            pl.BlockSpec((2 * r, w_fft), clamp),
            pl.BlockSpec((r, w_pool), clamp),
        ],
        out_shape=[
            jax.ShapeDtypeStruct((n_all, w_lru), BF16),
            jax.ShapeDtypeStruct((2 * n_all, w_fft), BF16),
            jax.ShapeDtypeStruct((n_all, w_pool), BF16),
        ],
        scratch_shapes=scratch,
        compiler_params=_params(),
        name="norm_uproj",
    )(*tokens, mod_k, g, wl, wl, wl, cd, conv_w, conv_b.reshape(1, w_lru))


def _lru_chunk(d, j, n_lat_chunks, n_ctx_chunks):
    in_ctx = j < n_ctx_chunks
    jj = j - n_ctx_chunks
    ctx_ck = n_lat_chunks + jnp.where(d == 0, j, n_ctx_chunks - 1 - j)
    lat_ck = jnp.where(d == 0, jj, n_lat_chunks - 1 - jj)
    return jnp.where(in_ctx, ctx_ck, lat_ck)


def _lru_body(steps, xh_ref, wg_ref, ba_ref, bx_ref, lam_ref, o_ref, a_sc, b_sc, h_sc, state):
    d = pl.program_id(0)
    j = pl.program_id(1)
    cb = xh_ref.shape[1]

    @pl.when(j == 0)
    def _():
        state[...] = jnp.zeros_like(state)

    lam = lam_ref[0]
    half_rate = (-0.5 * LRU_POWER) * (jnp.maximum(-lam, 0.0) + jnp.log1p(jnp.exp(-jnp.abs(lam))))
    half_ba = 0.5 * ba_ref[0]
    half_bx = 0.5 * bx_ref[0]
    gw = wg_ref.shape[2]
    for g in range(cb // gw):
        sl = slice(g * gw, (g + 1) * gw)
        xb = xh_ref[:, sl]
        gates = jnp.dot(xb, wg_ref[0, g], preferred_element_type=F32)
        log_a = (1.0 + jnp.tanh(gates[:, :gw] + half_ba[:, sl])) * half_rate[:, sl]
        a = jnp.exp(log_a)
        q = jnp.tanh(log_a) * (-1.0 - a * a)
        gain = jnp.where(q > 0.0, q * lax.rsqrt(q), 0.0)
        a_sc[:, sl] = a
        b_sc[:, sl] = gain * ((1.0 + jnp.tanh(gates[:, gw:] + half_bx[:, sl])) * xb.astype(F32))

    n_blocks = steps // SCAN_UNROLL
    block_rows = SCAN_UNROLL * SUBLANES

    def scan(reverse):
        def block(i, h):
            base = pl.multiple_of((n_blocks - 1 - i if reverse else i) * block_rows, block_rows)
            for k in (range(SCAN_UNROLL - 1, -1, -1) if reverse else range(SCAN_UNROLL)):
                r0 = base + k * SUBLANES
                h = a_sc[pl.ds(r0, SUBLANES), :] * h + b_sc[pl.ds(r0, SUBLANES), :]
                h_sc[pl.ds(r0, SUBLANES), :] = h
            return h

        state[...] = lax.fori_loop(0, n_blocks, block, state[...])

    @pl.when(d == 0)
    def _():
        scan(False)

    @pl.when(d == 1)
    def _():
        scan(True)

    o_ref[0] = h_sc[...].astype(BF16)


def _lru(xh, wg, ba, bx, lam, n_lat_rows, n_ctx_rows, steps):
    n_all, w = xh.shape
    rc = steps * SUBLANES
    n_lat_chunks = n_lat_rows // rc
    n_ctx_chunks = n_ctx_rows // rc
    n_chunks = n_lat_chunks + n_ctx_chunks
    gw = wg.shape[2]
    ck = functools.partial(_lru_chunk, n_lat_chunks=n_lat_chunks, n_ctx_chunks=n_ctx_chunks)
    return pl.pallas_call(
        functools.partial(_lru_body, steps),
        grid=(2, n_chunks),
        in_specs=[
            pl.BlockSpec((rc, w), lambda d, j: (ck(d, j), 0)),
            pl.BlockSpec((1, w // gw, gw, 2 * gw), lambda d, j: (d, 0, 0, 0)),
            pl.BlockSpec((1, 1, w), lambda d, j: (d, 0, 0)),
            pl.BlockSpec((1, 1, w), lambda d, j: (d, 0, 0)),
            pl.BlockSpec((1, 1, w), lambda d, j: (d, 0, 0)),
        ],
        out_specs=pl.BlockSpec((1, rc, w), lambda d, j: (d, ck(d, j), 0)),
        out_shape=jax.ShapeDtypeStruct((2, n_all, w), BF16),
        scratch_shapes=[
            pltpu.VMEM((rc, w), F32),
            pltpu.VMEM((rc, w), F32),
            pltpu.VMEM((rc, w), F32),
            pltpu.VMEM((SUBLANES, w), F32),
        ],
        compiler_params=_params(2),
        name="rglru_scan",
    )(xh, wg, ba.reshape(2, 1, w), bx.reshape(2, 1, w), lam.reshape(2, 1, w))


def _dft_constants(n2):
    n1 = DFT_RADIX
    n = n1 * n2
    eye = jnp.eye(SUBLANES, dtype=F32)
    k2 = np.arange(n2)
    ang1 = 2.0 * np.pi * ((k2[:, None] * k2[None, :]) % n2) / n2
    c1, s1 = np.cos(ang1) / np.sqrt(n2), np.sin(ang1) / np.sqrt(n2)
    base1 = np.stack([np.stack([c1, s1], axis=-1), np.stack([-s1, c1], axis=-1)], axis=1)
    m1 = jnp.kron(jnp.asarray(base1.reshape(2 * n2, 2 * n2), F32), eye)
    k1 = np.arange(n1)
    ang2 = 2.0 * np.pi * ((k1[:, None] * k1[None, :]) % n1) / n1
    base2 = np.stack([np.cos(ang2), np.sin(ang2)], axis=-1) / np.sqrt(n1)
    m2 = jnp.kron(jnp.asarray(base2.reshape(n1, 2 * n1), F32), eye)
    angt = 2.0 * np.pi * ((k1[:, None] * k2[None, :]) % n) / n
    shape = (n1, n2, SUBLANES, LANES)

    def table(vals):
        return jnp.broadcast_to(jnp.asarray(vals, F32)[:, :, None, None], shape)

    return m1.astype(BF16), m2.astype(BF16), table(np.cos(angt)), table(np.sin(angt))


def _fft1_body(v_ref, m1_ref, twc_ref, tws_ref, z_ref):
    n2, _, grp, w = v_ref.shape
    xs = v_ref[...].reshape(n2 * grp, w)
    r = jnp.dot(m1_ref[...], xs, preferred_element_type=F32).reshape(n2, 2, SUBLANES, w)
    rr, ri = r[:, 0], r[:, 1]
    reps = w // LANES
    cw = jnp.concatenate([twc_ref[0]] * reps, axis=-1)
    sw = jnp.concatenate([tws_ref[0]] * reps, axis=-1)
    z = jnp.stack([rr * cw + ri * sw, ri * cw - rr * sw], axis=1)
    z_ref[...] = z.reshape(n2, 1, grp, w).astype(BF16)


def _fft2_body(z_ref, m2_ref, y_ref):
    pair, n1, grp, w = z_ref.shape
    ys = [jnp.dot(m2_ref[...], z_ref[i].reshape(n1 * grp, w), preferred_element_type=F32).reshape(n1, SUBLANES, w)
          for i in range(pair)]
    y_ref[...] = jnp.stack(ys, axis=1).reshape(n1, 1, grp, w).astype(BF16)


def _position_dft(v4, n2, block0):
    m1, m2, twc, tws = _dft_constants(n2)
    n1 = DFT_RADIX
    grp, w = v4.shape[2], v4.shape[3]
    z = pl.pallas_call(
        _fft1_body,
        grid=(n1,),
        in_specs=[
            pl.BlockSpec((n2, 1, grp, w), lambda a: (block0, a, 0, 0)),
            _resident(m1.shape, lambda a: (0, 0)),
            pl.BlockSpec((1, n2, SUBLANES, LANES), lambda a: (a, 0, 0, 0)),
            pl.BlockSpec((1, n2, SUBLANES, LANES), lambda a: (a, 0, 0, 0)),
        ],
        out_specs=pl.BlockSpec((n2, 1, grp, w), lambda a: (0, a, 0, 0)),
        out_shape=jax.ShapeDtypeStruct((n2, n1, grp, w), BF16),
        compiler_params=_params(),
        name="dft_stage1",
    )(v4, m1, twc, tws)
    pair = grp // SUBLANES
    y = pl.pallas_call(
        _fft2_body,
        grid=(n2 // pair,),
        in_specs=[
            pl.BlockSpec((pair, n1, grp, w), lambda c: (c, 0, 0, 0)),
            _resident(m2.shape, lambda c: (0, 0)),
        ],
        out_specs=pl.BlockSpec((n1, 1, grp, w), lambda c: (0, c, 0, 0)),
        out_shape=jax.ShapeDtypeStruct((n1, n2 // pair, grp, w), BF16),
        compiler_params=_params(),
        name="dft_stage2",
    )(z, m2)
    return y.reshape(n1 * n2 * SUBLANES, w)


def _pool_minus_token(ue, rows, t0, t_seq):
    halo = POOL_HALO_ROWS
    tvec = t0 + lax.broadcasted_iota(jnp.int32, (rows, LANES), 0) // SUBLANES
    parts = []
    for g, win in enumerate(POOL_WINDOW_SIZES):
        col = ue[:, g * LANES:(g + 1) * LANES]
        n = col.shape[0]
        acc = col[0:n - SUBLANES] + col[SUBLANES:n]
        e0 = SUBLANES
        span = 1
        while 2 * span < win:
            sh = span * SUBLANES
            n = acc.shape[0]
            acc = acc[0:n - 2 * sh] + acc[2 * sh:n]
            e0 += sh
            span *= 2
        wsum = acc[halo - e0:halo - e0 + rows]
        half = win // 2
        cnt = jnp.minimum(tvec + half, t_seq) - jnp.maximum(tvec - half, 0)
        parts.append(wsum / cnt.astype(F32) - col[halo:halo + rows])
    return jnp.concatenate(parts, axis=1)


def _k3_body(cfg, *refs):
    first, last, n_lat_tiles, n_tiles, t_lat, t_ctx = cfg
    n_tok = 3 if first else 1
    tok_refs = refs[:n_tok]
    refs = list(refs[n_tok:])
    mod_ref, g_ref, hf_ref, hb_ref, yl_ref = refs[:5]
    refs = refs[5:]
    yc_ref = None if last else refs.pop(0)
    (up_ref, upp_ref, upn_ref, wzl_ref, wzf_ref, wzp_ref, wg0_ref, wg1_ref, wg2_ref, pa_ref, pb_ref, pc_ref,
     wo_ref, fw_ref, pw_ref, ps_ref, fg_ref, o_ref) = refs[:18]
    xs_ref = refs[18] if first else None
    j = pl.program_id(0)
    is_lat = j < n_lat_tiles
    x3 = _load_tokens(first, is_lat, tok_refs, xs_ref)
    tt, bsz, d = x3.shape
    rows = tt * bsz
    h = _norm_mod(x3, g_ref[...], mod_ref[0, 0], mod_ref[0, 1])

    z_lru = jnp.dot(h, wzl_ref[...], preferred_element_type=F32)
    y_lru = hf_ref[0].astype(F32) + hb_ref[0].astype(F32)
    ya = jnp.dot((y_lru * _silu(z_lru)).astype(BF16), pa_ref[...], preferred_element_type=F32)

    z_fft = jnp.dot(h, wzf_ref[...], preferred_element_type=F32)
    f = yl_ref[...] if last else jnp.where(is_lat, yl_ref[...], yc_ref[...])
    y_fft = jnp.dot(f, fw_ref[...], preferred_element_type=F32)
    yb = jnp.dot((y_fft * _silu(z_fft)).astype(BF16), pb_ref[...], preferred_element_type=F32)

    z_pool = jnp.dot(h, wzp_ref[...], preferred_element_type=F32)
    seq_first, seq_last = _seq_edges(j, n_lat_tiles, n_tiles)
    ue = jnp.concatenate([
        upp_ref[...].astype(F32) * jnp.where(seq_first, 0.0, 1.0).astype(F32),
        up_ref[...].astype(F32),
        upn_ref[...].astype(F32) * jnp.where(seq_last, 0.0, 1.0).astype(F32)], axis=0)
    t0 = jnp.where(is_lat, j, j - n_lat_tiles) * tt
    t_seq = jnp.where(is_lat, t_lat, t_ctx)
    p = _pool_minus_token(ue, rows, t0, t_seq)
    y_pool = jnp.dot(p.astype(BF16), pw_ref[...], preferred_element_type=F32) * ps_ref[...]
    yc = jnp.dot((y_pool * _silu(z_pool)).astype(BF16), pc_ref[...], preferred_element_type=F32)

    m = _sigmoid(jnp.dot(h, wg0_ref[...], preferred_element_type=F32)) * ya
    m = m + _sigmoid(jnp.dot(h, wg1_ref[...], preferred_element_type=F32)) * yb
    m = m + _sigmoid(jnp.dot(h, wg2_ref[...], preferred_element_type=F32)) * yc
    out = jnp.dot(m.astype(BF16), wo_ref[...], preferred_element_type=F32)
    xn = x3 + mod_ref[0, 2] * out.reshape(tt, bsz, d)
    if last:
        ms = jnp.mean(xn * xn, axis=-1, keepdims=True)
        xn = xn * lax.rsqrt(ms + NORM_EPS) * fg_ref[...]
        o_ref[...] = jnp.transpose(xn, (1, 0, 2))
    else:
        o_ref[...] = xn.reshape(rows, d)


def _k3(tokens, mod_k, g, h_lru, y_lat, y_ctx, u_pool, wl, pa, pb, pc, wo, fw, pw, ps, fg, first, last,
        n_lat_tiles, t_lat, t_ctx):
    d = g.shape[-1]
    bsz = SUBLANES
    r = TOKEN_TILE_ROWS
    tt = r // bsz
    n_all = u_pool.shape[0]
    n_tiles = n_all // r
    grid_tiles = n_lat_tiles if last else n_tiles
    w_lru = h_lru.shape[2]
    w_fft = y_lat.shape[1]
    w_pool = u_pool.shape[1]
    hr = r // POOL_HALO_ROWS
    n_halo = n_all // POOL_HALO_ROWS
    cfg = (first, last, n_lat_tiles, n_tiles, t_lat, t_ctx)
    const2 = lambda j: (0, 0)
    off_zl, off_zf, off_zp = w_lru, 2 * w_lru + w_fft, 2 * w_lru + 2 * w_fft + w_pool
    off_g = 2 * w_lru + 2 * w_fft + 2 * w_pool
    assert off_zl % w_lru == 0 and off_zf % w_fft == 0 and off_zp % w_pool == 0 and off_g % d == 0
    in_specs = _token_specs(first, n_lat_tiles, n_tiles, tt, d, bsz) + [
        pl.BlockSpec((1, 3, bsz, d), lambda j: (jnp.where(j < n_lat_tiles, 0, 1), 0, 0, 0)),
        pl.BlockSpec((1, d), const2),
        pl.BlockSpec((1, r, w_lru), lambda j: (0, j, 0)),
        pl.BlockSpec((1, r, w_lru), lambda j: (1, j, 0)),
        pl.BlockSpec((r, w_fft), lambda j: (jnp.minimum(j, n_lat_tiles - 1), 0)),
    ]
    args = list(tokens) + [mod_k, g, h_lru, h_lru, y_lat]
    if not last:
        in_specs.append(pl.BlockSpec((r, w_fft), lambda j: (jnp.maximum(j - n_lat_tiles, 0), 0)))
        args.append(y_ctx)
    in_specs += [
        pl.BlockSpec((r, w_pool), lambda j: (j, 0)),
        pl.BlockSpec((POOL_HALO_ROWS, w_pool), lambda j: (jnp.maximum(j * hr - 1, 0), 0)),
        pl.BlockSpec((POOL_HALO_ROWS, w_pool), lambda j: (jnp.minimum((j + 1) * hr, n_halo - 1), 0)),
        _resident((d, w_lru), lambda j: (0, off_zl // w_lru)),
        _resident((d, w_fft), lambda j: (0, off_zf // w_fft)),
        _resident((d, w_pool), lambda j: (0, off_zp // w_pool)),
        _resident((d, d), lambda j: (0, off_g // d)),
        _resident((d, d), lambda j: (0, off_g // d + 1)),
        _resident((d, d), lambda j: (0, off_g // d + 2)),
    ]
    args += [u_pool, u_pool, u_pool, wl, wl, wl, wl, wl, wl]
    for wgt in (pa, pb, pc, wo, fw, pw, ps, fg):
        in_specs.append(_resident(wgt.shape, const2))
        args.append(wgt)
    if last:
        out_spec = pl.BlockSpec((bsz, tt, d), lambda j: (0, j, 0))
        out_shape = jax.ShapeDtypeStruct((bsz, t_lat, d), F32)
    else:
        out_spec = pl.BlockSpec((r, d), lambda j: (j, 0))
        out_shape = jax.ShapeDtypeStruct((n_all, d), F32)
    return pl.pallas_call(
        functools.partial(_k3_body, cfg),
        grid=(grid_tiles,),
        in_specs=in_specs,
        out_specs=out_spec,
        out_shape=out_shape,
        scratch_shapes=[pltpu.VMEM((tt, bsz, d), F32)] if first else [],
        compiler_params=_params(),
        name="merge_residual",
    )(*args)


def _position_code(n_tokens, d):
    rows = n_tokens // GRID_WIDTH
    quarter = d // 4
    omega = 1.0 / (POSITION_BASE ** (jnp.arange(quarter, dtype=F32) / quarter))

    def emb(n):
        ang = jnp.arange(n).astype(F32)[:, None] * omega[None, :]
        return jnp.concatenate([jnp.sin(ang), jnp.cos(ang)], axis=-1)

    row_code = jnp.repeat(emb(rows), GRID_WIDTH, axis=0)
    col_code = jnp.tile(emb(GRID_WIDTH), (rows, 1))
    return jnp.concatenate([row_code, col_code], axis=-1).astype(F32)


def _block_diag(w):
    g, n, _ = w.shape
    eye = jnp.eye(g, dtype=w.dtype)
    return jnp.einsum("gij,gk->gikj", w, eye).reshape(g * n, g * n)


def _channel_dft_matrix(groups, n):
    k = np.arange(n)
    ang = 2.0 * np.pi * ((k[:, None] * k[None, :]) % n) / n
    eye = jnp.eye(groups, dtype=F32)
    cd = jnp.concatenate([jnp.kron(eye, jnp.asarray(np.cos(ang) / np.sqrt(n), F32)),
                          jnp.kron(eye, jnp.asarray(-np.sin(ang) / np.sqrt(n), F32))], axis=1)
    return cd.astype(BF16)


def kernel(x, c, ctx, c_ctx, norm_g, ada_w, ada_b, w_in, conv_w, conv_b, lru_wa, lru_ba, lru_wx, lru_bx, lru_lam,
           fft_w, pool_w, pool_scale, proj_a, proj_b, proj_c, w_out, final_g):
    bsz, t_lat, d = x.shape
    t_ctx = ctx.shape[1]
    depth = w_in.shape[0]
    w_lru = conv_w.shape[-1]
    fft_groups, fft_dim = fft_w.shape[1], fft_w.shape[2]
    w_fft = fft_groups * fft_dim
    w_pool = pool_w.shape[1] * pool_w.shape[2]
    heads, head_dim = lru_wa.shape[2], lru_wa.shape[3]
    assert bsz == SUBLANES and fft_dim == LANES and pool_w.shape[2] == LANES and conv_w.shape[1] == CONV_TAPS
    assert t_lat % DFT_RADIX == 0 and t_ctx % DFT_RADIX == 0 and (t_lat // DFT_RADIX) % (t_ctx // DFT_RADIX) == 0
    n_lat_rows, n_ctx_rows = t_lat * bsz, t_ctx * bsz
    n_all = n_lat_rows + n_ctx_rows
    assert n_lat_rows % TOKEN_TILE_ROWS == 0 and n_ctx_rows % TOKEN_TILE_ROWS == 0
    n_lat_tiles = n_lat_rows // TOKEN_TILE_ROWS
    lru_steps = min(LRU_CHUNK_STEPS, t_ctx)
    heads_per_group = LRU_GATE_WIDTH // head_dim

    pos3 = _position_code(t_lat, d).reshape(t_lat, 1, d)
    cc = jnp.concatenate([c, jnp.broadcast_to(c_ctx[None, :], (bsz, d))], axis=0)
    mod = _modulation(cc, ada_w, ada_b)
    cd = _channel_dft_matrix(fft_groups, fft_dim)
    n2_lat, n2_ctx = t_lat // DFT_RADIX, t_ctx // DFT_RADIX

    def gate_w(w):
        wd = w.reshape(2 * heads // heads_per_group, heads_per_group, head_dim, head_dim)
        return jax.vmap(_block_diag)(wd).reshape(2, heads // heads_per_group, LRU_GATE_WIDTH, LRU_GATE_WIDTH)

    tokens = (x, ctx, pos3)
    out = None
    for l in range(depth):
        first, last = l == 0, l == depth - 1
        mod_k = mod[l].reshape(2, bsz, 3, d).transpose(0, 2, 1, 3)
        g = norm_g[l].reshape(1, d)
        wl = w_in[l].astype(BF16)
        xh, v, u_pool = _k1(tokens, mod_k, g, wl, cd, conv_w[l], conv_b[l], first, n_all, n_lat_tiles,
                            w_lru, w_fft, w_pool)

        wg = jnp.concatenate([gate_w(lru_wa[l]), gate_w(lru_wx[l])], axis=-1).astype(BF16)
        h_lru = _lru(xh, wg, lru_ba[l], lru_bx[l], lru_lam[l], n_lat_rows, n_ctx_rows, lru_steps)

        v4 = v.reshape(n_all // (DFT_RADIX * bsz), DFT_RADIX, 2 * bsz, w_fft)
        y_lat = _position_dft(v4, n2_lat, 0)
        y_ctx = None if last else _position_dft(v4, n2_ctx, n2_lat // n2_ctx)

        out = _k3(tokens, mod_k, g, h_lru, y_lat, y_ctx, u_pool, wl,
                  proj_a[l].astype(BF16), proj_b[l].astype(BF16), proj_c[l].astype(BF16), w_out[l].astype(BF16),
                  _block_diag(fft_w[l]).astype(BF16), _block_diag(pool_w[l]).astype(BF16),
                  pool_scale[l].reshape(1, w_pool), final_g.reshape(1, d), first, last, n_lat_tiles, t_lat, t_ctx)
        tokens = (out,)
    return out
```

```python
import functools

import numpy as np
import jax
import jax.numpy as jnp
from jax import lax
from jax.experimental import pallas as pl
from jax.experimental.pallas import tpu as pltpu

F32 = jnp.float32
BF16 = jnp.bfloat16

GRID_WIDTH = 64
LRU_POWER = 8.0
NORM_EPS = 1e-6
POSITION_BASE = 10000.0
POOL_WINDOW_SIZES = (2, 4, 8, 16)
CONV_TAPS = 4
SUBLANES = 8
LANES = 128
DFT_RADIX = 64
DFT_GROUP_ROWS = 2 * SUBLANES
POOL_HALO_ROWS = 64
TOKEN_TILE_ROWS = DFT_RADIX * SUBLANES
LRU_CHUNK_STEPS = 128
LRU_GATE_WIDTH = 256
SCAN_UNROLL = 8
VMEM_LIMIT_BYTES = 56 * 1024 * 1024


def _half_tanh_sigmoid(x_half):
    return 0.5 * (1.0 + jnp.tanh(x_half))


def _sigmoid(x):
    return _half_tanh_sigmoid(0.5 * x)


def _silu(x):
    return x * _sigmoid(x)


def _params(n_axes=1):
    return pltpu.CompilerParams(dimension_semantics=("arbitrary",) * n_axes, vmem_limit_bytes=VMEM_LIMIT_BYTES)


def _resident(shape, index_map):
    return pl.BlockSpec(shape, index_map, pipeline_mode=pl.Buffered(1))


def _mod_body(cc_ref, w_ref, b_ref, o_ref):
    s = _silu(cc_ref[...]).astype(BF16)
    o_ref[0] = jnp.dot(s, w_ref[0], preferred_element_type=F32) + b_ref[0]


def _modulation(cc, ada_w, ada_b):
    depth, d, n = ada_w.shape
    tn = 1024
    return pl.pallas_call(
        _mod_body,
        grid=(depth, n // tn),
        in_specs=[
            pl.BlockSpec((2 * SUBLANES, d), lambda l, i: (0, 0)),
            pl.BlockSpec((1, d, tn), lambda l, i: (l, 0, i)),
            pl.BlockSpec((1, 1, tn), lambda l, i: (l, 0, i)),
        ],
        out_specs=pl.BlockSpec((1, 2 * SUBLANES, tn), lambda l, i: (l, 0, i)),
        out_shape=jax.ShapeDtypeStruct((depth, 2 * SUBLANES, n), F32),
        compiler_params=_params(2),
        name="adaln_mod",
    )(cc, ada_w.astype(BF16), ada_b.reshape(depth, 1, n))


def _rows_body(x_ref, o_ref):
    bsz, tt, d = x_ref.shape
    o_ref[...] = jnp.transpose(x_ref[...], (1, 0, 2)).reshape(tt * bsz, d)


def _batch_major_to_rows(x):
    bsz, t, d = x.shape
    tt = TOKEN_TILE_ROWS // bsz
    return pl.pallas_call(
        _rows_body,
        grid=(t // tt,),
        in_specs=[pl.BlockSpec((bsz, tt, d), lambda j: (0, j, 0))],
        out_specs=pl.BlockSpec((tt * bsz, d), lambda j: (j, 0)),
        out_shape=jax.ShapeDtypeStruct((t * bsz, d), x.dtype),
        compiler_params=_params(),
        name="ctx_rows",
    )(x)


def _seq_edges(p, n_lat_tiles, n_tiles):
    is_first = jnp.logical_or(p == 0, p == n_lat_tiles)
    is_last = jnp.logical_or(p == n_lat_tiles - 1, p == n_tiles - 1)
    return is_first, is_last


def _k1_body(first, n_lat_tiles, n_tiles, *refs):
    n_tok = 3 if first else 1
    tok_refs = refs[:n_tok]
    refs = list(refs[n_tok:])
    mod_ref, g_ref, wl_ref, wf_ref, wp_ref, cd_ref, cw_ref, cb_ref, xh_ref, v_ref, upool_ref, h_ref = refs[:12]
    refs = refs[12:]
    x0_ref = refs.pop(0) if first else None
    uprev, tail = refs
    j = pl.program_id(0)
    rows, d = h_ref.shape
    bsz = SUBLANES
    tt = rows // bsz
    tail_rows = tail.shape[0]

    @pl.when(j == 0)
    def _():
        uprev[...] = jnp.zeros_like(uprev)
        tail[...] = jnp.zeros_like(tail)

    if first:
        x_ref, ctx_ref, pos_ref = tok_refs
        x3 = jnp.where(j < n_lat_tiles, jnp.transpose(x_ref[...], (1, 0, 2)) + pos_ref[...],
                       ctx_ref[...].reshape(tt, bsz, d))
        x0_ref[...] = x3.reshape(rows, d)
    else:
        x3 = tok_refs[0][...].reshape(tt, bsz, d)
    ms = jnp.mean(x3 * x3, axis=-1, keepdims=True)
    hn = x3 * lax.rsqrt(ms + NORM_EPS) * g_ref[...]
    h = (hn * (1.0 + mod_ref[0, 1]) + mod_ref[0, 0]).reshape(rows, d).astype(BF16)
    h_ref[...] = h
    u = jnp.dot(h, wl_ref[...], preferred_element_type=F32)
    ufft = jnp.dot(h, wf_ref[...], preferred_element_type=F32).astype(BF16)
    v = jnp.dot(ufft, cd_ref[...], preferred_element_type=F32)
    w = v.shape[1] // 2
    v4 = jnp.stack([v[:, :w].reshape(tt, bsz, w), v[:, w:].reshape(tt, bsz, w)], axis=1)
    v_ref[...] = v4.reshape(1, tt, DFT_GROUP_ROWS, w).astype(BF16)
    upool_ref[...] = jnp.dot(h, wp_ref[...], preferred_element_type=F32).astype(BF16)

    p_first, p_last = _seq_edges(j - 1, n_lat_tiles, n_tiles)
    up = uprev[...]
    before = jnp.where(p_first, 0.0, tail[...])
    after = jnp.where(p_last, 0.0, u[0:SUBLANES, :])
    ext = jnp.concatenate([before, up, after], axis=0)
    cwh = 0.5 * cw_ref[...]
    xh = 0.5 * cb_ref[...] + cwh[0:1] * ext[0:rows]
    for k in range(1, CONV_TAPS):
        xh = xh + cwh[k:k + 1] * ext[k * SUBLANES:k * SUBLANES + rows]
    xh_ref[...] = xh.astype(BF16)

    tail[...] = up[rows - tail_rows:rows, :]
    uprev[...] = u


def _k1(tokens, mod_k, g, wl, cd, conv_w, conv_b, first, n_all, n_lat_tiles, w_lru, w_fft, w_pool):
    d = g.shape[-1]
    bsz = SUBLANES
    r = TOKEN_TILE_ROWS
    tt = r // bsz
    n_tiles = n_all // r
    n_ctx_tiles = n_tiles - n_lat_tiles
    const2 = lambda j: (0, 0)
    off_fft, off_pool = 2 * w_lru, 2 * w_lru + 2 * w_fft
    assert off_fft % w_fft == 0 and off_pool % w_pool == 0
    clamp = lambda j: (jnp.minimum(j, n_tiles - 1), 0)
    if first:
        tok_specs = [
            pl.BlockSpec((bsz, tt, d), lambda j: (0, jnp.minimum(j, n_lat_tiles - 1), 0)),
            pl.BlockSpec((r, d), lambda j: (jnp.clip(j - n_lat_tiles, 0, n_ctx_tiles - 1), 0)),
            pl.BlockSpec((tt, 1, d), lambda j: (jnp.minimum(j, n_lat_tiles - 1), 0, 0)),
        ]
    else:
        tok_specs = [pl.BlockSpec((r, d), clamp)]
    in_specs = tok_specs + [
        pl.BlockSpec((1, 3, bsz, d), lambda j: (jnp.where(j < n_lat_tiles, 0, 1), 0, 0, 0)),
        pl.BlockSpec((1, d), const2),
        _resident((d, w_lru), const2),
        _resident((d, w_fft), lambda j: (0, off_fft // w_fft)),
        _resident((d, w_pool), lambda j: (0, off_pool // w_pool)),
        _resident(cd.shape, const2),
        pl.BlockSpec((CONV_TAPS, w_lru), const2),
        pl.BlockSpec((1, w_lru), const2),
    ]
    out_specs = [
        pl.BlockSpec((r, w_lru), lambda j: (jnp.maximum(j - 1, 0), 0)),
        pl.BlockSpec((1, tt, DFT_GROUP_ROWS, w_fft), lambda j: (jnp.minimum(j, n_tiles - 1), 0, 0, 0)),
        pl.BlockSpec((r, w_pool), clamp),
        pl.BlockSpec((r, d), clamp),
    ]
    out_shape = [
        jax.ShapeDtypeStruct((n_all, w_lru), BF16),
        jax.ShapeDtypeStruct((n_tiles, tt, DFT_GROUP_ROWS, w_fft), BF16),
        jax.ShapeDtypeStruct((n_all, w_pool), BF16),
        jax.ShapeDtypeStruct((n_all, d), BF16),
    ]
    if first:
        out_specs.append(pl.BlockSpec((r, d), clamp))
        out_shape.append(jax.ShapeDtypeStruct((n_all, d), F32))
    return pl.pallas_call(
        functools.partial(_k1_body, first, n_lat_tiles, n_tiles),
        grid=(n_tiles + 1,),
        in_specs=in_specs,
        out_specs=out_specs,
        out_shape=out_shape,
        scratch_shapes=[pltpu.VMEM((r, w_lru), F32), pltpu.VMEM(((CONV_TAPS - 2) * SUBLANES, w_lru), F32)],
        compiler_params=_params(),
        name="norm_uproj",
    )(*tokens, mod_k, g, wl, wl, wl, cd, conv_w, conv_b.reshape(1, w_lru))


def _lru_chunk(d, j, n_lat_chunks, n_ctx_chunks):
    in_ctx = j < n_ctx_chunks
    jj = j - n_ctx_chunks
    ctx_ck = n_lat_chunks + jnp.where(d == 0, j, n_ctx_chunks - 1 - j)
    lat_ck = jnp.where(d == 0, jj, n_lat_chunks - 1 - jj)
    return jnp.where(in_ctx, ctx_ck, lat_ck)


def _lru_body(steps, xh_ref, wg_ref, ba_ref, bx_ref, lam_ref, o_ref, a_sc, b_sc, h_sc, state):
    d = pl.program_id(0)
    j = pl.program_id(1)
    cb = xh_ref.shape[1]

    @pl.when(j == 0)
    def _():
        state[...] = jnp.zeros_like(state)

    lam = lam_ref[0]
    half_rate = (-0.5 * LRU_POWER) * (jnp.maximum(-lam, 0.0) + jnp.log1p(jnp.exp(-jnp.abs(lam))))
    half_ba = 0.5 * ba_ref[0]
    half_bx = 0.5 * bx_ref[0]
    gw = wg_ref.shape[2]
    for g in range(cb // gw):
        sl = slice(g * gw, (g + 1) * gw)
        xb = xh_ref[:, sl]
        gates = jnp.dot(xb, wg_ref[0, g], preferred_element_type=F32)
        log_a = (1.0 + jnp.tanh(gates[:, :gw] + half_ba[:, sl])) * half_rate[:, sl]
        a = jnp.exp(log_a)
        q = jnp.tanh(log_a) * (-1.0 - a * a)
        gain = jnp.where(q > 0.0, q * lax.rsqrt(q), 0.0)
        a_sc[:, sl] = a
        b_sc[:, sl] = gain * ((1.0 + jnp.tanh(gates[:, gw:] + half_bx[:, sl])) * xb.astype(F32))

    n_blocks = steps // SCAN_UNROLL
    block_rows = SCAN_UNROLL * SUBLANES

    def scan(reverse):
        def block(i, h):
            base = pl.multiple_of((n_blocks - 1 - i if reverse else i) * block_rows, block_rows)
            for k in (range(SCAN_UNROLL - 1, -1, -1) if reverse else range(SCAN_UNROLL)):
                r0 = base + k * SUBLANES
                h = a_sc[pl.ds(r0, SUBLANES), :] * h + b_sc[pl.ds(r0, SUBLANES), :]
                h_sc[pl.ds(r0, SUBLANES), :] = h
            return h

        state[...] = lax.fori_loop(0, n_blocks, block, state[...])

    @pl.when(d == 0)
    def _():
        scan(False)

    @pl.when(d == 1)
    def _():
        scan(True)

    o_ref[0] = h_sc[...].astype(BF16)


def _lru(xh, wg, ba, bx, lam, n_lat_rows, n_ctx_rows, steps):
    n_all, w = xh.shape
    rc = steps * SUBLANES
    n_lat_chunks = n_lat_rows // rc
    n_ctx_chunks = n_ctx_rows // rc
    n_chunks = n_lat_chunks + n_ctx_chunks
    gw = wg.shape[2]
    ck = functools.partial(_lru_chunk, n_lat_chunks=n_lat_chunks, n_ctx_chunks=n_ctx_chunks)
    return pl.pallas_call(
        functools.partial(_lru_body, steps),
        grid=(2, n_chunks),
        in_specs=[
            pl.BlockSpec((rc, w), lambda d, j: (ck(d, j), 0)),
            pl.BlockSpec((1, w // gw, gw, 2 * gw), lambda d, j: (d, 0, 0, 0)),
            pl.BlockSpec((1, 1, w), lambda d, j: (d, 0, 0)),
            pl.BlockSpec((1, 1, w), lambda d, j: (d, 0, 0)),
            pl.BlockSpec((1, 1, w), lambda d, j: (d, 0, 0)),
        ],
        out_specs=pl.BlockSpec((1, rc, w), lambda d, j: (d, ck(d, j), 0)),
        out_shape=jax.ShapeDtypeStruct((2, n_all, w), BF16),
        scratch_shapes=[
            pltpu.VMEM((rc, w), F32),
            pltpu.VMEM((rc, w), F32),
            pltpu.VMEM((rc, w), F32),
            pltpu.VMEM((SUBLANES, w), F32),
        ],
        compiler_params=_params(2),
        name="rglru_scan",
    )(xh, wg, ba.reshape(2, 1, w), bx.reshape(2, 1, w), lam.reshape(2, 1, w))


def _dft_constants(n2):
    n1 = DFT_RADIX
    n = n1 * n2
    eye = jnp.eye(SUBLANES, dtype=F32)
    k2 = np.arange(n2)
    ang1 = 2.0 * np.pi * ((k2[:, None] * k2[None, :]) % n2) / n2
    c1, s1 = np.cos(ang1) / np.sqrt(n2), np.sin(ang1) / np.sqrt(n2)
    base1 = np.stack([np.stack([c1, s1], axis=-1), np.stack([-s1, c1], axis=-1)], axis=1)
    m1 = jnp.kron(jnp.asarray(base1.reshape(2 * n2, 2 * n2), F32), eye)
    k1 = np.arange(n1)
    ang2 = 2.0 * np.pi * ((k1[:, None] * k1[None, :]) % n1) / n1
    base2 = np.stack([np.cos(ang2), np.sin(ang2)], axis=-1) / np.sqrt(n1)
    m2 = jnp.kron(jnp.asarray(base2.reshape(n1, 2 * n1), F32), eye)
    angt = 2.0 * np.pi * ((k1[:, None] * k2[None, :]) % n) / n
    shape = (n1, n2, SUBLANES, LANES)

    def table(vals):
        return jnp.broadcast_to(jnp.asarray(vals, F32)[:, :, None, None], shape)

    return m1.astype(BF16), m2.astype(BF16), table(np.cos(angt)), table(np.sin(angt))


def _fft1_body(v_ref, m1_ref, twc_ref, tws_ref, z_ref):
    n2, _, grp, w = v_ref.shape
    xs = v_ref[...].reshape(n2 * grp, w)
    r = jnp.dot(m1_ref[...], xs, preferred_element_type=F32).reshape(n2, 2, SUBLANES, w)
    rr, ri = r[:, 0], r[:, 1]
    reps = w // LANES
    cw = jnp.concatenate([twc_ref[0]] * reps, axis=-1)
    sw = jnp.concatenate([tws_ref[0]] * reps, axis=-1)
    z = jnp.stack([rr * cw + ri * sw, ri * cw - rr * sw], axis=1)
    z_ref[...] = z.reshape(n2, 1, grp, w).astype(BF16)


def _fft2_body(z_ref, m2_ref, y_ref):
    pair, n1, grp, w = z_ref.shape
    ys = [jnp.dot(m2_ref[...], z_ref[i].reshape(n1 * grp, w), preferred_element_type=F32).reshape(n1, SUBLANES, w)
          for i in range(pair)]
    y_ref[...] = jnp.stack(ys, axis=1).reshape(n1, 1, grp, w).astype(BF16)


def _position_dft(v4, n2, block0):
    m1, m2, twc, tws = _dft_constants(n2)
    n1 = DFT_RADIX
    grp, w = v4.shape[2], v4.shape[3]
    z = pl.pallas_call(
        _fft1_body,
        grid=(n1,),
        in_specs=[
            pl.BlockSpec((n2, 1, grp, w), lambda a: (block0, a, 0, 0)),
            _resident(m1.shape, lambda a: (0, 0)),
            pl.BlockSpec((1, n2, SUBLANES, LANES), lambda a: (a, 0, 0, 0)),
            pl.BlockSpec((1, n2, SUBLANES, LANES), lambda a: (a, 0, 0, 0)),
        ],
        out_specs=pl.BlockSpec((n2, 1, grp, w), lambda a: (0, a, 0, 0)),
        out_shape=jax.ShapeDtypeStruct((n2, n1, grp, w), BF16),
        compiler_params=_params(),
        name="dft_stage1",
    )(v4, m1, twc, tws)
    pair = grp // SUBLANES
    return pl.pallas_call(
        _fft2_body,
        grid=(n2 // pair,),
        in_specs=[
            pl.BlockSpec((pair, n1, grp, w), lambda c: (c, 0, 0, 0)),
            _resident(m2.shape, lambda c: (0, 0)),
        ],
        out_specs=pl.BlockSpec((n1, 1, grp, w), lambda c: (0, c, 0, 0)),
        out_shape=jax.ShapeDtypeStruct((n1, n2 // pair, grp, w), BF16),
        compiler_params=_params(),
        name="dft_stage2",
    )(z, m2)


def _pool_minus_token(ue, rows, t0, t_seq):
    halo = POOL_HALO_ROWS
    tvec = t0 + lax.broadcasted_iota(jnp.int32, (rows, LANES), 0) // SUBLANES
    parts = []
    for g, win in enumerate(POOL_WINDOW_SIZES):
        col = ue[:, g * LANES:(g + 1) * LANES]
        n = col.shape[0]
        acc = col[0:n - SUBLANES] + col[SUBLANES:n]
        e0 = SUBLANES
        span = 1
        while 2 * span < win:
            sh = span * SUBLANES
            n = acc.shape[0]
            acc = acc[0:n - 2 * sh] + acc[2 * sh:n]
            e0 += sh
            span *= 2
        wsum = acc[halo - e0:halo - e0 + rows]
        half = win // 2
        cnt = jnp.minimum(tvec + half, t_seq) - jnp.maximum(tvec - half, 0)
        parts.append(wsum / cnt.astype(F32) - col[halo:halo + rows])
    return jnp.concatenate(parts, axis=1)


def _k3_body(cfg, *refs):
    last, n_lat_tiles, n_tiles, t_lat, t_ctx = cfg
    refs = list(refs)
    x_ref, h_ref, mod_ref, hf_ref, hb_ref, yl_ref = refs[:6]
    refs = refs[6:]
    yc_ref = None if last else refs.pop(0)
    (up_ref, upp_ref, upn_ref, wzl_ref, wzf_ref, wzp_ref, wg0_ref, wg1_ref, wg2_ref, pa_ref, pb_ref, pc_ref,
     wo_ref, fw_ref, pw_ref, ps_ref, fg_ref, o_ref) = refs
    j = pl.program_id(0)
    is_lat = j < n_lat_tiles
    rows, d = x_ref.shape
    bsz = SUBLANES
    tt = rows // bsz
    h = h_ref[...]

    z_lru = jnp.dot(h, wzl_ref[...], preferred_element_type=F32)
    y_lru = hf_ref[0].astype(F32) + hb_ref[0].astype(F32)
    ya = jnp.dot((y_lru * _silu(z_lru)).astype(BF16), pa_ref[...], preferred_element_type=F32)

    z_fft = jnp.dot(h, wzf_ref[...], preferred_element_type=F32)
    w = yl_ref.shape[-1]
    f = yl_ref[...].reshape(rows, w)
    if not last:
        f = jnp.where(is_lat, f, yc_ref[...].reshape(rows, w))
    y_fft = jnp.dot(f, fw_ref[...], preferred_element_type=F32)
    yb = jnp.dot((y_fft * _silu(z_fft)).astype(BF16), pb_ref[...], preferred_element_type=F32)

    z_pool = jnp.dot(h, wzp_ref[...], preferred_element_type=F32)
    seq_first, seq_last = _seq_edges(j, n_lat_tiles, n_tiles)
    ue = jnp.concatenate([
        upp_ref[...].astype(F32) * jnp.where(seq_first, 0.0, 1.0).astype(F32),
        up_ref[...].astype(F32),
        upn_ref[...].astype(F32) * jnp.where(seq_last, 0.0, 1.0).astype(F32)], axis=0)
    t0 = jnp.where(is_lat, j, j - n_lat_tiles) * tt
    t_seq = jnp.where(is_lat, t_lat, t_ctx)
    p = _pool_minus_token(ue, rows, t0, t_seq)
    y_pool = jnp.dot(p.astype(BF16), pw_ref[...], preferred_element_type=F32) * ps_ref[...]
    yc = jnp.dot((y_pool * _silu(z_pool)).astype(BF16), pc_ref[...], preferred_element_type=F32)

    m = _sigmoid(jnp.dot(h, wg0_ref[...], preferred_element_type=F32)) * ya
    m = m + _sigmoid(jnp.dot(h, wg1_ref[...], preferred_element_type=F32)) * yb
    m = m + _sigmoid(jnp.dot(h, wg2_ref[...], preferred_element_type=F32)) * yc
    out = jnp.dot(m.astype(BF16), wo_ref[...], preferred_element_type=F32)
    xn = x_ref[...].reshape(tt, bsz, d) + mod_ref[0, 2] * out.reshape(tt, bsz, d)
    if last:
        ms = jnp.mean(xn * xn, axis=-1, keepdims=True)
        xn = xn * lax.rsqrt(ms + NORM_EPS) * fg_ref[...]
        o_ref[...] = jnp.transpose(xn, (1, 0, 2))
    else:
        o_ref[...] = xn.reshape(rows, d)


def _k3(x_rows, h, mod_k, h_lru, y_lat, y_ctx, u_pool, wl, pa, pb, pc, wo, fw, pw, ps, fg, last,
        n_lat_tiles, t_lat, t_ctx):
    n_all, d = h.shape
    bsz = SUBLANES
    r = TOKEN_TILE_ROWS
    tt = r // bsz
    n_tiles = n_all // r
    grid_tiles = n_lat_tiles if last else n_tiles
    w_lru = h_lru.shape[2]
    w_fft = y_lat.shape[-1]
    w_pool = u_pool.shape[1]
    hr = r // POOL_HALO_ROWS
    n_halo = n_all // POOL_HALO_ROWS
    cfg = (last, n_lat_tiles, n_tiles, t_lat, t_ctx)
    const2 = lambda j: (0, 0)
    off_zl, off_zf, off_zp = w_lru, 2 * w_lru + w_fft, 2 * w_lru + 2 * w_fft + w_pool
    off_g = 2 * w_lru + 2 * w_fft + 2 * w_pool
    assert off_zl % w_lru == 0 and off_zf % w_fft == 0 and off_zp % w_pool == 0 and off_g % d == 0

    def dft_spec(y4, first_tile):
        n2 = y4.shape[1] * y4.shape[2] // bsz
        assert tt % n2 == 0
        blk = (tt // n2,) + y4.shape[1:]
        n_blk = y4.shape[0] // blk[0]
        return pl.BlockSpec(blk, lambda j: (jnp.clip(j - first_tile, 0, n_blk - 1), 0, 0, 0))

    row_spec = lambda width: pl.BlockSpec((r, width), lambda j: (j, 0))
    pair_spec = lambda k: pl.BlockSpec((1, r, w_lru), lambda j: (k, j, 0))
    in_specs = [
        row_spec(d), row_spec(d),
        pl.BlockSpec((1, 3, bsz, d), lambda j: (jnp.where(j < n_lat_tiles, 0, 1), 0, 0, 0)),
        pair_spec(0), pair_spec(1),
        dft_spec(y_lat, 0),
    ]
    args = [x_rows, h, mod_k, h_lru, h_lru, y_lat]
    if not last:
        in_specs.append(dft_spec(y_ctx, n_lat_tiles))
        args.append(y_ctx)
    in_specs += [
        row_spec(w_pool),
        pl.BlockSpec((POOL_HALO_ROWS, w_pool), lambda j: (jnp.maximum(j * hr - 1, 0), 0)),
        pl.BlockSpec((POOL_HALO_ROWS, w_pool), lambda j: (jnp.minimum((j + 1) * hr, n_halo - 1), 0)),
        _resident((d, w_lru), lambda j: (0, off_zl // w_lru)),
        _resident((d, w_fft), lambda j: (0, off_zf // w_fft)),
        _resident((d, w_pool), lambda j: (0, off_zp // w_pool)),
        _resident((d, d), lambda j: (0, off_g // d)),
        _resident((d, d), lambda j: (0, off_g // d + 1)),
        _resident((d, d), lambda j: (0, off_g // d + 2)),
    ]
    args += [u_pool, u_pool, u_pool, wl, wl, wl, wl, wl, wl]
    for wgt in (pa, pb, pc, wo, fw, pw, ps, fg):
        in_specs.append(_resident(wgt.shape, const2))
        args.append(wgt)
    if last:
        out_spec = pl.BlockSpec((bsz, tt, d), lambda j: (0, j, 0))
        out_shape = jax.ShapeDtypeStruct((bsz, t_lat, d), F32)
    else:
        out_spec = row_spec(d)
        out_shape = jax.ShapeDtypeStruct((n_all, d), F32)
    return pl.pallas_call(
        functools.partial(_k3_body, cfg),
        grid=(grid_tiles,),
        in_specs=in_specs,
        out_specs=out_spec,
        out_shape=out_shape,
        compiler_params=_params(),
        name="merge_residual",
    )(*args)


def _position_code(n_tokens, d):
    rows = n_tokens // GRID_WIDTH
    quarter = d // 4
    omega = 1.0 / (POSITION_BASE ** (jnp.arange(quarter, dtype=F32) / quarter))

    def emb(n):
        ang = jnp.arange(n).astype(F32)[:, None] * omega[None, :]
        return jnp.concatenate([jnp.sin(ang), jnp.cos(ang)], axis=-1)

    row_code = jnp.repeat(emb(rows), GRID_WIDTH, axis=0)
    col_code = jnp.tile(emb(GRID_WIDTH), (rows, 1))
    return jnp.concatenate([row_code, col_code], axis=-1).astype(F32)


def _block_diag(w):
    g, n, _ = w.shape
    eye = jnp.eye(g, dtype=w.dtype)
    return jnp.einsum("gij,gk->gikj", w, eye).reshape(g * n, g * n)


def _channel_dft_matrix(groups, n):
    k = np.arange(n)
    ang = 2.0 * np.pi * ((k[:, None] * k[None, :]) % n) / n
    eye = jnp.eye(groups, dtype=F32)
    cd = jnp.concatenate([jnp.kron(eye, jnp.asarray(np.cos(ang) / np.sqrt(n), F32)),
                          jnp.kron(eye, jnp.asarray(-np.sin(ang) / np.sqrt(n), F32))], axis=1)
    return cd.astype(BF16)


def kernel(x, c, ctx, c_ctx, norm_g, ada_w, ada_b, w_in, conv_w, conv_b, lru_wa, lru_ba, lru_wx, lru_bx, lru_lam,
           fft_w, pool_w, pool_scale, proj_a, proj_b, proj_c, w_out, final_g):
    bsz, t_lat, d = x.shape
    t_ctx = ctx.shape[1]
    depth = w_in.shape[0]
    w_lru = conv_w.shape[-1]
    fft_groups, fft_dim = fft_w.shape[1], fft_w.shape[2]
    w_fft = fft_groups * fft_dim
    w_pool = pool_w.shape[1] * pool_w.shape[2]
    heads, head_dim = lru_wa.shape[2], lru_wa.shape[3]
    assert bsz == SUBLANES and fft_dim == LANES and pool_w.shape[2] == LANES and conv_w.shape[1] == CONV_TAPS
    assert t_lat % DFT_RADIX == 0 and t_ctx % DFT_RADIX == 0 and (t_lat // DFT_RADIX) % (t_ctx // DFT_RADIX) == 0
    n_lat_rows, n_ctx_rows = t_lat * bsz, t_ctx * bsz
    n_all = n_lat_rows + n_ctx_rows
    n_lat_tiles = n_lat_rows // TOKEN_TILE_ROWS
    lru_steps = min(LRU_CHUNK_STEPS, t_ctx)
    heads_per_group = LRU_GATE_WIDTH // head_dim

    pos3 = _position_code(t_lat, d).reshape(t_lat, 1, d)
    cc = jnp.concatenate([c, jnp.broadcast_to(c_ctx[None, :], (bsz, d))], axis=0)
    mod = _modulation(cc, ada_w, ada_b)
    cd = _channel_dft_matrix(fft_groups, fft_dim)
    n2_lat, n2_ctx = t_lat // DFT_RADIX, t_ctx // DFT_RADIX

    def gate_w(w):
        wd = w.reshape(2 * heads // heads_per_group, heads_per_group, head_dim, head_dim)
        return jax.vmap(_block_diag)(wd).reshape(2, heads // heads_per_group, LRU_GATE_WIDTH, LRU_GATE_WIDTH)

    tokens = (x, _batch_major_to_rows(ctx), pos3)
    out = None
    for l in range(depth):
        first, last = l == 0, l == depth - 1
        mod_k = mod[l].reshape(2, bsz, 3, d).transpose(0, 2, 1, 3)
        g = norm_g[l].reshape(1, d)
        wl = w_in[l].astype(BF16)
        k1_out = _k1(tokens, mod_k, g, wl, cd, conv_w[l], conv_b[l], first, n_all, n_lat_tiles,
                     w_lru, w_fft, w_pool)
        xh, v4, u_pool, h = k1_out[:4]
        x_rows = k1_out[4] if first else tokens[0]

        wg = jnp.concatenate([gate_w(lru_wa[l]), gate_w(lru_wx[l])], axis=-1).astype(BF16)
        h_lru = _lru(xh, wg, lru_ba[l], lru_bx[l], lru_lam[l], n_lat_rows, n_ctx_rows, lru_steps)

        y_lat = _position_dft(v4, n2_lat, 0)
        y_ctx = None if last else _position_dft(v4, n2_ctx, n2_lat // n2_ctx)

        out = _k3(x_rows, h, mod_k, h_lru, y_lat, y_ctx, u_pool, wl,
                  proj_a[l].astype(BF16), proj_b[l].astype(BF16), proj_c[l].astype(BF16), w_out[l].astype(BF16),
                  _block_diag(fft_w[l]).astype(BF16), _block_diag(pool_w[l]).astype(BF16),
                  pool_scale[l].reshape(1, w_pool), final_g.reshape(1, d), last, n_lat_tiles, t_lat, t_ctx)
        tokens = (out,)
    return out
```

```python
import functools

import numpy as np
import jax
import jax.numpy as jnp
from jax import lax
from jax.experimental import pallas as pl
from jax.experimental.pallas import tpu as pltpu

F32 = jnp.float32
BF16 = jnp.bfloat16

GRID_WIDTH = 64
LRU_POWER = 8.0
NORM_EPS = 1e-6
POSITION_BASE = 10000.0
POOL_WINDOW_SIZES = (2, 4, 8, 16)
CONV_TAPS = 4
SUBLANES = 8
LANES = 128
DFT_RADIX = 64
DFT_GROUP_ROWS = 2 * SUBLANES
DFT_STAGE1_ROWS = 2048
POOL_HALO_ROWS = 64
TOKEN_TILE_ROWS = DFT_RADIX * SUBLANES
LRU_CHUNK_STEPS = 128
LRU_GATE_WIDTH = 256
SCAN_UNROLL = 8
VMEM_LIMIT_BYTES = 56 * 1024 * 1024


def _half_tanh_sigmoid(x_half):
    return 0.5 * (1.0 + jnp.tanh(x_half))


def _sigmoid(x):
    return _half_tanh_sigmoid(0.5 * x)


def _silu(x):
    return x * _sigmoid(x)


def _params(n_axes=1):
    return pltpu.CompilerParams(dimension_semantics=("arbitrary",) * n_axes, vmem_limit_bytes=VMEM_LIMIT_BYTES)


def _resident(shape, index_map):
    return pl.BlockSpec(shape, index_map, pipeline_mode=pl.Buffered(1))


def _mod_body(cc_ref, w_ref, b_ref, o_ref):
    s = _silu(cc_ref[...]).astype(BF16)
    o_ref[0] = jnp.dot(s, w_ref[0].astype(BF16), preferred_element_type=F32) + b_ref[0]


def _modulation(cc, ada_w, ada_b):
    depth, d, n = ada_w.shape
    tn = 1024
    return pl.pallas_call(
        _mod_body,
        grid=(depth, n // tn),
        in_specs=[
            pl.BlockSpec((2 * SUBLANES, d), lambda l, i: (0, 0)),
            pl.BlockSpec((1, d, tn), lambda l, i: (l, 0, i)),
            pl.BlockSpec((1, 1, tn), lambda l, i: (l, 0, i)),
        ],
        out_specs=pl.BlockSpec((1, 2 * SUBLANES, tn), lambda l, i: (l, 0, i)),
        out_shape=jax.ShapeDtypeStruct((depth, 2 * SUBLANES, n), F32),
        compiler_params=_params(2),
        name="adaln_mod",
    )(cc, ada_w, ada_b.reshape(depth, 1, n))


def _rows_body(x_ref, o_ref):
    bsz, tt, d = x_ref.shape
    o_ref[...] = jnp.transpose(x_ref[...], (1, 0, 2)).reshape(tt * bsz, d)


def _batch_major_to_rows(x):
    bsz, t, d = x.shape
    tt = TOKEN_TILE_ROWS // bsz
    return pl.pallas_call(
        _rows_body,
        grid=(t // tt,),
        in_specs=[pl.BlockSpec((bsz, tt, d), lambda j: (0, j, 0))],
        out_specs=pl.BlockSpec((tt * bsz, d), lambda j: (j, 0)),
        out_shape=jax.ShapeDtypeStruct((t * bsz, d), x.dtype),
        compiler_params=_params(),
        name="ctx_rows",
    )(x)


def _seq_edges(p, n_lat_tiles, n_tiles):
    is_first = jnp.logical_or(p == 0, p == n_lat_tiles)
    is_last = jnp.logical_or(p == n_lat_tiles - 1, p == n_tiles - 1)
    return is_first, is_last


def _k1_body(first, n_lat_tiles, n_tiles, *refs):
    n_tok = 3 if first else 1
    tok_refs = refs[:n_tok]
    refs = list(refs[n_tok:])
    mod_ref, g_ref, wl_ref, wf_ref, wp_ref, cd_ref, cw_ref, cb_ref, xh_ref, v_ref, upool_ref, h_ref = refs[:12]
    refs = refs[12:]
    x0_ref = refs.pop(0) if first else None
    uprev, tail = refs
    j = pl.program_id(0)
    rows, d = h_ref.shape
    bsz = SUBLANES
    tt = rows // bsz
    tail_rows = tail.shape[0]

    @pl.when(j == 0)
    def _():
        uprev[...] = jnp.zeros_like(uprev)
        tail[...] = jnp.zeros_like(tail)

    if first:
        x_ref, ctx_ref, pos_ref = tok_refs
        x3 = jnp.where(j < n_lat_tiles, jnp.transpose(x_ref[...] + pos_ref[...][None], (1, 0, 2)),
                       ctx_ref[...].reshape(tt, bsz, d))
        x0_ref[...] = x3.reshape(rows, d)
    else:
        x3 = tok_refs[0][...].reshape(tt, bsz, d)
    ms = jnp.mean(x3 * x3, axis=-1, keepdims=True)
    hn = x3 * lax.rsqrt(ms + NORM_EPS) * g_ref[...]
    h = (hn * (1.0 + mod_ref[0, 1]) + mod_ref[0, 0]).reshape(rows, d).astype(BF16)
    h_ref[...] = h
    u = jnp.dot(h, wl_ref[0], preferred_element_type=F32)
    ufft = jnp.dot(h, wf_ref[0], preferred_element_type=F32).astype(BF16)
    v = jnp.dot(ufft, cd_ref[...], preferred_element_type=F32)
    w = v.shape[1] // 2
    v4 = jnp.stack([v[:, :w].reshape(tt, bsz, w), v[:, w:].reshape(tt, bsz, w)], axis=1)
    v_ref[...] = v4.reshape(1, tt, DFT_GROUP_ROWS, w).astype(BF16)
    upool_ref[...] = jnp.dot(h, wp_ref[0], preferred_element_type=F32).astype(BF16)

    p_first, p_last = _seq_edges(j - 1, n_lat_tiles, n_tiles)
    up = uprev[...]
    before = jnp.where(p_first, 0.0, tail[...])
    after = jnp.where(p_last, 0.0, u[0:SUBLANES, :])
    ext = jnp.concatenate([before, up, after], axis=0)
    cwh = 0.5 * cw_ref[...]
    xh = 0.5 * cb_ref[...] + cwh[0:1] * ext[0:rows]
    for k in range(1, CONV_TAPS):
        xh = xh + cwh[k:k + 1] * ext[k * SUBLANES:k * SUBLANES + rows]
    xh_ref[...] = xh.astype(BF16)

    tail[...] = up[rows - tail_rows:rows, :]
    uprev[...] = u


def _k1(tokens, mod_k, g, w_all, layer, cd, conv_w, conv_b, first, n_all, n_lat_tiles, w_lru, w_fft, w_pool):
    d = g.shape[-1]
    bsz = SUBLANES
    r = TOKEN_TILE_ROWS
    tt = r // bsz
    n_tiles = n_all // r
    n_ctx_tiles = n_tiles - n_lat_tiles
    const2 = lambda j: (0, 0)
    off_fft, off_pool = 2 * w_lru, 2 * w_lru + 2 * w_fft
    assert off_fft % w_fft == 0 and off_pool % w_pool == 0
    clamp = lambda j: (jnp.minimum(j, n_tiles - 1), 0)
    if first:
        tok_specs = [
            pl.BlockSpec((bsz, tt, d), lambda j: (0, jnp.minimum(j, n_lat_tiles - 1), 0)),
            pl.BlockSpec((r, d), lambda j: (jnp.clip(j - n_lat_tiles, 0, n_ctx_tiles - 1), 0)),
            pl.BlockSpec((tt, d), lambda j: (jnp.minimum(j, n_lat_tiles - 1), 0)),
        ]
    else:
        tok_specs = [pl.BlockSpec((r, d), clamp)]
    in_specs = tok_specs + [
        pl.BlockSpec((1, 3, bsz, d), lambda j: (jnp.where(j < n_lat_tiles, 0, 1), 0, 0, 0)),
        pl.BlockSpec((1, d), const2),
        _resident((1, d, w_lru), lambda j: (layer, 0, 0)),
        _resident((1, d, w_fft), lambda j: (layer, 0, off_fft // w_fft)),
        _resident((1, d, w_pool), lambda j: (layer, 0, off_pool // w_pool)),
        _resident(cd.shape, const2),
        pl.BlockSpec((CONV_TAPS, w_lru), const2),
        pl.BlockSpec((1, w_lru), const2),
    ]
    out_specs = [
        pl.BlockSpec((r, w_lru), lambda j: (jnp.maximum(j - 1, 0), 0)),
        pl.BlockSpec((1, tt, DFT_GROUP_ROWS, w_fft), lambda j: (jnp.minimum(j, n_tiles - 1), 0, 0, 0)),
        pl.BlockSpec((r, w_pool), clamp),
        pl.BlockSpec((r, d), clamp),
    ]
    out_shape = [
        jax.ShapeDtypeStruct((n_all, w_lru), BF16),
        jax.ShapeDtypeStruct((n_tiles, tt, DFT_GROUP_ROWS, w_fft), BF16),
        jax.ShapeDtypeStruct((n_all, w_pool), BF16),
        jax.ShapeDtypeStruct((n_all, d), BF16),
    ]
    if first:
        out_specs.append(pl.BlockSpec((r, d), clamp))
        out_shape.append(jax.ShapeDtypeStruct((n_all, d), F32))
    return pl.pallas_call(
        functools.partial(_k1_body, first, n_lat_tiles, n_tiles),
        grid=(n_tiles + 1,),
        in_specs=in_specs,
        out_specs=out_specs,
        out_shape=out_shape,
        scratch_shapes=[pltpu.VMEM((r, w_lru), F32), pltpu.VMEM(((CONV_TAPS - 2) * SUBLANES, w_lru), F32)],
        compiler_params=_params(),
        name="norm_uproj",
    )(*tokens, mod_k, g, w_all, w_all, w_all, cd, conv_w, conv_b.reshape(1, w_lru))


def _lru_chunk(d, j, n_lat_chunks, n_ctx_chunks):
    in_ctx = j < n_ctx_chunks
    jj = j - n_ctx_chunks
    ctx_ck = n_lat_chunks + jnp.where(d == 0, j, n_ctx_chunks - 1 - j)
    lat_ck = jnp.where(d == 0, jj, n_lat_chunks - 1 - jj)
    return jnp.where(in_ctx, ctx_ck, lat_ck)


def _lru_body(steps, xh_ref, wg_ref, ba_ref, bx_ref, lam_ref, o_ref, a_sc, b_sc, h_sc, state):
    d = pl.program_id(0)
    j = pl.program_id(1)
    cb = xh_ref.shape[1]

    @pl.when(j == 0)
    def _():
        state[...] = jnp.zeros_like(state)

    lam = lam_ref[0]
    half_rate = (-0.5 * LRU_POWER) * (jnp.maximum(-lam, 0.0) + jnp.log1p(jnp.exp(-jnp.abs(lam))))
    half_ba = 0.5 * ba_ref[0]
    half_bx = 0.5 * bx_ref[0]
    gw = wg_ref.shape[2]
    for g in range(cb // gw):
        sl = slice(g * gw, (g + 1) * gw)
        xb = xh_ref[:, sl]
        gates = jnp.dot(xb, wg_ref[0, g], preferred_element_type=F32)
        log_a = (1.0 + jnp.tanh(gates[:, :gw] + half_ba[:, sl])) * half_rate[:, sl]
        a = jnp.exp(log_a)
        q = jnp.tanh(log_a) * (-1.0 - a * a)
        gain = jnp.where(q > 0.0, q * lax.rsqrt(q), 0.0)
        a_sc[:, sl] = a
        b_sc[:, sl] = gain * ((1.0 + jnp.tanh(gates[:, gw:] + half_bx[:, sl])) * xb.astype(F32))

    n_blocks = steps // SCAN_UNROLL
    block_rows = SCAN_UNROLL * SUBLANES

    def scan(reverse):
        def block(i, h):
            base = pl.multiple_of((n_blocks - 1 - i if reverse else i) * block_rows, block_rows)
            for k in (range(SCAN_UNROLL - 1, -1, -1) if reverse else range(SCAN_UNROLL)):
                r0 = base + k * SUBLANES
                h = a_sc[pl.ds(r0, SUBLANES), :] * h + b_sc[pl.ds(r0, SUBLANES), :]
                h_sc[pl.ds(r0, SUBLANES), :] = h
            return h

        state[...] = lax.fori_loop(0, n_blocks, block, state[...])

    @pl.when(d == 0)
    def _():
        scan(False)

    @pl.when(d == 1)
    def _():
        scan(True)

    o_ref[0] = h_sc[...].astype(BF16)


def _lru(xh, wg, ba, bx, lam, n_lat_rows, n_ctx_rows, steps):
    n_all, w = xh.shape
    rc = steps * SUBLANES
    n_lat_chunks = n_lat_rows // rc
    n_ctx_chunks = n_ctx_rows // rc
    n_chunks = n_lat_chunks + n_ctx_chunks
    gw = wg.shape[2]
    ck = functools.partial(_lru_chunk, n_lat_chunks=n_lat_chunks, n_ctx_chunks=n_ctx_chunks)
    return pl.pallas_call(
        functools.partial(_lru_body, steps),
        grid=(2, n_chunks),
        in_specs=[
            pl.BlockSpec((rc, w), lambda d, j: (ck(d, j), 0)),
            pl.BlockSpec((1, w // gw, gw, 2 * gw), lambda d, j: (d, 0, 0, 0)),
            pl.BlockSpec((1, 1, w), lambda d, j: (d, 0, 0)),
            pl.BlockSpec((1, 1, w), lambda d, j: (d, 0, 0)),
            pl.BlockSpec((1, 1, w), lambda d, j: (d, 0, 0)),
        ],
        out_specs=pl.BlockSpec((1, rc, w), lambda d, j: (d, ck(d, j), 0)),
        out_shape=jax.ShapeDtypeStruct((2, n_all, w), BF16),
        scratch_shapes=[
            pltpu.VMEM((rc, w), F32),
            pltpu.VMEM((rc, w), F32),
            pltpu.VMEM((rc, w), F32),
            pltpu.VMEM((SUBLANES, w), F32),
        ],
        compiler_params=_params(2),
        name="rglru_scan",
    )(xh, wg, ba.reshape(2, 1, w), bx.reshape(2, 1, w), lam.reshape(2, 1, w))


def _dft_constants(n2):
    n1 = DFT_RADIX
    n = n1 * n2
    eye = np.eye(SUBLANES)
    k2 = np.arange(n2)
    ang1 = 2.0 * np.pi * ((k2[:, None] * k2[None, :]) % n2) / n2
    c1, s1 = np.cos(ang1) / np.sqrt(n2), np.sin(ang1) / np.sqrt(n2)
    base1 = np.stack([np.stack([c1, s1], axis=-1), np.stack([-s1, c1], axis=-1)], axis=1)
    m1 = jnp.asarray(np.kron(base1.reshape(2 * n2, 2 * n2), eye), F32)
    k1 = np.arange(n1)
    ang2 = 2.0 * np.pi * ((k1[:, None] * k1[None, :]) % n1) / n1
    base2 = np.stack([np.cos(ang2), np.sin(ang2)], axis=-1) / np.sqrt(n1)
    m2 = jnp.asarray(np.kron(base2.reshape(n1, 2 * n1), eye), F32)
    angt = 2.0 * np.pi * ((k1[:, None] * k2[None, :]) % n) / n
    shape = (n1, n2, SUBLANES, LANES)

    def table(vals):
        return jnp.broadcast_to(jnp.asarray(vals, F32)[:, :, None, None], shape)

    return m1.astype(BF16), m2.astype(BF16), table(np.cos(angt)), table(np.sin(angt))


def _fft1_body(v_ref, m1_ref, twc_ref, tws_ref, z_ref):
    n2, a_blk, grp, w = v_ref.shape
    reps = w // LANES
    for i in range(a_blk):
        xs = v_ref[:, i].reshape(n2 * grp, w)
        r = jnp.dot(m1_ref[...], xs, preferred_element_type=F32).reshape(n2, 2, SUBLANES, w)
        rr, ri = r[:, 0], r[:, 1]
        cw = jnp.concatenate([twc_ref[i]] * reps, axis=-1)
        sw = jnp.concatenate([tws_ref[i]] * reps, axis=-1)
        z = jnp.stack([rr * cw + ri * sw, ri * cw - rr * sw], axis=1)
        z_ref[:, i] = z.reshape(n2, grp, w).astype(BF16)


def _fft2_body(z_ref, m2_ref, y_ref):
    pair, n1, grp, w = z_ref.shape
    ys = [jnp.dot(m2_ref[...], z_ref[i].reshape(n1 * grp, w), preferred_element_type=F32).reshape(n1, SUBLANES, w)
          for i in range(pair)]
    y_ref[...] = jnp.stack(ys, axis=1).reshape(n1, 1, grp, w).astype(BF16)


def _position_dft(v4, n2, block0):
    m1, m2, twc, tws = _dft_constants(n2)
    n1 = DFT_RADIX
    grp, w = v4.shape[2], v4.shape[3]
    a_blk = max(1, DFT_STAGE1_ROWS // (n2 * grp))
    z = pl.pallas_call(
        _fft1_body,
        grid=(n1 // a_blk,),
        in_specs=[
            pl.BlockSpec((n2, a_blk, grp, w), lambda a: (block0, a, 0, 0)),
            _resident(m1.shape, lambda a: (0, 0)),
            pl.BlockSpec((a_blk, n2, SUBLANES, LANES), lambda a: (a, 0, 0, 0)),
            pl.BlockSpec((a_blk, n2, SUBLANES, LANES), lambda a: (a, 0, 0, 0)),
        ],
        out_specs=pl.BlockSpec((n2, a_blk, grp, w), lambda a: (0, a, 0, 0)),
        out_shape=jax.ShapeDtypeStruct((n2, n1, grp, w), BF16),
        compiler_params=_params(),
        name="dft_stage1",
    )(v4, m1, twc, tws)
    pair = grp // SUBLANES
    return pl.pallas_call(
        _fft2_body,
        grid=(n2 // pair,),
        in_specs=[
            pl.BlockSpec((pair, n1, grp, w), lambda c: (c, 0, 0, 0)),
            _resident(m2.shape, lambda c: (0, 0)),
        ],
        out_specs=pl.BlockSpec((n1, 1, grp, w), lambda c: (0, c, 0, 0)),
        out_shape=jax.ShapeDtypeStruct((n1, n2 // pair, grp, w), BF16),
        compiler_params=_params(),
        name="dft_stage2",
    )(z, m2)


def _pool_minus_token(ue, rows, t0, t_seq):
    halo = POOL_HALO_ROWS
    tvec = t0 + lax.broadcasted_iota(jnp.int32, (rows, LANES), 0) // SUBLANES
    parts = []
    for g, win in enumerate(POOL_WINDOW_SIZES):
        col = ue[:, g * LANES:(g + 1) * LANES]
        n = col.shape[0]
        acc = col[0:n - SUBLANES] + col[SUBLANES:n]
        e0 = SUBLANES
        span = 1
        while 2 * span < win:
            sh = span * SUBLANES
            n = acc.shape[0]
            acc = acc[0:n - 2 * sh] + acc[2 * sh:n]
            e0 += sh
            span *= 2
        wsum = acc[halo - e0:halo - e0 + rows]
        half = win // 2
        cnt = jnp.minimum(tvec + half, t_seq) - jnp.maximum(tvec - half, 0)
        parts.append(wsum / cnt.astype(F32) - col[halo:halo + rows])
    return jnp.concatenate(parts, axis=1)


def _k3_body(cfg, *refs):
    last, n_lat_tiles, n_tiles, t_lat, t_ctx = cfg
    refs = list(refs)
    x_ref, h_ref, mod_ref, hf_ref, hb_ref, yl_ref = refs[:6]
    refs = refs[6:]
    yc_ref = None if last else refs.pop(0)
    (up_ref, upp_ref, upn_ref, wzl_ref, wzf_ref, wzp_ref, wg0_ref, wg1_ref, wg2_ref, pa_ref, pb_ref, pc_ref,
     wo_ref, fw_ref, pw_ref, ps_ref, fg_ref, o_ref) = refs
    j = pl.program_id(0)
    is_lat = j < n_lat_tiles
    rows, d = x_ref.shape
    bsz = SUBLANES
    tt = rows // bsz
    h = h_ref[...]

    z_lru = jnp.dot(h, wzl_ref[0], preferred_element_type=F32)
    y_lru = hf_ref[0].astype(F32) + hb_ref[0].astype(F32)
    ya = jnp.dot((y_lru * _silu(z_lru)).astype(BF16), pa_ref[...], preferred_element_type=F32)

    z_fft = jnp.dot(h, wzf_ref[0], preferred_element_type=F32)
    w = yl_ref.shape[-1]
    f = yl_ref[...].reshape(rows, w)
    if not last:
        f = jnp.where(is_lat, f, yc_ref[...].reshape(rows, w))
    y_fft = jnp.dot(f, fw_ref[...], preferred_element_type=F32)
    yb = jnp.dot((y_fft * _silu(z_fft)).astype(BF16), pb_ref[...], preferred_element_type=F32)

    z_pool = jnp.dot(h, wzp_ref[0], preferred_element_type=F32)
    seq_first, seq_last = _seq_edges(j, n_lat_tiles, n_tiles)
    ue = jnp.concatenate([
        upp_ref[...].astype(F32) * jnp.where(seq_first, 0.0, 1.0).astype(F32),
        up_ref[...].astype(F32),
        upn_ref[...].astype(F32) * jnp.where(seq_last, 0.0, 1.0).astype(F32)], axis=0)
    t0 = jnp.where(is_lat, j, j - n_lat_tiles) * tt
    t_seq = jnp.where(is_lat, t_lat, t_ctx)
    p = _pool_minus_token(ue, rows, t0, t_seq)
    y_pool = jnp.dot(p.astype(BF16), pw_ref[...], preferred_element_type=F32) * ps_ref[...]
    yc = jnp.dot((y_pool * _silu(z_pool)).astype(BF16), pc_ref[...], preferred_element_type=F32)

    m = _sigmoid(jnp.dot(h, wg0_ref[0], preferred_element_type=F32)) * ya
    m = m + _sigmoid(jnp.dot(h, wg1_ref[0], preferred_element_type=F32)) * yb
    m = m + _sigmoid(jnp.dot(h, wg2_ref[0], preferred_element_type=F32)) * yc
    out = jnp.dot(m.astype(BF16), wo_ref[...], preferred_element_type=F32)
    xn = x_ref[...].reshape(tt, bsz, d) + mod_ref[0, 2] * out.reshape(tt, bsz, d)
    if last:
        ms = jnp.mean(xn * xn, axis=-1, keepdims=True)
        xn = xn * lax.rsqrt(ms + NORM_EPS) * fg_ref[...]
        o_ref[...] = jnp.transpose(xn, (1, 0, 2))
    else:
        o_ref[...] = xn.reshape(rows, d)


def _k3(x_rows, h, mod_k, h_lru, y_lat, y_ctx, u_pool, w_all, layer, pa, pb, pc, wo, fw, pw, ps, fg, last,
        n_lat_tiles, t_lat, t_ctx):
    n_all, d = h.shape
    bsz = SUBLANES
    r = TOKEN_TILE_ROWS
    tt = r // bsz
    n_tiles = n_all // r
    grid_tiles = n_lat_tiles if last else n_tiles
    w_lru = h_lru.shape[2]
    w_fft = y_lat.shape[-1]
    w_pool = u_pool.shape[1]
    hr = r // POOL_HALO_ROWS
    n_halo = n_all // POOL_HALO_ROWS
    cfg = (last, n_lat_tiles, n_tiles, t_lat, t_ctx)
    const2 = lambda j: (0, 0)
    off_zl, off_zf, off_zp = w_lru, 2 * w_lru + w_fft, 2 * w_lru + 2 * w_fft + w_pool
    off_g = 2 * w_lru + 2 * w_fft + 2 * w_pool
    assert off_zl % w_lru == 0 and off_zf % w_fft == 0 and off_zp % w_pool == 0 and off_g % d == 0

    def dft_spec(y4, first_tile):
        n2 = y4.shape[1] * y4.shape[2] // bsz
        assert tt % n2 == 0
        blk = (tt // n2,) + y4.shape[1:]
        n_blk = y4.shape[0] // blk[0]
        return pl.BlockSpec(blk, lambda j: (jnp.clip(j - first_tile, 0, n_blk - 1), 0, 0, 0))

    row_spec = lambda width: pl.BlockSpec((r, width), lambda j: (j, 0))
    pair_spec = lambda k: pl.BlockSpec((1, r, w_lru), lambda j: (k, j, 0))
    in_specs = [
        row_spec(d), row_spec(d),
        pl.BlockSpec((1, 3, bsz, d), lambda j: (jnp.where(j < n_lat_tiles, 0, 1), 0, 0, 0)),
        pair_spec(0), pair_spec(1),
        dft_spec(y_lat, 0),
    ]
    args = [x_rows, h, mod_k, h_lru, h_lru, y_lat]
    if not last:
        in_specs.append(dft_spec(y_ctx, n_lat_tiles))
        args.append(y_ctx)
    in_specs += [
        row_spec(w_pool),
        pl.BlockSpec((POOL_HALO_ROWS, w_pool), lambda j: (jnp.maximum(j * hr - 1, 0), 0)),
        pl.BlockSpec((POOL_HALO_ROWS, w_pool), lambda j: (jnp.minimum((j + 1) * hr, n_halo - 1), 0)),
        _resident((1, d, w_lru), lambda j: (layer, 0, off_zl // w_lru)),
        _resident((1, d, w_fft), lambda j: (layer, 0, off_zf // w_fft)),
        _resident((1, d, w_pool), lambda j: (layer, 0, off_zp // w_pool)),
        _resident((1, d, d), lambda j: (layer, 0, off_g // d)),
        _resident((1, d, d), lambda j: (layer, 0, off_g // d + 1)),
        _resident((1, d, d), lambda j: (layer, 0, off_g // d + 2)),
    ]
    args += [u_pool, u_pool, u_pool] + [w_all] * 6
    for wgt in (pa, pb, pc, wo, fw, pw, ps, fg):
        in_specs.append(_resident(wgt.shape, const2))
        args.append(wgt)
    if last:
        out_spec = pl.BlockSpec((bsz, tt, d), lambda j: (0, j, 0))
        out_shape = jax.ShapeDtypeStruct((bsz, t_lat, d), F32)
    else:
        out_spec = row_spec(d)
        out_shape = jax.ShapeDtypeStruct((n_all, d), F32)
    return pl.pallas_call(
        functools.partial(_k3_body, cfg),
        grid=(grid_tiles,),
        in_specs=in_specs,
        out_specs=out_spec,
        out_shape=out_shape,
        compiler_params=_params(),
        name="merge_residual",
    )(*args)


def _position_code(n_tokens, d):
    rows = n_tokens // GRID_WIDTH
    quarter = d // 4
    omega = 1.0 / (POSITION_BASE ** (jnp.arange(quarter, dtype=F32) / quarter))

    def emb(n):
        ang = jnp.arange(n).astype(F32)[:, None] * omega[None, :]
        return jnp.concatenate([jnp.sin(ang), jnp.cos(ang)], axis=-1)

    row_code = jnp.repeat(emb(rows), GRID_WIDTH, axis=0)
    col_code = jnp.tile(emb(GRID_WIDTH), (rows, 1))
    return jnp.concatenate([row_code, col_code], axis=-1).astype(F32)


def _block_diag(w):
    g, n, _ = w.shape
    eye = jnp.eye(g, dtype=w.dtype)
    return jnp.einsum("gij,gk->gikj", w, eye).reshape(g * n, g * n)


def _channel_dft_matrix(groups, n):
    k = np.arange(n)
    ang = 2.0 * np.pi * ((k[:, None] * k[None, :]) % n) / n
    eye = jnp.eye(groups, dtype=F32)
    cd = jnp.concatenate([jnp.kron(eye, jnp.asarray(np.cos(ang) / np.sqrt(n), F32)),
                          jnp.kron(eye, jnp.asarray(-np.sin(ang) / np.sqrt(n), F32))], axis=1)
    return cd.astype(BF16)


def kernel(x, c, ctx, c_ctx, norm_g, ada_w, ada_b, w_in, conv_w, conv_b, lru_wa, lru_ba, lru_wx, lru_bx, lru_lam,
           fft_w, pool_w, pool_scale, proj_a, proj_b, proj_c, w_out, final_g):
    bsz, t_lat, d = x.shape
    t_ctx = ctx.shape[1]
    depth = w_in.shape[0]
    w_lru = conv_w.shape[-1]
    fft_groups, fft_dim = fft_w.shape[1], fft_w.shape[2]
    w_fft = fft_groups * fft_dim
    w_pool = pool_w.shape[1] * pool_w.shape[2]
    heads, head_dim = lru_wa.shape[2], lru_wa.shape[3]
    assert bsz == SUBLANES and fft_dim == LANES and pool_w.shape[2] == LANES and conv_w.shape[1] == CONV_TAPS
    assert t_lat % DFT_RADIX == 0 and t_ctx % DFT_RADIX == 0 and (t_lat // DFT_RADIX) % (t_ctx // DFT_RADIX) == 0
    n_lat_rows, n_ctx_rows = t_lat * bsz, t_ctx * bsz
    n_all = n_lat_rows + n_ctx_rows
    n_lat_tiles = n_lat_rows // TOKEN_TILE_ROWS
    lru_steps = min(LRU_CHUNK_STEPS, t_ctx)
    heads_per_group = LRU_GATE_WIDTH // head_dim

    pos = _position_code(t_lat, d)
    w_all = w_in.astype(BF16)
    cc =jnp.concatenate([c, jnp.broadcast_to(c_ctx[None, :], (bsz, d))], axis=0)
    mod = _modulation(cc, ada_w, ada_b)
    cd = _channel_dft_matrix(fft_groups, fft_dim)
    n2_lat, n2_ctx = t_lat // DFT_RADIX, t_ctx // DFT_RADIX

    def gate_w(w):
        wd = w.reshape(2 * heads // heads_per_group, heads_per_group, head_dim, head_dim)
        return jax.vmap(_block_diag)(wd).reshape(2, heads // heads_per_group, LRU_GATE_WIDTH, LRU_GATE_WIDTH)

    tokens = (x, _batch_major_to_rows(ctx), pos)
    out = None
    for l in range(depth):
        first, last = l == 0, l == depth - 1
        mod_k = mod[l].reshape(2, bsz, 3, d).transpose(0, 2, 1, 3)
        g = norm_g[l].reshape(1, d)
        k1_out = _k1(tokens, mod_k, g, w_all, l, cd, conv_w[l], conv_b[l], first, n_all, n_lat_tiles,
                     w_lru, w_fft, w_pool)
        xh, v4, u_pool, h = k1_out[:4]
        x_rows = k1_out[4] if first else tokens[0]

        wg = jnp.concatenate([gate_w(lru_wa[l]), gate_w(lru_wx[l])], axis=-1).astype(BF16)
        h_lru = _lru(xh, wg, lru_ba[l], lru_bx[l], lru_lam[l], n_lat_rows, n_ctx_rows, lru_steps)

        y_lat = _position_dft(v4, n2_lat, 0)
        y_ctx = None if last else _position_dft(v4, n2_ctx, n2_lat // n2_ctx)

        out = _k3(x_rows, h, mod_k, h_lru, y_lat, y_ctx, u_pool, w_all, l,
                  proj_a[l].astype(BF16), proj_b[l].astype(BF16), proj_c[l].astype(BF16), w_out[l].astype(BF16),
                  _block_diag(fft_w[l]).astype(BF16), _block_diag(pool_w[l]).astype(BF16),
                  pool_scale[l].reshape(1, w_pool), final_g.reshape(1, d), last, n_lat_tiles, t_lat, t_ctx)
        tokens = (out,)
    return out
```

```python
import functools

import numpy as np
import jax
import jax.numpy as jnp
from jax import lax
from jax.experimental import pallas as pl
from jax.experimental.pallas import tpu as pltpu

F32 = jnp.float32
BF16 = jnp.bfloat16

GRID_WIDTH = 64
LRU_POWER = 8.0
NORM_EPS = 1e-6
POSITION_BASE = 10000.0
POOL_WINDOW_SIZES = (2, 4, 8, 16)
CONV_TAPS = 4
SUBLANES = 8
LANES = 128
DFT_RADIX = 64
DFT_GROUP_ROWS = 2 * SUBLANES
DFT_STAGE1_ROWS = 2048
DFT_LANE_BLOCK = 256
POOL_HALO_ROWS = 64
TOKEN_TILE_ROWS = DFT_RADIX * SUBLANES
LRU_CHUNK_STEPS = 128
LRU_GATE_WIDTH = 256
SCAN_UNROLL = 8
VMEM_LIMIT_BYTES = 56 * 1024 * 1024


def _half_tanh_sigmoid(x_half):
    return 0.5 * (1.0 + jnp.tanh(x_half))


def _sigmoid(x):
    return _half_tanh_sigmoid(0.5 * x)


def _silu(x):
    return x * _sigmoid(x)


def _params(n_axes=1):
    return pltpu.CompilerParams(dimension_semantics=("arbitrary",) * n_axes, vmem_limit_bytes=VMEM_LIMIT_BYTES)


def _resident(shape, index_map):
    return pl.BlockSpec(shape, index_map, pipeline_mode=pl.Buffered(1))


def _mod_body(cc_ref, w_ref, b_ref, o_ref):
    s = _silu(cc_ref[...]).astype(BF16)
    o_ref[0] = jnp.dot(s, w_ref[0].astype(BF16), preferred_element_type=F32) + b_ref[0]


def _modulation(cc, ada_w, ada_b):
    depth, d, n = ada_w.shape
    tn = 1024
    return pl.pallas_call(
        _mod_body,
        grid=(depth, n // tn),
        in_specs=[
            pl.BlockSpec((2 * SUBLANES, d), lambda l, i: (0, 0)),
            pl.BlockSpec((1, d, tn), lambda l, i: (l, 0, i)),
            pl.BlockSpec((1, 1, tn), lambda l, i: (l, 0, i)),
        ],
        out_specs=pl.BlockSpec((1, 2 * SUBLANES, tn), lambda l, i: (l, 0, i)),
        out_shape=jax.ShapeDtypeStruct((depth, 2 * SUBLANES, n), F32),
        compiler_params=_params(2),
        name="adaln_mod",
    )(cc, ada_w, ada_b.reshape(depth, 1, n))


def _rows_body(x_ref, o_ref):
    bsz, tt, d = x_ref.shape
    o_ref[...] = jnp.transpose(x_ref[...], (1, 0, 2)).reshape(tt * bsz, d)


def _batch_major_to_rows(x):
    bsz, t, d = x.shape
    tt = TOKEN_TILE_ROWS // bsz
    return pl.pallas_call(
        _rows_body,
        grid=(t // tt,),
        in_specs=[pl.BlockSpec((bsz, tt, d), lambda j: (0, j, 0))],
        out_specs=pl.BlockSpec((tt * bsz, d), lambda j: (j, 0)),
        out_shape=jax.ShapeDtypeStruct((t * bsz, d), x.dtype),
        compiler_params=_params(),
        name="ctx_rows",
    )(x)


def _seq_edges(p, n_lat_tiles, n_tiles):
    is_first = jnp.logical_or(p == 0, p == n_lat_tiles)
    is_last = jnp.logical_or(p == n_lat_tiles - 1, p == n_tiles - 1)
    return is_first, is_last


def _k1_body(first, n_lat_tiles, n_tiles, *refs):
    n_tok = 3 if first else 1
    tok_refs = refs[:n_tok]
    refs = list(refs[n_tok:])
    mod_ref, g_ref, wl_ref, wf_ref, wp_ref, cd_ref, cw_ref, cb_ref, xh_ref, v_ref, upool_ref, h_ref = refs[:12]
    refs = refs[12:]
    x0_ref = refs.pop(0) if first else None
    uprev, tail = refs
    j = pl.program_id(0)
    rows, d = h_ref.shape
    bsz = SUBLANES
    tt = rows // bsz
    tail_rows = tail.shape[0]

    @pl.when(j == 0)
    def _():
        uprev[...] = jnp.zeros_like(uprev)
        tail[...] = jnp.zeros_like(tail)

    if first:
        x_ref, ctx_ref, pos_ref = tok_refs
        x3 = jnp.where(j < n_lat_tiles, jnp.transpose(x_ref[...] + pos_ref[...][None], (1, 0, 2)),
                       ctx_ref[...].reshape(tt, bsz, d))
        x0_ref[...] = x3.reshape(rows, d)
    else:
        x3 = tok_refs[0][...].reshape(tt, bsz, d)
    ms = jnp.mean(x3 * x3, axis=-1, keepdims=True)
    hn = x3 * lax.rsqrt(ms + NORM_EPS) * g_ref[...]
    h = (hn * (1.0 + mod_ref[0, 1]) + mod_ref[0, 0]).reshape(rows, d).astype(BF16)
    h_ref[...] = h
    u = jnp.dot(h, wl_ref[0], preferred_element_type=F32)
    ufft = jnp.dot(h, wf_ref[0], preferred_element_type=F32).astype(BF16)
    v = jnp.dot(ufft, cd_ref[...], preferred_element_type=F32)
    w = v.shape[1] // 2
    v4 = jnp.stack([v[:, :w].reshape(tt, bsz, w), v[:, w:].reshape(tt, bsz, w)], axis=1)
    v_ref[...] = v4.reshape(1, tt, DFT_GROUP_ROWS, w).astype(BF16)
    upool_ref[...] = jnp.dot(h, wp_ref[0], preferred_element_type=F32).astype(BF16)

    p_first, p_last = _seq_edges(j - 1, n_lat_tiles, n_tiles)
    up = uprev[...]
    before = jnp.where(p_first, 0.0, tail[...])
    after = jnp.where(p_last, 0.0, u[0:SUBLANES, :])
    ext = jnp.concatenate([before, up, after], axis=0)
    cwh = 0.5 * cw_ref[...]
    xh = 0.5 * cb_ref[...] + cwh[0:1] * ext[0:rows]
    for k in range(1, CONV_TAPS):
        xh = xh + cwh[k:k + 1] * ext[k * SUBLANES:k * SUBLANES + rows]
    xh_ref[...] = xh.astype(BF16)

    tail[...] = up[rows - tail_rows:rows, :]
    uprev[...] = u


def _k1(tokens, mod_k, g, w_all, layer, cd, conv_w, conv_b, first, n_all, n_lat_tiles, w_lru, w_fft, w_pool):
    d = g.shape[-1]
    bsz = SUBLANES
    r = TOKEN_TILE_ROWS
    tt = r // bsz
    n_tiles = n_all // r
    n_ctx_tiles = n_tiles - n_lat_tiles
    const2 = lambda j: (0, 0)
    off_fft, off_pool = 2 * w_lru, 2 * w_lru + 2 * w_fft
    assert off_fft % w_fft == 0 and off_pool % w_pool == 0
    clamp = lambda j: (jnp.minimum(j, n_tiles - 1), 0)
    if first:
        tok_specs = [
            pl.BlockSpec((bsz, tt, d), lambda j: (0, jnp.minimum(j, n_lat_tiles - 1), 0)),
            pl.BlockSpec((r, d), lambda j: (jnp.clip(j - n_lat_tiles, 0, n_ctx_tiles - 1), 0)),
            pl.BlockSpec((tt, d), lambda j: (jnp.minimum(j, n_lat_tiles - 1), 0)),
        ]
    else:
        tok_specs = [pl.BlockSpec((r, d), clamp)]
    in_specs = tok_specs + [
        pl.BlockSpec((1, 3, bsz, d), lambda j: (jnp.where(j < n_lat_tiles, 0, 1), 0, 0, 0)),
        pl.BlockSpec((1, d), const2),
        _resident((1, d, w_lru), lambda j: (layer, 0, 0)),
        _resident((1, d, w_fft), lambda j: (layer, 0, off_fft // w_fft)),
        _resident((1, d, w_pool), lambda j: (layer, 0, off_pool // w_pool)),
        _resident(cd.shape, const2),
        pl.BlockSpec((CONV_TAPS, w_lru), const2),
        pl.BlockSpec((1, w_lru), const2),
    ]
    out_specs = [
        pl.BlockSpec((r, w_lru), lambda j: (jnp.maximum(j - 1, 0), 0)),
        pl.BlockSpec((1, tt, DFT_GROUP_ROWS, w_fft), lambda j: (jnp.minimum(j, n_tiles - 1), 0, 0, 0)),
        pl.BlockSpec((r, w_pool), clamp),
        pl.BlockSpec((r, d), clamp),
    ]
    out_shape = [
        jax.ShapeDtypeStruct((n_all, w_lru), BF16),
        jax.ShapeDtypeStruct((n_tiles, tt, DFT_GROUP_ROWS, w_fft), BF16),
        jax.ShapeDtypeStruct((n_all, w_pool), BF16),
        jax.ShapeDtypeStruct((n_all, d), BF16),
    ]
    if first:
        out_specs.append(pl.BlockSpec((r, d), clamp))
        out_shape.append(jax.ShapeDtypeStruct((n_all, d), F32))
    return pl.pallas_call(
        functools.partial(_k1_body, first, n_lat_tiles, n_tiles),
        grid=(n_tiles + 1,),
        in_specs=in_specs,
        out_specs=out_specs,
        out_shape=out_shape,
        scratch_shapes=[pltpu.VMEM((r, w_lru), F32), pltpu.VMEM(((CONV_TAPS - 2) * SUBLANES, w_lru), F32)],
        compiler_params=_params(),
        name="norm_uproj",
    )(*tokens, mod_k, g, w_all, w_all, w_all, cd, conv_w, conv_b.reshape(1, w_lru))


def _lru_chunk(d, j, n_lat_chunks, n_ctx_chunks):
    in_ctx = j < n_ctx_chunks
    jj = j - n_ctx_chunks
    ctx_ck = n_lat_chunks + jnp.where(d == 0, j, n_ctx_chunks - 1 - j)
    lat_ck = jnp.where(d == 0, jj, n_lat_chunks - 1 - jj)
    return jnp.where(in_ctx, ctx_ck, lat_ck)


def _lru_body(steps, xh_ref, wg_ref, ba_ref, bx_ref, lam_ref, o_ref, a_sc, b_sc, h_sc, state):
    d = pl.program_id(0)
    j = pl.program_id(1)
    cb = xh_ref.shape[1]

    @pl.when(j == 0)
    def _():
        state[...] = jnp.zeros_like(state)

    lam = lam_ref[0]
    half_rate = (-0.5 * LRU_POWER) * (jnp.maximum(-lam, 0.0) + jnp.log1p(jnp.exp(-jnp.abs(lam))))
    half_ba = 0.5 * ba_ref[0]
    half_bx = 0.5 * bx_ref[0]
    gw = wg_ref.shape[2]
    for g in range(cb // gw):
        sl = slice(g * gw, (g + 1) * gw)
        xb = xh_ref[:, sl]
        gates = jnp.dot(xb, wg_ref[0, g], preferred_element_type=F32)
        log_a = (1.0 + jnp.tanh(gates[:, :gw] + half_ba[:, sl])) * half_rate[:, sl]
        a = jnp.exp(log_a)
        q = jnp.tanh(log_a) * (-1.0 - a * a)
        gain = jnp.where(q > 0.0, q * lax.rsqrt(q), 0.0)
        a_sc[:, sl] = a
        b_sc[:, sl] = gain * ((1.0 + jnp.tanh(gates[:, gw:] + half_bx[:, sl])) * xb.astype(F32))

    n_blocks = steps // SCAN_UNROLL
    block_rows = SCAN_UNROLL * SUBLANES

    def scan(reverse):
        def block(i, h):
            base = pl.multiple_of((n_blocks - 1 - i if reverse else i) * block_rows, block_rows)
            for k in (range(SCAN_UNROLL - 1, -1, -1) if reverse else range(SCAN_UNROLL)):
                r0 = base + k * SUBLANES
                h = a_sc[pl.ds(r0, SUBLANES), :] * h + b_sc[pl.ds(r0, SUBLANES), :]
                h_sc[pl.ds(r0, SUBLANES), :] = h
            return h

        state[...] = lax.fori_loop(0, n_blocks, block, state[...])

    @pl.when(d == 0)
    def _():
        scan(False)

    @pl.when(d == 1)
    def _():
        scan(True)

    o_ref[0] = h_sc[...].astype(BF16)


def _lru(xh, wg, ba, bx, lam, n_lat_rows, n_ctx_rows, steps):
    n_all, w = xh.shape
    rc = steps * SUBLANES
    n_lat_chunks = n_lat_rows // rc
    n_ctx_chunks = n_ctx_rows // rc
    n_chunks = n_lat_chunks + n_ctx_chunks
    gw = wg.shape[2]
    ck = functools.partial(_lru_chunk, n_lat_chunks=n_lat_chunks, n_ctx_chunks=n_ctx_chunks)
    return pl.pallas_call(
        functools.partial(_lru_body, steps),
        grid=(2, n_chunks),
        in_specs=[
            pl.BlockSpec((rc, w), lambda d, j: (ck(d, j), 0)),
            pl.BlockSpec((1, w // gw, gw, 2 * gw), lambda d, j: (d, 0, 0, 0)),
            pl.BlockSpec((1, 1, w), lambda d, j: (d, 0, 0)),
            pl.BlockSpec((1, 1, w), lambda d, j: (d, 0, 0)),
            pl.BlockSpec((1, 1, w), lambda d, j: (d, 0, 0)),
        ],
        out_specs=pl.BlockSpec((1, rc, w), lambda d, j: (d, ck(d, j), 0)),
        out_shape=jax.ShapeDtypeStruct((2, n_all, w), BF16),
        scratch_shapes=[
            pltpu.VMEM((rc, w), F32),
            pltpu.VMEM((rc, w), F32),
            pltpu.VMEM((rc, w), F32),
            pltpu.VMEM((SUBLANES, w), F32),
        ],
        compiler_params=_params(2),
        name="rglru_scan",
    )(xh, wg, ba.reshape(2, 1, w), bx.reshape(2, 1, w), lam.reshape(2, 1, w))


def _dft_constants(n2):
    n1 = DFT_RADIX
    n = n1 * n2
    eye = np.eye(SUBLANES)
    k2 = np.arange(n2)
    ang1 = 2.0 * np.pi * ((k2[:, None] * k2[None, :]) % n2) / n2
    c1, s1 = np.cos(ang1) / np.sqrt(n2), np.sin(ang1) / np.sqrt(n2)
    base1 = np.stack([np.stack([c1, s1], axis=-1), np.stack([-s1, c1], axis=-1)], axis=1)
    m1 = jnp.asarray(np.kron(base1.reshape(2 * n2, 2 * n2), eye), F32)
    k1 = np.arange(n1)
    ang2 = 2.0 * np.pi * ((k1[:, None] * k1[None, :]) % n1) / n1
    base2 = np.stack([np.cos(ang2), np.sin(ang2)], axis=-1) / np.sqrt(n1)
    m2 = jnp.asarray(np.kron(base2.reshape(n1, 2 * n1), eye), F32)
    angt = 2.0 * np.pi * ((k1[:, None] * k2[None, :]) % n) / n
    shape = (n1, n2, SUBLANES, LANES)

    def table(vals):
        return jnp.broadcast_to(jnp.asarray(vals, F32)[:, :, None, None], shape)

    return m1.astype(BF16), m2.astype(BF16), table(np.cos(angt)), table(np.sin(angt))


def _dft_body(steps1, v_ref, m1_ref, twc_ref, tws_ref, m2_ref, y_ref, zs):
    s = pl.program_id(1)
    n2, a_blk, grp, w = v_ref.shape
    n1 = zs.shape[1]
    pair = grp // SUBLANES

    @pl.when(s < steps1)
    def _():
        reps = w // LANES
        for i in range(a_blk):
            xs = v_ref[:, i].reshape(n2 * grp, w)
            r = jnp.dot(m1_ref[...], xs, preferred_element_type=F32).reshape(n2, 2, SUBLANES, w)
            rr, ri = r[:, 0], r[:, 1]
            cw = jnp.concatenate([twc_ref[i]] * reps, axis=-1)
            sw = jnp.concatenate([tws_ref[i]] * reps, axis=-1)
            z = jnp.stack([rr * cw + ri * sw, ri * cw - rr * sw], axis=1)
            zs[:, s * a_blk + i] = z.reshape(n2, grp, w).astype(BF16)

    @pl.when(s >= steps1)
    def _():
        c0 = (s - steps1) * pair
        ys = [jnp.dot(m2_ref[...], zs[c0 + i].reshape(n1 * grp, w), preferred_element_type=F32)
              .reshape(n1, SUBLANES, w) for i in range(pair)]
        y_ref[...] = jnp.stack(ys, axis=1).reshape(n1, 1, grp, w).astype(BF16)


def _position_dft(v4, n2, block0):
    m1, m2, twc, tws = _dft_constants(n2)
    n1 = DFT_RADIX
    grp, w = v4.shape[2], v4.shape[3]
    wh = min(w, DFT_LANE_BLOCK)
    a_blk = max(1, DFT_STAGE1_ROWS // (n2 * grp))
    steps1 = n1 // a_blk
    pair = grp // SUBLANES
    stage1_step = lambda s: jnp.minimum(s, steps1 - 1)
    return pl.pallas_call(
        functools.partial(_dft_body, steps1),
        grid=(w // wh, steps1 + n2 // pair),
        in_specs=[
            pl.BlockSpec((n2, a_blk, grp, wh), lambda c, s: (block0, stage1_step(s), 0, c)),
            _resident(m1.shape, lambda c, s: (0, 0)),
            pl.BlockSpec((a_blk, n2, SUBLANES, LANES), lambda c, s: (stage1_step(s), 0, 0, 0)),
            pl.BlockSpec((a_blk, n2, SUBLANES, LANES), lambda c, s: (stage1_step(s), 0, 0, 0)),
            _resident(m2.shape, lambda c, s: (0, 0)),
        ],
        out_specs=pl.BlockSpec((n1, 1, grp, wh), lambda c, s: (0, jnp.maximum(s - steps1, 0), 0, c)),
        out_shape=jax.ShapeDtypeStruct((n1, n2 // pair, grp, w), BF16),
        scratch_shapes=[pltpu.VMEM((n2, n1, grp, wh), BF16)],
        compiler_params=_params(2),
        name="position_dft",
    )(v4, m1, twc, tws, m2)


def _pool_minus_token(ue, rows, t0, t_seq):
    halo = POOL_HALO_ROWS
    tvec = t0 + lax.broadcasted_iota(jnp.int32, (rows, LANES), 0) // SUBLANES
    parts = []
    for g, win in enumerate(POOL_WINDOW_SIZES):
        col = ue[:, g * LANES:(g + 1) * LANES]
        n = col.shape[0]
        acc = col[0:n - SUBLANES] + col[SUBLANES:n]
        e0 = SUBLANES
        span = 1
        while 2 * span < win:
            sh = span * SUBLANES
            n = acc.shape[0]
            acc = acc[0:n - 2 * sh] + acc[2 * sh:n]
            e0 += sh
            span *= 2
        wsum = acc[halo - e0:halo - e0 + rows]
        half = win // 2
        cnt = jnp.minimum(tvec + half, t_seq) - jnp.maximum(tvec - half, 0)
        parts.append(wsum / cnt.astype(F32) - col[halo:halo + rows])
    return jnp.concatenate(parts, axis=1)


def _k3_body(cfg, *refs):
    last, n_lat_tiles, n_tiles, t_lat, t_ctx = cfg
    refs = list(refs)
    x_ref, h_ref, mod_ref, hf_ref, hb_ref, yl_ref = refs[:6]
    refs = refs[6:]
    yc_ref = None if last else refs.pop(0)
    (up_ref, upp_ref, upn_ref, wzl_ref, wzf_ref, wzp_ref, wg0_ref, wg1_ref, wg2_ref, pa_ref, pb_ref, pc_ref,
     wo_ref, pw_ref, ps_ref, fg_ref, o_ref) = refs
    j = pl.program_id(0)
    is_lat = j < n_lat_tiles
    rows, d = x_ref.shape
    bsz = SUBLANES
    tt = rows // bsz
    h = h_ref[...]

    z_lru = jnp.dot(h, wzl_ref[0], preferred_element_type=F32)
    y_lru = hf_ref[0].astype(F32) + hb_ref[0].astype(F32)
    ya = jnp.dot((y_lru * _silu(z_lru)).astype(BF16), pa_ref[...], preferred_element_type=F32)

    z_fft = jnp.dot(h, wzf_ref[0], preferred_element_type=F32)
    w = yl_ref.shape[-1]
    y_fft = yl_ref[...].reshape(rows, w)
    if not last:
        y_fft = jnp.where(is_lat, y_fft, yc_ref[...].reshape(rows, w))
    yb = jnp.dot((y_fft.astype(F32) * _silu(z_fft)).astype(BF16), pb_ref[...], preferred_element_type=F32)

    z_pool = jnp.dot(h, wzp_ref[0], preferred_element_type=F32)
    seq_first, seq_last = _seq_edges(j, n_lat_tiles, n_tiles)
    ue = jnp.concatenate([
        upp_ref[...].astype(F32) * jnp.where(seq_first, 0.0, 1.0).astype(F32),
        up_ref[...].astype(F32),
        upn_ref[...].astype(F32) * jnp.where(seq_last, 0.0, 1.0).astype(F32)], axis=0)
    t0 = jnp.where(is_lat, j, j - n_lat_tiles) * tt
    t_seq = jnp.where(is_lat, t_lat, t_ctx)
    p = _pool_minus_token(ue, rows, t0, t_seq)
    y_pool = jnp.dot(p.astype(BF16), pw_ref[...], preferred_element_type=F32) * ps_ref[...]
    yc = jnp.dot((y_pool * _silu(z_pool)).astype(BF16), pc_ref[...], preferred_element_type=F32)

    m = _sigmoid(jnp.dot(h, wg0_ref[0], preferred_element_type=F32)) * ya
    m = m + _sigmoid(jnp.dot(h, wg1_ref[0], preferred_element_type=F32)) * yb
    m = m + _sigmoid(jnp.dot(h, wg2_ref[0], preferred_element_type=F32)) * yc
    out = jnp.dot(m.astype(BF16), wo_ref[...], preferred_element_type=F32)
    xn = x_ref[...].reshape(tt, bsz, d) + mod_ref[0, 2] * out.reshape(tt, bsz, d)
    if last:
        ms = jnp.mean(xn * xn, axis=-1, keepdims=True)
        xn = xn * lax.rsqrt(ms + NORM_EPS) * fg_ref[...]
        o_ref[...] = jnp.transpose(xn, (1, 0, 2))
    else:
        o_ref[...] = xn.reshape(rows, d)


def _k3(x_rows, h, mod_k, h_lru, y_lat, y_ctx, u_pool, w_all, layer, pa, pb, pc, wo, pw, ps, fg, last,
        n_lat_tiles, t_lat, t_ctx):
    n_all, d = h.shape
    bsz = SUBLANES
    r = TOKEN_TILE_ROWS
    tt = r // bsz
    n_tiles = n_all // r
    grid_tiles = n_lat_tiles if last else n_tiles
    w_lru = h_lru.shape[2]
    w_fft = y_lat.shape[-1]
    w_pool = u_pool.shape[1]
    hr = r // POOL_HALO_ROWS
    n_halo = n_all // POOL_HALO_ROWS
    cfg = (last, n_lat_tiles, n_tiles, t_lat, t_ctx)
    const2 = lambda j: (0, 0)
    off_zl, off_zf, off_zp = w_lru, 2 * w_lru + w_fft, 2 * w_lru + 2 * w_fft + w_pool
    off_g = 2 * w_lru + 2 * w_fft + 2 * w_pool
    assert off_zl % w_lru == 0 and off_zf % w_fft == 0 and off_zp % w_pool == 0 and off_g % d == 0

    def dft_spec(y4, first_tile):
        n2 = y4.shape[1] * y4.shape[2] // bsz
        assert tt % n2 == 0
        blk = (tt // n2,) + y4.shape[1:]
        n_blk = y4.shape[0] // blk[0]
        return pl.BlockSpec(blk, lambda j: (jnp.clip(j - first_tile, 0, n_blk - 1), 0, 0, 0))

    row_spec = lambda width: pl.BlockSpec((r, width), lambda j: (j, 0))
    pair_spec = lambda k: pl.BlockSpec((1, r, w_lru), lambda j: (k, j, 0))
    in_specs = [
        row_spec(d), row_spec(d),
        pl.BlockSpec((1, 3, bsz, d), lambda j: (jnp.where(j < n_lat_tiles, 0, 1), 0, 0, 0)),
        pair_spec(0), pair_spec(1),
        dft_spec(y_lat, 0),
    ]
    args = [x_rows, h, mod_k, h_lru, h_lru, y_lat]
    if not last:
        in_specs.append(dft_spec(y_ctx, n_lat_tiles))
        args.append(y_ctx)
    in_specs += [
        row_spec(w_pool),
        pl.BlockSpec((POOL_HALO_ROWS, w_pool), lambda j: (jnp.maximum(j * hr - 1, 0), 0)),
        pl.BlockSpec((POOL_HALO_ROWS, w_pool), lambda j: (jnp.minimum((j + 1) * hr, n_halo - 1), 0)),
        _resident((1, d, w_lru), lambda j: (layer, 0, off_zl // w_lru)),
        _resident((1, d, w_fft), lambda j: (layer, 0, off_zf // w_fft)),
        _resident((1, d, w_pool), lambda j: (layer, 0, off_zp // w_pool)),
        _resident((1, d, d), lambda j: (layer, 0, off_g // d)),
        _resident((1, d, d), lambda j: (layer, 0, off_g // d + 1)),
        _resident((1, d, d), lambda j: (layer, 0, off_g // d + 2)),
    ]
    args += [u_pool, u_pool, u_pool] + [w_all] * 6
    for wgt in (pa, pb, pc, wo, pw, ps, fg):
        in_specs.append(_resident(wgt.shape, const2))
        args.append(wgt)
    if last:
        out_spec = pl.BlockSpec((bsz, tt, d), lambda j: (0, j, 0))
        out_shape = jax.ShapeDtypeStruct((bsz, t_lat, d), F32)
    else:
        out_spec = row_spec(d)
        out_shape = jax.ShapeDtypeStruct((n_all, d), F32)
    return pl.pallas_call(
        functools.partial(_k3_body, cfg),
        grid=(grid_tiles,),
        in_specs=in_specs,
        out_specs=out_spec,
        out_shape=out_shape,
        compiler_params=_params(),
        name="merge_residual",
    )(*args)


def _position_code(n_tokens, d):
    rows = n_tokens // GRID_WIDTH
    quarter = d // 4
    omega = 1.0 / (POSITION_BASE ** (jnp.arange(quarter, dtype=F32) / quarter))

    def emb(n):
        ang = jnp.arange(n).astype(F32)[:, None] * omega[None, :]
        return jnp.concatenate([jnp.sin(ang), jnp.cos(ang)], axis=-1)

    row_code = jnp.repeat(emb(rows), GRID_WIDTH, axis=0)
    col_code = jnp.tile(emb(GRID_WIDTH), (rows, 1))
    return jnp.concatenate([row_code, col_code], axis=-1).astype(F32)


def _block_diag(w):
    g, n, _ = w.shape
    eye = jnp.eye(g, dtype=w.dtype)
    return jnp.einsum("gij,gk->gikj", w, eye).reshape(g * n, g * n)


def _channel_dft_matrix(groups, n):
    k = np.arange(n)
    ang = 2.0 * np.pi * ((k[:, None] * k[None, :]) % n) / n
    eye = np.eye(groups)
    cd = np.concatenate([np.kron(eye, np.cos(ang) / np.sqrt(n)), np.kron(eye, -np.sin(ang) / np.sqrt(n))], axis=1)
    return jnp.asarray(cd, F32)


def _fold_body(cd_ref, fw_ref, o_ref):
    w = fw_ref.shape[0]
    parts = [jnp.dot(cd_ref[:, i * w:(i + 1) * w], fw_ref[...], preferred_element_type=F32,
                     precision=lax.Precision.HIGHEST) for i in range(2)]
    o_ref[...] = jnp.concatenate(parts, axis=1).astype(BF16)


def _fold_channel_map(cd, fw):
    return pl.pallas_call(
        _fold_body,
        out_shape=jax.ShapeDtypeStruct(cd.shape, BF16),
        compiler_params=pltpu.CompilerParams(vmem_limit_bytes=VMEM_LIMIT_BYTES),
        name="fold_fft_map",
    )(cd, fw)


def kernel(x, c, ctx, c_ctx, norm_g, ada_w, ada_b, w_in, conv_w, conv_b, lru_wa, lru_ba, lru_wx, lru_bx, lru_lam,
           fft_w, pool_w, pool_scale, proj_a, proj_b, proj_c, w_out, final_g):
    bsz, t_lat, d = x.shape
    t_ctx = ctx.shape[1]
    depth = w_in.shape[0]
    w_lru = conv_w.shape[-1]
    fft_groups, fft_dim = fft_w.shape[1], fft_w.shape[2]
    w_fft = fft_groups * fft_dim
    w_pool = pool_w.shape[1] * pool_w.shape[2]
    heads, head_dim = lru_wa.shape[2], lru_wa.shape[3]
    assert bsz == SUBLANES and fft_dim == LANES and pool_w.shape[2] == LANES and conv_w.shape[1] == CONV_TAPS
    assert t_lat % DFT_RADIX == 0 and t_ctx % DFT_RADIX == 0 and (t_lat // DFT_RADIX) % (t_ctx // DFT_RADIX) == 0
    n_lat_rows, n_ctx_rows = t_lat * bsz, t_ctx * bsz
    n_all = n_lat_rows + n_ctx_rows
    n_lat_tiles = n_lat_rows // TOKEN_TILE_ROWS
    lru_steps = min(LRU_CHUNK_STEPS, t_ctx)
    heads_per_group = LRU_GATE_WIDTH // head_dim

    pos = _position_code(t_lat, d)
    w_all = w_in.astype(BF16)
    cc =jnp.concatenate([c, jnp.broadcast_to(c_ctx[None, :], (bsz, d))], axis=0)
    mod = _modulation(cc, ada_w, ada_b)
    cd = _channel_dft_matrix(fft_groups, fft_dim)
    n2_lat, n2_ctx = t_lat // DFT_RADIX, t_ctx // DFT_RADIX

    def gate_w(w):
        wd = w.reshape(2 * heads // heads_per_group, heads_per_group, head_dim, head_dim)
        return jax.vmap(_block_diag)(wd).reshape(2, heads // heads_per_group, LRU_GATE_WIDTH, LRU_GATE_WIDTH)

    tokens = (x, _batch_major_to_rows(ctx), pos)
    out = None
    for l in range(depth):
        first, last = l == 0, l == depth - 1
        mod_k = mod[l].reshape(2, bsz, 3, d).transpose(0, 2, 1, 3)
        g = norm_g[l].reshape(1, d)
        cdw = _fold_channel_map(cd, _block_diag(fft_w[l]))
        k1_out = _k1(tokens, mod_k, g, w_all, l, cdw, conv_w[l], conv_b[l], first, n_all, n_lat_tiles,
                     w_lru, w_fft, w_pool)
        xh, v4, u_pool, h = k1_out[:4]
        x_rows = k1_out[4] if first else tokens[0]

        wg = jnp.concatenate([gate_w(lru_wa[l]), gate_w(lru_wx[l])], axis=-1).astype(BF16)
        h_lru = _lru(xh, wg, lru_ba[l], lru_bx[l], lru_lam[l], n_lat_rows, n_ctx_rows, lru_steps)

        y_lat = _position_dft(v4, n2_lat, 0)
        y_ctx = None if last else _position_dft(v4, n2_ctx, n2_lat // n2_ctx)

        out = _k3(x_rows, h, mod_k, h_lru, y_lat, y_ctx, u_pool, w_all, l,
                  proj_a[l].astype(BF16), proj_b[l].astype(BF16), proj_c[l].astype(BF16), w_out[l].astype(BF16),
                  _block_diag(pool_w[l]).astype(BF16),
                  pool_scale[l].reshape(1, w_pool), final_g.reshape(1, d), last, n_lat_tiles, t_lat, t_ctx)
        tokens = (out,)
    return out
```

```python
import functools

import numpy as np
import jax
import jax.numpy as jnp
from jax import lax
from jax.experimental import pallas as pl
from jax.experimental.pallas import tpu as pltpu

F32 = jnp.float32
BF16 = jnp.bfloat16

GRID_WIDTH = 64
LRU_POWER = 8.0
NORM_EPS = 1e-6
POSITION_BASE = 10000.0
POOL_WINDOW_SIZES = (2, 4, 8, 16)
CONV_TAPS = 4
SUBLANES = 8
LANES = 128
DFT_RADIX = 64
DFT_GROUP_ROWS = 2 * SUBLANES
DFT_STAGE1_ROWS = 4096
DFT_STAGE2_GROUPS = 2
DFT_LANE_BLOCK = 256
POOL_HALO_ROWS = 64
TOKEN_TILE_ROWS = DFT_RADIX * SUBLANES
LRU_CHUNK_STEPS = 128
LRU_GATE_WIDTH = 256
SCAN_UNROLL = 8
VMEM_LIMIT_BYTES = 56 * 1024 * 1024


def _half_tanh_sigmoid(x_half):
    return 0.5 * (1.0 + jnp.tanh(x_half))


def _sigmoid(x):
    return _half_tanh_sigmoid(0.5 * x)


def _silu(x):
    return x * _sigmoid(x)


def _params(n_axes=1):
    return pltpu.CompilerParams(dimension_semantics=("arbitrary",) * n_axes, vmem_limit_bytes=VMEM_LIMIT_BYTES)


def _resident(shape, index_map):
    return pl.BlockSpec(shape, index_map, pipeline_mode=pl.Buffered(1))


def _mod_body(cc_ref, w_ref, b_ref, o_ref):
    s = _silu(cc_ref[...]).astype(BF16)
    o_ref[0] = jnp.dot(s, w_ref[0].astype(BF16), preferred_element_type=F32) + b_ref[0]


def _modulation(cc, ada_w, ada_b):
    depth, d, n = ada_w.shape
    tn = 1024
    return pl.pallas_call(
        _mod_body,
        grid=(depth, n // tn),
        in_specs=[
            pl.BlockSpec((2 * SUBLANES, d), lambda l, i: (0, 0)),
            pl.BlockSpec((1, d, tn), lambda l, i: (l, 0, i)),
            pl.BlockSpec((1, 1, tn), lambda l, i: (l, 0, i)),
        ],
        out_specs=pl.BlockSpec((1, 2 * SUBLANES, tn), lambda l, i: (l, 0, i)),
        out_shape=jax.ShapeDtypeStruct((depth, 2 * SUBLANES, n), F32),
        compiler_params=_params(2),
        name="adaln_mod",
    )(cc, ada_w, ada_b.reshape(depth, 1, n))


def _rows_body(x_ref, o_ref):
    bsz, tt, d = x_ref.shape
    o_ref[...] = jnp.transpose(x_ref[...], (1, 0, 2)).reshape(tt * bsz, d)


def _batch_major_to_rows(x):
    bsz, t, d = x.shape
    tt = TOKEN_TILE_ROWS // bsz
    return pl.pallas_call(
        _rows_body,
        grid=(t // tt,),
        in_specs=[pl.BlockSpec((bsz, tt, d), lambda j: (0, j, 0))],
        out_specs=pl.BlockSpec((tt * bsz, d), lambda j: (j, 0)),
        out_shape=jax.ShapeDtypeStruct((t * bsz, d), x.dtype),
        compiler_params=_params(),
        name="ctx_rows",
    )(x)


def _seq_edges(p, n_lat_tiles, n_tiles):
    is_first = jnp.logical_or(p == 0, p == n_lat_tiles)
    is_last = jnp.logical_or(p == n_lat_tiles - 1, p == n_tiles - 1)
    return is_first, is_last


def _k1_body(first, n_lat_tiles, n_tiles, *refs):
    n_tok = 3 if first else 1
    tok_refs = refs[:n_tok]
    refs = list(refs[n_tok:])
    mod_ref, g_ref, wl_ref, wf_ref, wp_ref, cd_ref, cw_ref, cb_ref, xh_ref, v_ref, upool_ref, h_ref = refs[:12]
    refs = refs[12:]
    x0_ref = refs.pop(0) if first else None
    uprev, tail = refs
    j = pl.program_id(0)
    rows, d = h_ref.shape
    bsz = SUBLANES
    tt = rows // bsz
    tail_rows = tail.shape[0]

    @pl.when(j == 0)
    def _():
        uprev[...] = jnp.zeros_like(uprev)
        tail[...] = jnp.zeros_like(tail)

    if first:
        x_ref, ctx_ref, pos_ref = tok_refs
        x3 = jnp.where(j < n_lat_tiles, jnp.transpose(x_ref[...] + pos_ref[...][None], (1, 0, 2)),
                       ctx_ref[...].reshape(tt, bsz, d))
        x0_ref[...] = x3.reshape(rows, d)
    else:
        x3 = tok_refs[0][...].reshape(tt, bsz, d)
    ms = jnp.mean(x3 * x3, axis=-1, keepdims=True)
    hn = x3 * lax.rsqrt(ms + NORM_EPS) * g_ref[...]
    h = (hn * (1.0 + mod_ref[0, 1]) + mod_ref[0, 0]).reshape(rows, d).astype(BF16)
    h_ref[...] = h
    u = jnp.dot(h, wl_ref[0], preferred_element_type=F32)
    ufft = jnp.dot(h, wf_ref[0], preferred_element_type=F32).astype(BF16)
    v = jnp.dot(ufft, cd_ref[0], preferred_element_type=F32)
    w = v.shape[1] // 2
    v4 = jnp.stack([v[:, :w].reshape(tt, bsz, w), v[:, w:].reshape(tt, bsz, w)], axis=1)
    v_ref[...] = v4.reshape(1, tt, DFT_GROUP_ROWS, w).astype(BF16)
    upool_ref[...] = jnp.dot(h, wp_ref[0], preferred_element_type=F32).astype(BF16)

    p_first, p_last = _seq_edges(j - 1, n_lat_tiles, n_tiles)
    up = uprev[...]
    before = jnp.where(p_first, 0.0, tail[...])
    after = jnp.where(p_last, 0.0, u[0:SUBLANES, :])
    ext = jnp.concatenate([before, up, after], axis=0)
    cwh = 0.5 * cw_ref[...]
    xh = 0.5 * cb_ref[...] + cwh[0:1] * ext[0:rows]
    for k in range(1, CONV_TAPS):
        xh = xh + cwh[k:k + 1] * ext[k * SUBLANES:k * SUBLANES + rows]
    xh_ref[...] = xh.astype(BF16)

    tail[...] = up[rows - tail_rows:rows, :]
    uprev[...] = u


def _k1(tokens, mod_k, g, w_all, layer, cd, conv_w, conv_b, first, n_all, n_lat_tiles, w_lru, w_fft, w_pool):
    d = g.shape[-1]
    bsz = SUBLANES
    r = TOKEN_TILE_ROWS
    tt = r // bsz
    n_tiles = n_all // r
    n_ctx_tiles = n_tiles - n_lat_tiles
    const2 = lambda j: (0, 0)
    off_fft, off_pool = 2 * w_lru, 2 * w_lru + 2 * w_fft
    assert off_fft % w_fft == 0 and off_pool % w_pool == 0
    clamp = lambda j: (jnp.minimum(j, n_tiles - 1), 0)
    if first:
        tok_specs = [
            pl.BlockSpec((bsz, tt, d), lambda j: (0, jnp.minimum(j, n_lat_tiles - 1), 0)),
            pl.BlockSpec((r, d), lambda j: (jnp.clip(j - n_lat_tiles, 0, n_ctx_tiles - 1), 0)),
            pl.BlockSpec((tt, d), lambda j: (jnp.minimum(j, n_lat_tiles - 1), 0)),
        ]
    else:
        tok_specs = [pl.BlockSpec((r, d), clamp)]
    in_specs = tok_specs + [
        pl.BlockSpec((1, 3, bsz, d), lambda j: (jnp.where(j < n_lat_tiles, 0, 1), 0, 0, 0)),
        pl.BlockSpec((1, d), const2),
        _resident((1, d, w_lru), lambda j: (layer, 0, 0)),
        _resident((1, d, w_fft), lambda j: (layer, 0, off_fft // w_fft)),
        _resident((1, d, w_pool), lambda j: (layer, 0, off_pool // w_pool)),
        _resident((1,) + cd.shape[1:], lambda j: (layer, 0, 0)),
        pl.BlockSpec((CONV_TAPS, w_lru), const2),
        pl.BlockSpec((1, w_lru), const2),
    ]
    out_specs = [
        pl.BlockSpec((r, w_lru), lambda j: (jnp.maximum(j - 1, 0), 0)),
        pl.BlockSpec((1, tt, DFT_GROUP_ROWS, w_fft), lambda j: (jnp.minimum(j, n_tiles - 1), 0, 0, 0)),
        pl.BlockSpec((r, w_pool), clamp),
        pl.BlockSpec((r, d), clamp),
    ]
    out_shape = [
        jax.ShapeDtypeStruct((n_all, w_lru), BF16),
        jax.ShapeDtypeStruct((n_tiles, tt, DFT_GROUP_ROWS, w_fft), BF16),
        jax.ShapeDtypeStruct((n_all, w_pool), BF16),
        jax.ShapeDtypeStruct((n_all, d), BF16),
    ]
    if first:
        out_specs.append(pl.BlockSpec((r, d), clamp))
        out_shape.append(jax.ShapeDtypeStruct((n_all, d), F32))
    return pl.pallas_call(
        functools.partial(_k1_body, first, n_lat_tiles, n_tiles),
        grid=(n_tiles + 1,),
        in_specs=in_specs,
        out_specs=out_specs,
        out_shape=out_shape,
        scratch_shapes=[pltpu.VMEM((r, w_lru), F32), pltpu.VMEM(((CONV_TAPS - 2) * SUBLANES, w_lru), F32)],
        compiler_params=_params(),
        name="norm_uproj",
    )(*tokens, mod_k, g, w_all, w_all, w_all, cd, conv_w, conv_b.reshape(1, w_lru))


def _lru_chunk(d, j, n_lat_chunks, n_ctx_chunks):
    in_ctx = j < n_ctx_chunks
    jj = j - n_ctx_chunks
    ctx_ck = n_lat_chunks + jnp.where(d == 0, j, n_ctx_chunks - 1 - j)
    lat_ck = jnp.where(d == 0, jj, n_lat_chunks - 1 - jj)
    return jnp.where(in_ctx, ctx_ck, lat_ck)


def _lru_body(steps, xh_ref, wg_ref, ba_ref, bx_ref, lam_ref, o_ref, a_sc, b_sc, h_sc, state):
    d = pl.program_id(0)
    j = pl.program_id(1)
    cb = xh_ref.shape[1]

    @pl.when(j == 0)
    def _():
        state[...] = jnp.zeros_like(state)

    lam = lam_ref[0]
    half_rate = (-0.5 * LRU_POWER) * (jnp.maximum(-lam, 0.0) + jnp.log1p(jnp.exp(-jnp.abs(lam))))
    half_ba = 0.5 * ba_ref[0]
    half_bx = 0.5 * bx_ref[0]
    gw = wg_ref.shape[2]
    for g in range(cb // gw):
        sl = slice(g * gw, (g + 1) * gw)
        xb = xh_ref[:, sl]
        gates = jnp.dot(xb, wg_ref[0, g], preferred_element_type=F32)
        log_a = (1.0 + jnp.tanh(gates[:, :gw] + half_ba[:, sl])) * half_rate[:, sl]
        a = jnp.exp(log_a)
        q = jnp.tanh(log_a) * (-1.0 - a * a)
        gain = jnp.where(q > 0.0, q * lax.rsqrt(q), 0.0)
        a_sc[:, sl] = a
        b_sc[:, sl] = gain * ((1.0 + jnp.tanh(gates[:, gw:] + half_bx[:, sl])) * xb.astype(F32))

    n_blocks = steps // SCAN_UNROLL
    block_rows = SCAN_UNROLL * SUBLANES

    def scan(reverse):
        def block(i, h):
            base = pl.multiple_of((n_blocks - 1 - i if reverse else i) * block_rows, block_rows)
            for k in (range(SCAN_UNROLL - 1, -1, -1) if reverse else range(SCAN_UNROLL)):
                r0 = base + k * SUBLANES
                h = a_sc[pl.ds(r0, SUBLANES), :] * h + b_sc[pl.ds(r0, SUBLANES), :]
                h_sc[pl.ds(r0, SUBLANES), :] = h
            return h

        state[...] = lax.fori_loop(0, n_blocks, block, state[...])

    @pl.when(d == 0)
    def _():
        scan(False)

    @pl.when(d == 1)
    def _():
        scan(True)

    o_ref[0] = h_sc[...].astype(BF16)


def _lru(xh, wg, ba, bx, lam, n_lat_rows, n_ctx_rows, steps):
    n_all, w = xh.shape
    rc = steps * SUBLANES
    n_lat_chunks = n_lat_rows // rc
    n_ctx_chunks = n_ctx_rows // rc
    n_chunks = n_lat_chunks + n_ctx_chunks
    gw = wg.shape[2]
    ck = functools.partial(_lru_chunk, n_lat_chunks=n_lat_chunks, n_ctx_chunks=n_ctx_chunks)
    return pl.pallas_call(
        functools.partial(_lru_body, steps),
        grid=(2, n_chunks),
        in_specs=[
            pl.BlockSpec((rc, w), lambda d, j: (ck(d, j), 0)),
            pl.BlockSpec((1, w // gw, gw, 2 * gw), lambda d, j: (d, 0, 0, 0)),
            pl.BlockSpec((1, 1, w), lambda d, j: (d, 0, 0)),
            pl.BlockSpec((1, 1, w), lambda d, j: (d, 0, 0)),
            pl.BlockSpec((1, 1, w), lambda d, j: (d, 0, 0)),
        ],
        out_specs=pl.BlockSpec((1, rc, w), lambda d, j: (d, ck(d, j), 0)),
        out_shape=jax.ShapeDtypeStruct((2, n_all, w), BF16),
        scratch_shapes=[
            pltpu.VMEM((rc, w), F32),
            pltpu.VMEM((rc, w), F32),
            pltpu.VMEM((rc, w), F32),
            pltpu.VMEM((SUBLANES, w), F32),
        ],
        compiler_params=_params(2),
        name="rglru_scan",
    )(xh, wg, ba.reshape(2, 1, w), bx.reshape(2, 1, w), lam.reshape(2, 1, w))


def _dft_constants(n2):
    n1 = DFT_RADIX
    n = n1 * n2
    eye = np.eye(SUBLANES)
    k2 = np.arange(n2)
    ang1 = 2.0 * np.pi * ((k2[:, None] * k2[None, :]) % n2) / n2
    c1, s1 = np.cos(ang1) / np.sqrt(n2), np.sin(ang1) / np.sqrt(n2)
    base1 = np.stack([np.stack([c1, s1], axis=-1), np.stack([-s1, c1], axis=-1)], axis=1)
    m1 = jnp.asarray(np.kron(base1.reshape(2 * n2, 2 * n2), eye), F32)
    k1 = np.arange(n1)
    ang2 = 2.0 * np.pi * ((k1[:, None] * k1[None, :]) % n1) / n1
    base2 = np.stack([np.cos(ang2), np.sin(ang2)], axis=-1) / np.sqrt(n1)
    m2 = jnp.asarray(np.kron(base2.reshape(n1, 2 * n1), eye), F32)
    angt = 2.0 * np.pi * ((k1[:, None] * k2[None, :]) % n) / n
    shape = (n1, n2, SUBLANES, LANES)

    def table(vals):
        return jnp.asarray(np.ascontiguousarray(np.broadcast_to(vals[:, :, None, None], shape)), F32)

    return m1.astype(BF16), m2.astype(BF16), table(np.cos(angt)), table(np.sin(angt))


def _dft_body(steps1, v_ref, m1_ref, twc_ref, tws_ref, m2_ref, y_ref, zs):
    s = pl.program_id(1)
    n2, a_blk, grp, w = v_ref.shape
    n1 = zs.shape[1]
    pair = grp // SUBLANES

    @pl.when(s < steps1)
    def _():
        reps = w // LANES
        for i in range(a_blk):
            xs = v_ref[:, i].reshape(n2 * grp, w)
            r = jnp.dot(m1_ref[...], xs, preferred_element_type=F32).reshape(n2, 2, SUBLANES, w)
            rr, ri = r[:, 0], r[:, 1]
            cw = jnp.concatenate([twc_ref[i]] * reps, axis=-1)
            sw = jnp.concatenate([tws_ref[i]] * reps, axis=-1)
            z = jnp.stack([rr * cw + ri * sw, ri * cw - rr * sw], axis=1)
            zs[:, s * a_blk + i] = z.reshape(n2, grp, w).astype(BF16)

    @pl.when(s >= steps1)
    def _():
        n_out = y_ref.shape[1] * pair
        c0 = (s - steps1) * n_out
        ys = [jnp.dot(m2_ref[...], zs[c0 + i].reshape(n1 * grp, w), preferred_element_type=F32)
              .reshape(n1, SUBLANES, w) for i in range(n_out)]
        y_ref[...] = jnp.stack(ys, axis=1).reshape(n1, n_out // pair, grp, w).astype(BF16)


def _position_dft(v4, n2, block0):
    m1, m2, twc, tws = _dft_constants(n2)
    n1 = DFT_RADIX
    grp, w = v4.shape[2], v4.shape[3]
    wh = min(w, DFT_LANE_BLOCK)
    a_blk = min(n1, max(1, DFT_STAGE1_ROWS // (n2 * grp)))
    steps1 = n1 // a_blk
    pair = grp // SUBLANES
    groups2 = min(DFT_STAGE2_GROUPS, n2 // pair)
    stage1_step = lambda s: jnp.minimum(s, steps1 - 1)
    return pl.pallas_call(
        functools.partial(_dft_body, steps1),
        grid=(w // wh, steps1 + n2 // (pair * groups2)),
        in_specs=[
            pl.BlockSpec((n2, a_blk, grp, wh), lambda c, s: (block0, stage1_step(s), 0, c)),
            _resident(m1.shape, lambda c, s: (0, 0)),
            pl.BlockSpec((a_blk, n2, SUBLANES, LANES), lambda c, s: (stage1_step(s), 0, 0, 0)),
            pl.BlockSpec((a_blk, n2, SUBLANES, LANES), lambda c, s: (stage1_step(s), 0, 0, 0)),
            _resident(m2.shape, lambda c, s: (0, 0)),
        ],
        out_specs=pl.BlockSpec((n1, groups2, grp, wh), lambda c, s: (0, jnp.maximum(s - steps1, 0), 0, c)),
        out_shape=jax.ShapeDtypeStruct((n1, n2 // pair, grp, w), BF16),
        scratch_shapes=[pltpu.VMEM((n2, n1, grp, wh), BF16)],
        compiler_params=_params(2),
        name="position_dft",
    )(v4, m1, twc, tws, m2)


def _pool_minus_token(ue, rows, t0, t_seq):
    halo = POOL_HALO_ROWS
    tvec = t0 + lax.broadcasted_iota(jnp.int32, (rows, LANES), 0) // SUBLANES
    parts = []
    for g, win in enumerate(POOL_WINDOW_SIZES):
        col = ue[:, g * LANES:(g + 1) * LANES]
        n = col.shape[0]
        acc = col[0:n - SUBLANES] + col[SUBLANES:n]
        e0 = SUBLANES
        span = 1
        while 2 * span < win:
            sh = span * SUBLANES
            n = acc.shape[0]
            acc = acc[0:n - 2 * sh] + acc[2 * sh:n]
            e0 += sh
            span *= 2
        wsum = acc[halo - e0:halo - e0 + rows]
        half = win // 2
        cnt = jnp.minimum(tvec + half, t_seq) - jnp.maximum(tvec - half, 0)
        parts.append(wsum / cnt.astype(F32) - col[halo:halo + rows])
    return jnp.concatenate(parts, axis=1)


def _k3_body(cfg, *refs):
    last, n_lat_tiles, n_tiles, t_lat, t_ctx = cfg
    refs = list(refs)
    x_ref, h_ref, mod_ref, hf_ref, hb_ref, yl_ref = refs[:6]
    refs = refs[6:]
    yc_ref = None if last else refs.pop(0)
    (up_ref, upp_ref, upn_ref, wzl_ref, wzf_ref, wzp_ref, wg0_ref, wg1_ref, wg2_ref, pa_ref, pb_ref, pc_ref,
     wo_ref, pw_ref, ps_ref, fg_ref, o_ref) = refs
    j = pl.program_id(0)
    is_lat = j < n_lat_tiles
    rows, d = x_ref.shape
    bsz = SUBLANES
    tt = rows // bsz
    h = h_ref[...]

    z_lru = jnp.dot(h, wzl_ref[0], preferred_element_type=F32)
    y_lru = hf_ref[0].astype(F32) + hb_ref[0].astype(F32)
    ya = jnp.dot((y_lru * _silu(z_lru)).astype(BF16), pa_ref[...], preferred_element_type=F32)

    z_fft = jnp.dot(h, wzf_ref[0], preferred_element_type=F32)
    w = yl_ref.shape[-1]
    y_fft = yl_ref[...].reshape(rows, w)
    if not last:
        y_fft = jnp.where(is_lat, y_fft, yc_ref[...].reshape(rows, w))
    yb = jnp.dot((y_fft.astype(F32) * _silu(z_fft)).astype(BF16), pb_ref[...], preferred_element_type=F32)

    z_pool = jnp.dot(h, wzp_ref[0], preferred_element_type=F32)
    seq_first, seq_last = _seq_edges(j, n_lat_tiles, n_tiles)
    ue = jnp.concatenate([
        upp_ref[...].astype(F32) * jnp.where(seq_first, 0.0, 1.0).astype(F32),
        up_ref[...].astype(F32),
        upn_ref[...].astype(F32) * jnp.where(seq_last, 0.0, 1.0).astype(F32)], axis=0)
    t0 = jnp.where(is_lat, j, j - n_lat_tiles) * tt
    t_seq = jnp.where(is_lat, t_lat, t_ctx)
    p = _pool_minus_token(ue, rows, t0, t_seq)
    y_pool = jnp.dot(p.astype(BF16), pw_ref[...], preferred_element_type=F32) * ps_ref[...]
    yc = jnp.dot((y_pool * _silu(z_pool)).astype(BF16), pc_ref[...], preferred_element_type=F32)

    m = _sigmoid(jnp.dot(h, wg0_ref[0], preferred_element_type=F32)) * ya
    m = m + _sigmoid(jnp.dot(h, wg1_ref[0], preferred_element_type=F32)) * yb
    m = m + _sigmoid(jnp.dot(h, wg2_ref[0], preferred_element_type=F32)) * yc
    out = jnp.dot(m.astype(BF16), wo_ref[...], preferred_element_type=F32)
    xn = x_ref[...].reshape(tt, bsz, d) + mod_ref[0, 2] * out.reshape(tt, bsz, d)
    if last:
        ms = jnp.mean(xn * xn, axis=-1, keepdims=True)
        xn = xn * lax.rsqrt(ms + NORM_EPS) * fg_ref[...]
        o_ref[...] = jnp.transpose(xn, (1, 0, 2))
    else:
        o_ref[...] = xn.reshape(rows, d)


def _k3(x_rows, h, mod_k, h_lru, y_lat, y_ctx, u_pool, w_all, layer, pa, pb, pc, wo, pw, ps, fg, last,
        n_lat_tiles, t_lat, t_ctx):
    n_all, d = h.shape
    bsz = SUBLANES
    r = TOKEN_TILE_ROWS
    tt = r // bsz
    n_tiles = n_all // r
    grid_tiles = n_lat_tiles if last else n_tiles
    w_lru = h_lru.shape[2]
    w_fft = y_lat.shape[-1]
    w_pool = u_pool.shape[1]
    hr = r // POOL_HALO_ROWS
    n_halo = n_all // POOL_HALO_ROWS
    cfg = (last, n_lat_tiles, n_tiles, t_lat, t_ctx)
    const2 = lambda j: (0, 0)
    off_zl, off_zf, off_zp = w_lru, 2 * w_lru + w_fft, 2 * w_lru + 2 * w_fft + w_pool
    off_g = 2 * w_lru + 2 * w_fft + 2 * w_pool
    assert off_zl % w_lru == 0 and off_zf % w_fft == 0 and off_zp % w_pool == 0 and off_g % d == 0

    def dft_spec(y4, first_tile):
        n2 = y4.shape[1] * y4.shape[2] // bsz
        assert tt % n2 == 0
        blk = (tt // n2,) + y4.shape[1:]
        n_blk = y4.shape[0] // blk[0]
        return pl.BlockSpec(blk, lambda j: (jnp.clip(j - first_tile, 0, n_blk - 1), 0, 0, 0))

    row_spec = lambda width: pl.BlockSpec((r, width), lambda j: (j, 0))
    pair_spec = lambda k: pl.BlockSpec((1, r, w_lru), lambda j: (k, j, 0))
    in_specs = [
        row_spec(d), row_spec(d),
        pl.BlockSpec((1, 3, bsz, d), lambda j: (jnp.where(j < n_lat_tiles, 0, 1), 0, 0, 0)),
        pair_spec(0), pair_spec(1),
        dft_spec(y_lat, 0),
    ]
    args = [x_rows, h, mod_k, h_lru, h_lru, y_lat]
    if not last:
        in_specs.append(dft_spec(y_ctx, n_lat_tiles))
        args.append(y_ctx)
    in_specs += [
        row_spec(w_pool),
        pl.BlockSpec((POOL_HALO_ROWS, w_pool), lambda j: (jnp.maximum(j * hr - 1, 0), 0)),
        pl.BlockSpec((POOL_HALO_ROWS, w_pool), lambda j: (jnp.minimum((j + 1) * hr, n_halo - 1), 0)),
        _resident((1, d, w_lru), lambda j: (layer, 0, off_zl // w_lru)),
        _resident((1, d, w_fft), lambda j: (layer, 0, off_zf // w_fft)),
        _resident((1, d, w_pool), lambda j: (layer, 0, off_zp // w_pool)),
        _resident((1, d, d), lambda j: (layer, 0, off_g // d)),
        _resident((1, d, d), lambda j: (layer, 0, off_g // d + 1)),
        _resident((1, d, d), lambda j: (layer, 0, off_g // d + 2)),
    ]
    args += [u_pool, u_pool, u_pool] + [w_all] * 6
    for wgt in (pa, pb, pc, wo, pw, ps, fg):
        in_specs.append(_resident(wgt.shape, const2))
        args.append(wgt)
    if last:
        out_spec = pl.BlockSpec((bsz, tt, d), lambda j: (0, j, 0))
        out_shape = jax.ShapeDtypeStruct((bsz, t_lat, d), F32)
    else:
        out_spec = row_spec(d)
        out_shape = jax.ShapeDtypeStruct((n_all, d), F32)
    return pl.pallas_call(
        functools.partial(_k3_body, cfg),
        grid=(grid_tiles,),
        in_specs=in_specs,
        out_specs=out_spec,
        out_shape=out_shape,
        compiler_params=_params(),
        name="merge_residual",
    )(*args)


def _position_code(n_tokens, d):
    rows = n_tokens // GRID_WIDTH
    quarter = d // 4
    omega = 1.0 / (POSITION_BASE ** (jnp.arange(quarter, dtype=F32) / quarter))

    def emb(n):
        ang = jnp.arange(n).astype(F32)[:, None] * omega[None, :]
        return jnp.concatenate([jnp.sin(ang), jnp.cos(ang)], axis=-1)

    row_code = jnp.repeat(emb(rows), GRID_WIDTH, axis=0)
    col_code = jnp.tile(emb(GRID_WIDTH), (rows, 1))
    return jnp.concatenate([row_code, col_code], axis=-1).astype(F32)


def _block_diag(w):
    g, n, _ = w.shape
    eye = jnp.eye(g, dtype=w.dtype)
    return jnp.einsum("gij,gk->gikj", w, eye).reshape(g * n, g * n)


def _channel_dft_matrix(groups, n):
    k = np.arange(n)
    ang = 2.0 * np.pi * ((k[:, None] * k[None, :]) % n) / n
    eye = np.eye(groups)
    cd = np.concatenate([np.kron(eye, np.cos(ang) / np.sqrt(n)), np.kron(eye, -np.sin(ang) / np.sqrt(n))], axis=1)
    return jnp.asarray(cd, F32)


def _fold_body(cd_ref, fw_ref, o_ref):
    w = fw_ref.shape[1]
    parts = [jnp.dot(cd_ref[:, i * w:(i + 1) * w], fw_ref[0], preferred_element_type=F32,
                     precision=lax.Precision.HIGHEST) for i in range(2)]
    o_ref[0] = jnp.concatenate(parts, axis=1).astype(BF16)


def _fold_channel_map(cd, fw):
    depth, w, _ = fw.shape
    return pl.pallas_call(
        _fold_body,
        grid=(depth,),
        in_specs=[pl.BlockSpec(cd.shape, lambda l: (0, 0)), pl.BlockSpec((1, w, w), lambda l: (l, 0, 0))],
        out_specs=pl.BlockSpec((1,) + cd.shape, lambda l: (l, 0, 0)),
        out_shape=jax.ShapeDtypeStruct((depth,) + cd.shape, BF16),
        compiler_params=_params(),
        name="fold_fft_map",
    )(cd, fw)


def kernel(x, c, ctx, c_ctx, norm_g, ada_w, ada_b, w_in, conv_w, conv_b, lru_wa, lru_ba, lru_wx, lru_bx, lru_lam,
           fft_w, pool_w, pool_scale, proj_a, proj_b, proj_c, w_out, final_g):
    bsz, t_lat, d = x.shape
    t_ctx = ctx.shape[1]
    depth = w_in.shape[0]
    w_lru = conv_w.shape[-1]
    fft_groups, fft_dim = fft_w.shape[1], fft_w.shape[2]
    w_fft = fft_groups * fft_dim
    w_pool = pool_w.shape[1] * pool_w.shape[2]
    heads, head_dim = lru_wa.shape[2], lru_wa.shape[3]
    assert bsz == SUBLANES and fft_dim == LANES and pool_w.shape[2] == LANES and conv_w.shape[1] == CONV_TAPS
    assert t_lat % DFT_RADIX == 0 and t_ctx % DFT_RADIX == 0 and (t_lat // DFT_RADIX) % (t_ctx // DFT_RADIX) == 0
    n_lat_rows, n_ctx_rows = t_lat * bsz, t_ctx * bsz
    n_all = n_lat_rows + n_ctx_rows
    n_lat_tiles = n_lat_rows // TOKEN_TILE_ROWS
    lru_steps = min(LRU_CHUNK_STEPS, t_ctx)
    heads_per_group = LRU_GATE_WIDTH // head_dim

    pos = _position_code(t_lat, d)
    w_all = w_in.astype(BF16)
    cc =jnp.concatenate([c, jnp.broadcast_to(c_ctx[None, :], (bsz, d))], axis=0)
    mod = _modulation(cc, ada_w, ada_b)
    cdw = _fold_channel_map(_channel_dft_matrix(fft_groups, fft_dim), jax.vmap(_block_diag)(fft_w))
    n2_lat, n2_ctx = t_lat // DFT_RADIX, t_ctx // DFT_RADIX

    def gate_w(w):
        wd = w.reshape(2 * heads // heads_per_group, heads_per_group, head_dim, head_dim)
        return jax.vmap(_block_diag)(wd).reshape(2, heads // heads_per_group, LRU_GATE_WIDTH, LRU_GATE_WIDTH)

    tokens = (x, _batch_major_to_rows(ctx), pos)
    out = None
    for l in range(depth):
        first, last = l == 0, l == depth - 1
        mod_k = mod[l].reshape(2, bsz, 3, d).transpose(0, 2, 1, 3)
        g = norm_g[l].reshape(1, d)
        k1_out = _k1(tokens, mod_k, g, w_all, l, cdw, conv_w[l], conv_b[l], first, n_all, n_lat_tiles,
                     w_lru, w_fft, w_pool)
        xh, v4, u_pool, h = k1_out[:4]
        x_rows = k1_out[4] if first else tokens[0]

        wg = jnp.concatenate([gate_w(lru_wa[l]), gate_w(lru_wx[l])], axis=-1).astype(BF16)
        h_lru = _lru(xh, wg, lru_ba[l], lru_bx[l], lru_lam[l], n_lat_rows, n_ctx_rows, lru_steps)

        y_lat = _position_dft(v4, n2_lat, 0)
        y_ctx = None if last else _position_dft(v4, n2_ctx, n2_lat // n2_ctx)

        out = _k3(x_rows, h, mod_k, h_lru, y_lat, y_ctx, u_pool, w_all, l,
                  proj_a[l].astype(BF16), proj_b[l].astype(BF16), proj_c[l].astype(BF16), w_out[l].astype(BF16),
                  _block_diag(pool_w[l]).astype(BF16),
                  pool_scale[l].reshape(1, w_pool), final_g.reshape(1, d), last, n_lat_tiles, t_lat, t_ctx)
        tokens = (out,)
    return out
```

```python
import functools

import numpy as np
import jax
import jax.numpy as jnp
from jax import lax
from jax.experimental import pallas as pl
from jax.experimental.pallas import tpu as pltpu

F32 = jnp.float32
BF16 = jnp.bfloat16

GRID_WIDTH = 64
LRU_POWER = 8.0
NORM_EPS = 1e-6
POSITION_BASE = 10000.0
POOL_WINDOW_SIZES = (2, 4, 8, 16)
CONV_TAPS = 4
SUBLANES = 8
LANES = 128
DFT_RADIX = 64
DFT_GROUP_ROWS = 2 * SUBLANES
DFT_STAGE1_ROWS = 4096
DFT_STAGE2_GROUPS = 2
DFT_LANE_BLOCK = 256
POOL_HALO_ROWS = 64
TOKEN_TILE_ROWS = DFT_RADIX * SUBLANES
LRU_CHUNK_STEPS = 128
LRU_GATE_WIDTH = 256
SCAN_UNROLL = 8
VMEM_LIMIT_BYTES = 56 * 1024 * 1024


def _half_tanh_sigmoid(x_half):
    return 0.5 * (1.0 + jnp.tanh(x_half))


def _sigmoid(x):
    return _half_tanh_sigmoid(0.5 * x)


def _silu(x):
    return x * _sigmoid(x)


def _params(n_axes=1):
    return pltpu.CompilerParams(dimension_semantics=("arbitrary",) * n_axes, vmem_limit_bytes=VMEM_LIMIT_BYTES)


def _resident(shape, index_map):
    return pl.BlockSpec(shape, index_map, pipeline_mode=pl.Buffered(1))


def _mod_body(cc_ref, w_ref, b_ref, o_ref):
    s = _silu(cc_ref[...]).astype(BF16)
    o_ref[0] = jnp.dot(s, w_ref[0].astype(BF16), preferred_element_type=F32) + b_ref[0]


def _modulation(cc, ada_w, ada_b):
    depth, d, n = ada_w.shape
    tn = 1024
    return pl.pallas_call(
        _mod_body,
        grid=(depth, n // tn),
        in_specs=[
            pl.BlockSpec((2 * SUBLANES, d), lambda l, i: (0, 0)),
            pl.BlockSpec((1, d, tn), lambda l, i: (l, 0, i)),
            pl.BlockSpec((1, 1, tn), lambda l, i: (l, 0, i)),
        ],
        out_specs=pl.BlockSpec((1, 2 * SUBLANES, tn), lambda l, i: (l, 0, i)),
        out_shape=jax.ShapeDtypeStruct((depth, 2 * SUBLANES, n), F32),
        compiler_params=_params(2),
        name="adaln_mod",
    )(cc, ada_w, ada_b.reshape(depth, 1, n))


def _rows_body(x_ref, o_ref):
    bsz, tt, d = x_ref.shape
    o_ref[...] = jnp.transpose(x_ref[...], (1, 0, 2)).reshape(tt * bsz, d)


def _batch_major_to_rows(x):
    bsz, t, d = x.shape
    tt = TOKEN_TILE_ROWS // bsz
    return pl.pallas_call(
        _rows_body,
        grid=(t // tt,),
        in_specs=[pl.BlockSpec((bsz, tt, d), lambda j: (0, j, 0))],
        out_specs=pl.BlockSpec((tt * bsz, d), lambda j: (j, 0)),
        out_shape=jax.ShapeDtypeStruct((t * bsz, d), x.dtype),
        compiler_params=_params(),
        name="ctx_rows",
    )(x)


def _seq_edges(p, n_lat_tiles, n_tiles):
    is_first = jnp.logical_or(p == 0, p == n_lat_tiles)
    is_last = jnp.logical_or(p == n_lat_tiles - 1, p == n_tiles - 1)
    return is_first, is_last


def _k1_body(first, n_lat_tiles, n_tiles, *refs):
    n_tok = 3 if first else 1
    tok_refs = refs[:n_tok]
    refs = list(refs[n_tok:])
    (mod_ref, g_ref, wl_ref, wf_ref, wp_ref, ws_ref, cd_ref, cw_ref, cb_ref, xh_ref, v_ref, upool_ref, h_ref,
     wz_ref) = refs[:14]
    refs = refs[14:]
    x0_ref = refs.pop(0) if first else None
    uprev, tail, wl_s, wf_s, wp_s = refs
    j = pl.program_id(0)
    rows, d = h_ref.shape
    bsz = SUBLANES
    tt = rows // bsz
    tail_rows = tail.shape[0]

    @pl.when(j == 0)
    def _():
        uprev[...] = jnp.zeros_like(uprev)
        tail[...] = jnp.zeros_like(tail)
        wl_s[...] = wl_ref[0].astype(BF16)
        wf_s[...] = wf_ref[0].astype(BF16)
        wp_s[...] = wp_ref[0].astype(BF16)

    wz_ref[...] = ws_ref[0].astype(BF16)

    if first:
        x_ref, ctx_ref, pos_ref = tok_refs
        x3 = jnp.where(j < n_lat_tiles, jnp.transpose(x_ref[...] + pos_ref[...][None], (1, 0, 2)),
                       ctx_ref[...].reshape(tt, bsz, d))
        x0_ref[...] = x3.reshape(rows, d)
    else:
        x3 = tok_refs[0][...].reshape(tt, bsz, d)
    ms = jnp.mean(x3 * x3, axis=-1, keepdims=True)
    gain = g_ref[...] * (1.0 + mod_ref[0, 1])
    h = (x3 * lax.rsqrt(ms + NORM_EPS) * gain + mod_ref[0, 0]).reshape(rows, d).astype(BF16)
    h_ref[...] = h
    u = jnp.dot(h, wl_s[...], preferred_element_type=F32)
    ufft = jnp.dot(h, wf_s[...], preferred_element_type=F32).astype(BF16)
    v = jnp.dot(ufft, cd_ref[0], preferred_element_type=F32)
    w = v.shape[1] // 2
    v4 = jnp.stack([v[:, :w].reshape(tt, bsz, w), v[:, w:].reshape(tt, bsz, w)], axis=1)
    v_ref[...] = v4.reshape(1, tt, DFT_GROUP_ROWS, w).astype(BF16)
    upool_ref[...] = jnp.dot(h, wp_s[...], preferred_element_type=F32).astype(BF16)

    p_first, p_last = _seq_edges(j - 1, n_lat_tiles, n_tiles)
    up = uprev[...]
    before = jnp.where(p_first, 0.0, tail[...])
    after = jnp.where(p_last, 0.0, u[0:SUBLANES, :])
    ext = jnp.concatenate([before, up, after], axis=0)
    cwh = 0.5 * cw_ref[...]
    xh = 0.5 * cb_ref[...] + cwh[0:1] * ext[0:rows]
    for k in range(1, CONV_TAPS):
        xh = xh + cwh[k:k + 1] * ext[k * SUBLANES:k * SUBLANES + rows]
    xh_ref[...] = xh.astype(BF16)

    tail[...] = up[rows - tail_rows:rows, :]
    uprev[...] = u


def _k1(tokens, mod_k, g, w_all, layer, cd, conv_w, conv_b, first, n_all, n_lat_tiles, w_lru, w_fft, w_pool):
    d = g.shape[-1]
    bsz = SUBLANES
    r = TOKEN_TILE_ROWS
    tt = r // bsz
    n_tiles = n_all // r
    n_ctx_tiles = n_tiles - n_lat_tiles
    const2 = lambda j: (0, 0)
    off_fft, off_pool = 2 * w_lru, 2 * w_lru + 2 * w_fft
    assert off_fft % w_fft == 0 and off_pool % w_pool == 0
    clamp = lambda j: (jnp.minimum(j, n_tiles - 1), 0)

    segs = _merge_weight_segments(w_lru, w_fft, w_pool, d)
    total = sum(width for _, width in segs)
    slab_w = next(c for c in (LANES, 2 * LANES, 4 * LANES, 8 * LANES) if total // c <= n_tiles + 1)
    assert all(off % slab_w == 0 and width % slab_w == 0 for off, width in segs)
    n_slabs = total // slab_w
    bounds = np.cumsum([0] + [width // slab_w for _, width in segs])

    def slab_source(j):
        s = jnp.minimum(j, n_slabs - 1)
        blk = segs[-1][0] // slab_w + (s - int(bounds[-2]))
        for k in range(len(segs) - 2, -1, -1):
            blk = jnp.where(s < int(bounds[k + 1]), segs[k][0] // slab_w + (s - int(bounds[k])), blk)
        return blk

    if first:
        tok_specs = [
            pl.BlockSpec((bsz, tt, d), lambda j: (0, jnp.minimum(j, n_lat_tiles - 1), 0)),
            pl.BlockSpec((r, d), lambda j: (jnp.clip(j - n_lat_tiles, 0, n_ctx_tiles - 1), 0)),
            pl.BlockSpec((tt, d), lambda j: (jnp.minimum(j, n_lat_tiles - 1), 0)),
        ]
    else:
        tok_specs = [pl.BlockSpec((r, d), clamp)]
    in_specs = tok_specs + [
        pl.BlockSpec((1, 3, bsz, d), lambda j: (jnp.where(j < n_lat_tiles, 0, 1), 0, 0, 0)),
        pl.BlockSpec((1, d), const2),
        _resident((1, d, w_lru), lambda j: (layer, 0, 0)),
        _resident((1, d, w_fft), lambda j: (layer, 0, off_fft // w_fft)),
        _resident((1, d, w_pool), lambda j: (layer, 0, off_pool // w_pool)),
        pl.BlockSpec((1, d, slab_w), lambda j: (layer, 0, slab_source(j))),
        _resident((1,) + cd.shape[1:], lambda j: (layer, 0, 0)),
        pl.BlockSpec((CONV_TAPS, w_lru), const2),
        pl.BlockSpec((1, w_lru), const2),
    ]
    out_specs = [
        pl.BlockSpec((r, w_lru), lambda j: (jnp.maximum(j - 1, 0), 0)),
        pl.BlockSpec((1, tt, DFT_GROUP_ROWS, w_fft), lambda j: (jnp.minimum(j, n_tiles - 1), 0, 0, 0)),
        pl.BlockSpec((r, w_pool), clamp),
        pl.BlockSpec((r, d), clamp),
        pl.BlockSpec((d, slab_w), lambda j: (0, jnp.minimum(j, n_slabs - 1))),
    ]
    out_shape = [
        jax.ShapeDtypeStruct((n_all, w_lru), BF16),
        jax.ShapeDtypeStruct((n_tiles, tt, DFT_GROUP_ROWS, w_fft), BF16),
        jax.ShapeDtypeStruct((n_all, w_pool), BF16),
        jax.ShapeDtypeStruct((n_all, d), BF16),
        jax.ShapeDtypeStruct((d, total), BF16),
    ]
    if first:
        out_specs.append(pl.BlockSpec((r, d), clamp))
        out_shape.append(jax.ShapeDtypeStruct((n_all, d), F32))
    return pl.pallas_call(
        functools.partial(_k1_body, first, n_lat_tiles, n_tiles),
        grid=(n_tiles + 1,),
        in_specs=in_specs,
        out_specs=out_specs,
        out_shape=out_shape,
        scratch_shapes=[pltpu.VMEM((r, w_lru), F32), pltpu.VMEM(((CONV_TAPS - 2) * SUBLANES, w_lru), F32),
                        pltpu.VMEM((d, w_lru), BF16), pltpu.VMEM((d, w_fft), BF16), pltpu.VMEM((d, w_pool), BF16)],
        compiler_params=_params(),
        name="norm_uproj",
    )(*tokens, mod_k, g, w_all, w_all, w_all, w_all, cd, conv_w, conv_b.reshape(1, w_lru))


def _merge_weight_segments(w_lru, w_fft, w_pool, d):
    return [(w_lru, w_lru), (2 * w_lru + w_fft, w_fft), (2 * w_lru + 2 * w_fft + w_pool, w_pool),
            (2 * w_lru + 2 * w_fft + 2 * w_pool, 3 * d)]


def _lru_chunk(d, j, n_lat_chunks, n_ctx_chunks):
    in_ctx = j < n_ctx_chunks
    jj = j - n_ctx_chunks
    ctx_ck = n_lat_chunks + jnp.where(d == 0, j, n_ctx_chunks - 1 - j)
    lat_ck = jnp.where(d == 0, jj, n_lat_chunks - 1 - jj)
    return jnp.where(in_ctx, ctx_ck, lat_ck)


def _lru_body(steps, xh_ref, wg_ref, ba_ref, bx_ref, lam_ref, o_ref, a_sc, b_sc, h_sc, state):
    d = pl.program_id(0)
    j = pl.program_id(1)
    cb = xh_ref.shape[1]

    @pl.when(j == 0)
    def _():
        state[...] = jnp.zeros_like(state)

    lam = lam_ref[0]
    half_rate = (-0.5 * LRU_POWER) * (jnp.maximum(-lam, 0.0) + jnp.log1p(jnp.exp(-jnp.abs(lam))))
    half_ba = 0.5 * ba_ref[0]
    half_bx = 0.5 * bx_ref[0]
    gw = wg_ref.shape[2]
    for g in range(cb // gw):
        sl = slice(g * gw, (g + 1) * gw)
        xb = xh_ref[:, sl]
        gates = jnp.dot(xb, wg_ref[0, g], preferred_element_type=F32)
        log_a = (1.0 + jnp.tanh(gates[:, :gw] + half_ba[:, sl])) * half_rate[:, sl]
        a = jnp.exp(log_a)
        q = jnp.tanh(log_a) * (-1.0 - a * a)
        gain = jnp.where(q > 0.0, q * lax.rsqrt(q), 0.0)
        a_sc[:, sl] = a
        b_sc[:, sl] = gain * ((1.0 + jnp.tanh(gates[:, gw:] + half_bx[:, sl])) * xb.astype(F32))

    n_blocks = steps // SCAN_UNROLL
    block_rows = SCAN_UNROLL * SUBLANES

    def scan(reverse):
        def block(i, h):
            base = pl.multiple_of((n_blocks - 1 - i if reverse else i) * block_rows, block_rows)
            for k in (range(SCAN_UNROLL - 1, -1, -1) if reverse else range(SCAN_UNROLL)):
                r0 = base + k * SUBLANES
                h = a_sc[pl.ds(r0, SUBLANES), :] * h + b_sc[pl.ds(r0, SUBLANES), :]
                h_sc[pl.ds(r0, SUBLANES), :] = h
            return h

        state[...] = lax.fori_loop(0, n_blocks, block, state[...])

    @pl.when(d == 0)
    def _():
        scan(False)

    @pl.when(d == 1)
    def _():
        scan(True)

    o_ref[0] = h_sc[...].astype(BF16)


def _lru(xh, wg, ba, bx, lam, n_lat_rows, n_ctx_rows, steps):
    n_all, w = xh.shape
    rc = steps * SUBLANES
    n_lat_chunks = n_lat_rows // rc
    n_ctx_chunks = n_ctx_rows // rc
    n_chunks = n_lat_chunks + n_ctx_chunks
    gw = wg.shape[2]
    ck = functools.partial(_lru_chunk, n_lat_chunks=n_lat_chunks, n_ctx_chunks=n_ctx_chunks)
    return pl.pallas_call(
        functools.partial(_lru_body, steps),
        grid=(2, n_chunks),
        in_specs=[
            pl.BlockSpec((rc, w), lambda d, j: (ck(d, j), 0)),
            pl.BlockSpec((1, w // gw, gw, 2 * gw), lambda d, j: (d, 0, 0, 0)),
            pl.BlockSpec((1, 1, w), lambda d, j: (d, 0, 0)),
            pl.BlockSpec((1, 1, w), lambda d, j: (d, 0, 0)),
            pl.BlockSpec((1, 1, w), lambda d, j: (d, 0, 0)),
        ],
        out_specs=pl.BlockSpec((1, rc, w), lambda d, j: (d, ck(d, j), 0)),
        out_shape=jax.ShapeDtypeStruct((2, n_all, w), BF16),
        scratch_shapes=[
            pltpu.VMEM((rc, w), F32),
            pltpu.VMEM((rc, w), F32),
            pltpu.VMEM((rc, w), F32),
            pltpu.VMEM((SUBLANES, w), F32),
        ],
        compiler_params=_params(2),
        name="rglru_scan",
    )(xh, wg, ba.reshape(2, 1, w), bx.reshape(2, 1, w), lam.reshape(2, 1, w))


def _dft_constants(n2):
    n1 = DFT_RADIX
    n = n1 * n2
    eye = np.eye(SUBLANES)
    k2 = np.arange(n2)
    ang1 = 2.0 * np.pi * ((k2[:, None] * k2[None, :]) % n2) / n2
    c1, s1 = np.cos(ang1) / np.sqrt(n2), np.sin(ang1) / np.sqrt(n2)
    base1 = np.stack([np.stack([c1, s1], axis=-1), np.stack([-s1, c1], axis=-1)], axis=1)
    m1 = jnp.asarray(np.kron(base1.reshape(2 * n2, 2 * n2), eye), F32)
    k1 = np.arange(n1)
    ang2 = 2.0 * np.pi * ((k1[:, None] * k1[None, :]) % n1) / n1
    base2 = np.stack([np.cos(ang2), np.sin(ang2)], axis=-1) / np.sqrt(n1)
    m2 = jnp.asarray(np.kron(base2.reshape(n1, 2 * n1), eye), F32)
    angt = 2.0 * np.pi * ((k1[:, None] * k2[None, :]) % n) / n
    shape = (n1, n2, SUBLANES, LANES)

    def table(vals):
        return jnp.asarray(np.ascontiguousarray(np.broadcast_to(vals[:, :, None, None], shape)), F32)

    return m1.astype(BF16), m2.astype(BF16), table(np.cos(angt)), table(np.sin(angt))


def _dft_body(steps1, v_ref, m1_ref, twc_ref, tws_ref, m2_ref, y_ref, zs):
    s = pl.program_id(1)
    n2, a_blk, grp, w = v_ref.shape
    n1 = zs.shape[1]
    pair = grp // SUBLANES

    @pl.when(s < steps1)
    def _():
        reps = w // LANES
        for i in range(a_blk):
            xs = v_ref[:, i].reshape(n2 * grp, w)
            r = jnp.dot(m1_ref[...], xs, preferred_element_type=F32).reshape(n2, 2, SUBLANES, w)
            rr, ri = r[:, 0], r[:, 1]
            cw = jnp.concatenate([twc_ref[i]] * reps, axis=-1)
            sw = jnp.concatenate([tws_ref[i]] * reps, axis=-1)
            z = jnp.stack([rr * cw + ri * sw, ri * cw - rr * sw], axis=1)
            zs[:, s * a_blk + i] = z.reshape(n2, grp, w).astype(BF16)

    @pl.when(s >= steps1)
    def _():
        n_out = y_ref.shape[1] * pair
        c0 = (s - steps1) * n_out
        ys = [jnp.dot(m2_ref[...], zs[c0 + i].reshape(n1 * grp, w), preferred_element_type=F32)
              .reshape(n1, SUBLANES, w) for i in range(n_out)]
        y_ref[...] = jnp.stack(ys, axis=1).reshape(n1, n_out // pair, grp, w).astype(BF16)


def _position_dft(v4, n2, block0):
    m1, m2, twc, tws = _dft_constants(n2)
    n1 = DFT_RADIX
    grp, w = v4.shape[2], v4.shape[3]
    wh = min(w, DFT_LANE_BLOCK)
    a_blk = min(n1, max(1, DFT_STAGE1_ROWS // (n2 * grp)))
    steps1 = n1 // a_blk
    pair = grp // SUBLANES
    groups2 = min(DFT_STAGE2_GROUPS, n2 // pair)
    stage1_step = lambda s: jnp.minimum(s, steps1 - 1)
    return pl.pallas_call(
        functools.partial(_dft_body, steps1),
        grid=(w // wh, steps1 + n2 // (pair * groups2)),
        in_specs=[
            pl.BlockSpec((n2, a_blk, grp, wh), lambda c, s: (block0, stage1_step(s), 0, c)),
            _resident(m1.shape, lambda c, s: (0, 0)),
            pl.BlockSpec((a_blk, n2, SUBLANES, LANES), lambda c, s: (stage1_step(s), 0, 0, 0)),
            pl.BlockSpec((a_blk, n2, SUBLANES, LANES), lambda c, s: (stage1_step(s), 0, 0, 0)),
            _resident(m2.shape, lambda c, s: (0, 0)),
        ],
        out_specs=pl.BlockSpec((n1, groups2, grp, wh), lambda c, s: (0, jnp.maximum(s - steps1, 0), 0, c)),
        out_shape=jax.ShapeDtypeStruct((n1, n2 // pair, grp, w), BF16),
        scratch_shapes=[pltpu.VMEM((n2, n1, grp, wh), BF16)],
        compiler_params=_params(2),
        name="position_dft",
    )(v4, m1, twc, tws, m2)


def _pool_minus_token(ue, rows, t0, t_seq):
    halo = POOL_HALO_ROWS
    tvec = t0 + lax.broadcasted_iota(jnp.int32, (rows, LANES), 0) // SUBLANES
    parts = []
    for g, win in enumerate(POOL_WINDOW_SIZES):
        col = ue[:, g * LANES:(g + 1) * LANES]
        n = col.shape[0]
        acc = col[0:n - SUBLANES] + col[SUBLANES:n]
        e0 = SUBLANES
        span = 1
        while 2 * span < win:
            sh = span * SUBLANES
            n = acc.shape[0]
            acc = acc[0:n - 2 * sh] + acc[2 * sh:n]
            e0 += sh
            span *= 2
        wsum = acc[halo - e0:halo - e0 + rows]
        half = win // 2
        cnt = jnp.minimum(tvec + half, t_seq) - jnp.maximum(tvec - half, 0)
        parts.append(wsum / cnt.astype(F32) - col[halo:halo + rows])
    return jnp.concatenate(parts, axis=1)


def _k3_body(cfg, *refs):
    last, n_lat_tiles, n_tiles, t_lat, t_ctx = cfg
    refs = list(refs)
    x_ref, h_ref, mod_ref, hf_ref, hb_ref, yl_ref = refs[:6]
    refs = refs[6:]
    yc_ref = None if last else refs.pop(0)
    (up_ref, upp_ref, upn_ref, wzl_ref, wzf_ref, wzp_ref, wg0_ref, wg1_ref, wg2_ref, pa_ref, pb_ref, pc_ref,
     wo_ref, pw_ref, ps_ref, fg_ref, o_ref) = refs
    j = pl.program_id(0)
    is_lat = j < n_lat_tiles
    rows, d = x_ref.shape
    bsz = SUBLANES
    tt = rows // bsz
    h = h_ref[...]

    seq_first, seq_last = _seq_edges(j, n_lat_tiles, n_tiles)
    ue = jnp.concatenate([
        upp_ref[...].astype(F32) * jnp.where(seq_first, 0.0, 1.0).astype(F32),
        up_ref[...].astype(F32),
        upn_ref[...].astype(F32) * jnp.where(seq_last, 0.0, 1.0).astype(F32)], axis=0)
    t0 = jnp.where(is_lat, j, j - n_lat_tiles) * tt
    t_seq = jnp.where(is_lat, t_lat, t_ctx)
    p = _pool_minus_token(ue, rows, t0, t_seq).astype(BF16)

    z_lru = jnp.dot(h, wzl_ref[...], preferred_element_type=F32)
    y_lru = hf_ref[0].astype(F32) + hb_ref[0].astype(F32)
    ya = jnp.dot((y_lru * _silu(z_lru)).astype(BF16), pa_ref[...], preferred_element_type=F32)

    z_fft = jnp.dot(h, wzf_ref[...], preferred_element_type=F32)
    w = yl_ref.shape[-1]
    y_fft = yl_ref[...].reshape(rows, w)
    if not last:
        y_fft = jnp.where(is_lat, y_fft, yc_ref[...].reshape(rows, w))
    yb = jnp.dot((y_fft.astype(F32) * _silu(z_fft)).astype(BF16), pb_ref[...], preferred_element_type=F32)

    z_pool = jnp.dot(h, wzp_ref[...], preferred_element_type=F32)
    y_pool = jnp.dot(p, pw_ref[...], preferred_element_type=F32) * ps_ref[...]
    yc = jnp.dot((y_pool * _silu(z_pool)).astype(BF16), pc_ref[...], preferred_element_type=F32)

    m = _sigmoid(jnp.dot(h, wg0_ref[...], preferred_element_type=F32)) * ya
    m = m + _sigmoid(jnp.dot(h, wg1_ref[...], preferred_element_type=F32)) * yb
    m = m + _sigmoid(jnp.dot(h, wg2_ref[...], preferred_element_type=F32)) * yc
    out = jnp.dot(m.astype(BF16), wo_ref[...], preferred_element_type=F32)
    xn = x_ref[...].reshape(tt, bsz, d) + mod_ref[0, 2] * out.reshape(tt, bsz, d)
    if last:
        ms = jnp.mean(xn * xn, axis=-1, keepdims=True)
        xn = xn * lax.rsqrt(ms + NORM_EPS) * fg_ref[...]
        o_ref[...] = jnp.transpose(xn, (1, 0, 2))
    else:
        o_ref[...] = xn.reshape(rows, d)


def _k3(x_rows, h, mod_k, h_lru, y_lat, y_ctx, u_pool, wz, pa, pb, pc, wo, pw, ps, fg, last,
        n_lat_tiles, t_lat, t_ctx):
    n_all, d = h.shape
    bsz = SUBLANES
    r = TOKEN_TILE_ROWS
    tt = r // bsz
    n_tiles = n_all // r
    grid_tiles = n_lat_tiles if last else n_tiles
    w_lru = h_lru.shape[2]
    w_fft = y_lat.shape[-1]
    w_pool = u_pool.shape[1]
    hr = r // POOL_HALO_ROWS
    n_halo = n_all // POOL_HALO_ROWS
    cfg = (last, n_lat_tiles, n_tiles, t_lat, t_ctx)
    const2 = lambda j: (0, 0)
    off_zl, off_zf, off_zp, off_g = 0, w_lru, w_lru + w_fft, w_lru + w_fft + w_pool
    assert [width for _, width in _merge_weight_segments(w_lru, w_fft, w_pool, d)] == [w_lru, w_fft, w_pool, 3 * d]
    assert off_zf % w_fft == 0 and off_zp % w_pool == 0 and off_g % d == 0

    def dft_spec(y4, first_tile):
        n2 = y4.shape[1] * y4.shape[2] // bsz
        assert tt % n2 == 0
        blk = (tt // n2,) + y4.shape[1:]
        n_blk = y4.shape[0] // blk[0]
        return pl.BlockSpec(blk, lambda j: (jnp.clip(j - first_tile, 0, n_blk - 1), 0, 0, 0))

    row_spec = lambda width: pl.BlockSpec((r, width), lambda j: (j, 0))
    pair_spec = lambda k: pl.BlockSpec((1, r, w_lru), lambda j: (k, j, 0))
    in_specs = [
        row_spec(d), row_spec(d),
        pl.BlockSpec((1, 3, bsz, d), lambda j: (jnp.where(j < n_lat_tiles, 0, 1), 0, 0, 0)),
        pair_spec(0), pair_spec(1),
        dft_spec(y_lat, 0),
    ]
    args = [x_rows, h, mod_k, h_lru, h_lru, y_lat]
    if not last:
        in_specs.append(dft_spec(y_ctx, n_lat_tiles))
        args.append(y_ctx)
    in_specs += [
        row_spec(w_pool),
        pl.BlockSpec((POOL_HALO_ROWS, w_pool), lambda j: (jnp.maximum(j * hr - 1, 0), 0)),
        pl.BlockSpec((POOL_HALO_ROWS, w_pool), lambda j: (jnp.minimum((j + 1) * hr, n_halo - 1), 0)),
        _resident((d, w_lru), lambda j: (0, off_zl // w_lru)),
        _resident((d, w_fft), lambda j: (0, off_zf // w_fft)),
        _resident((d, w_pool), lambda j: (0, off_zp // w_pool)),
        _resident((d, d), lambda j: (0, off_g // d)),
        _resident((d, d), lambda j: (0, off_g // d + 1)),
        _resident((d, d), lambda j: (0, off_g // d + 2)),
    ]
    args += [u_pool, u_pool, u_pool] + [wz] * 6
    for wgt in (pa, pb, pc, wo, pw, ps, fg):
        in_specs.append(_resident(wgt.shape, const2))
        args.append(wgt)
    if last:
        out_spec = pl.BlockSpec((bsz, tt, d), lambda j: (0, j, 0))
        out_shape = jax.ShapeDtypeStruct((bsz, t_lat, d), F32)
    else:
        out_spec = row_spec(d)
        out_shape = jax.ShapeDtypeStruct((n_all, d), F32)
    return pl.pallas_call(
        functools.partial(_k3_body, cfg),
        grid=(grid_tiles,),
        in_specs=in_specs,
        out_specs=out_spec,
        out_shape=out_shape,
        compiler_params=_params(),
        name="merge_residual",
    )(*args)


def _position_code(n_tokens, d):
    rows = n_tokens // GRID_WIDTH
    quarter = d // 4
    omega = 1.0 / (POSITION_BASE ** (jnp.arange(quarter, dtype=F32) / quarter))

    def emb(n):
        ang = jnp.arange(n).astype(F32)[:, None] * omega[None, :]
        return jnp.concatenate([jnp.sin(ang), jnp.cos(ang)], axis=-1)

    row_code = jnp.repeat(emb(rows), GRID_WIDTH, axis=0)
    col_code = jnp.tile(emb(GRID_WIDTH), (rows, 1))
    return jnp.concatenate([row_code, col_code], axis=-1).astype(F32)


def _block_diag(w):
    g, n, _ = w.shape
    eye = jnp.eye(g, dtype=w.dtype)
    return jnp.einsum("gij,gk->gikj", w, eye).reshape(g * n, g * n)


def _channel_dft_matrix(groups, n):
    k = np.arange(n)
    ang = 2.0 * np.pi * ((k[:, None] * k[None, :]) % n) / n
    eye = np.eye(groups)
    cd = np.concatenate([np.kron(eye, np.cos(ang) / np.sqrt(n)), np.kron(eye, -np.sin(ang) / np.sqrt(n))], axis=1)
    return jnp.asarray(cd, F32)


def _fold_body(cd_ref, fw_ref, o_ref):
    w = fw_ref.shape[1]
    parts = [jnp.dot(cd_ref[:, i * w:(i + 1) * w], fw_ref[0], preferred_element_type=F32,
                     precision=lax.Precision.HIGHEST) for i in range(2)]
    o_ref[0] = jnp.concatenate(parts, axis=1).astype(BF16)


def _fold_channel_map(cd, fw):
    depth, w, _ = fw.shape
    return pl.pallas_call(
        _fold_body,
        grid=(depth,),
        in_specs=[pl.BlockSpec(cd.shape, lambda l: (0, 0)), pl.BlockSpec((1, w, w), lambda l: (l, 0, 0))],
        out_specs=pl.BlockSpec((1,) + cd.shape, lambda l: (l, 0, 0)),
        out_shape=jax.ShapeDtypeStruct((depth,) + cd.shape, BF16),
        compiler_params=_params(),
        name="fold_fft_map",
    )(cd, fw)


def kernel(x, c, ctx, c_ctx, norm_g, ada_w, ada_b, w_in, conv_w, conv_b, lru_wa, lru_ba, lru_wx, lru_bx, lru_lam,
           fft_w, pool_w, pool_scale, proj_a, proj_b, proj_c, w_out, final_g):
    bsz, t_lat, d = x.shape
    t_ctx = ctx.shape[1]
    depth = w_in.shape[0]
    w_lru = conv_w.shape[-1]
    fft_groups, fft_dim = fft_w.shape[1], fft_w.shape[2]
    w_fft = fft_groups * fft_dim
    w_pool = pool_w.shape[1] * pool_w.shape[2]
    heads, head_dim = lru_wa.shape[2], lru_wa.shape[3]
    assert bsz == SUBLANES and fft_dim == LANES and pool_w.shape[2] == LANES and conv_w.shape[1] == CONV_TAPS
    assert t_lat % DFT_RADIX == 0 and t_ctx % DFT_RADIX == 0 and (t_lat // DFT_RADIX) % (t_ctx // DFT_RADIX) == 0
    n_lat_rows, n_ctx_rows = t_lat * bsz, t_ctx * bsz
    n_all = n_lat_rows + n_ctx_rows
    n_lat_tiles = n_lat_rows // TOKEN_TILE_ROWS
    lru_steps = min(LRU_CHUNK_STEPS, t_ctx)
    heads_per_group = LRU_GATE_WIDTH // head_dim

    pos = _position_code(t_lat, d)
    cc =jnp.concatenate([c, jnp.broadcast_to(c_ctx[None, :], (bsz, d))], axis=0)
    mod = _modulation(cc, ada_w, ada_b)
    cdw = _fold_channel_map(_channel_dft_matrix(fft_groups, fft_dim), jax.vmap(_block_diag)(fft_w))
    n2_lat, n2_ctx = t_lat // DFT_RADIX, t_ctx // DFT_RADIX

    def gate_w(w):
        wd = w.reshape(2 * heads // heads_per_group, heads_per_group, head_dim, head_dim)
        return jax.vmap(_block_diag)(wd).reshape(2, heads // heads_per_group, LRU_GATE_WIDTH, LRU_GATE_WIDTH)

    tokens = (x, _batch_major_to_rows(ctx), pos)
    out = None
    for l in range(depth):
        first, last = l == 0, l == depth - 1
        mod_k = mod[l].reshape(2, bsz, 3, d).transpose(0, 2, 1, 3)
        g = norm_g[l].reshape(1, d)
        k1_out = _k1(tokens, mod_k, g, w_in, l, cdw, conv_w[l], conv_b[l], first, n_all, n_lat_tiles,
                     w_lru, w_fft, w_pool)
        xh, v4, u_pool, h, wz = k1_out[:5]
        x_rows = k1_out[5] if first else tokens[0]

        wg = jnp.concatenate([gate_w(lru_wa[l]), gate_w(lru_wx[l])], axis=-1).astype(BF16)
        h_lru = _lru(xh, wg, lru_ba[l], lru_bx[l], lru_lam[l], n_lat_rows, n_ctx_rows, lru_steps)

        y_lat = _position_dft(v4, n2_lat, 0)
        y_ctx = None if last else _position_dft(v4, n2_ctx, n2_lat // n2_ctx)

        out = _k3(x_rows, h, mod_k, h_lru, y_lat, y_ctx, u_pool, wz,
                  proj_a[l].astype(BF16), proj_b[l].astype(BF16), proj_c[l].astype(BF16), w_out[l].astype(BF16),
                  _block_diag(pool_w[l]).astype(BF16),
                  pool_scale[l].reshape(1, w_pool), final_g.reshape(1, d), last, n_lat_tiles, t_lat, t_ctx)
        tokens = (out,)
    return out
```

```python
import functools

import numpy as np
import jax
import jax.numpy as jnp
from jax import lax
from jax.experimental import pallas as pl
from jax.experimental.pallas import tpu as pltpu

F32 = jnp.float32
BF16 = jnp.bfloat16

GRID_WIDTH = 64
LRU_POWER = 8.0
NORM_EPS = 1e-6
POSITION_BASE = 10000.0
POOL_WINDOW_SIZES = (2, 4, 8, 16)
CONV_TAPS = 4
SUBLANES = 8
LANES = 128
DFT_RADIX = 64
DFT_GROUP_ROWS = 2 * SUBLANES
DFT_STAGE1_ROWS = 4096
DFT_STAGE2_GROUPS = 2
DFT_LANE_BLOCK = 256
POOL_HALO_ROWS = 64
TOKEN_TILE_ROWS = DFT_RADIX * SUBLANES
LRU_CHUNK_STEPS = 128
LRU_GATE_WIDTH = 256
SCAN_UNROLL = 8
VMEM_LIMIT_BYTES = 56 * 1024 * 1024


def _half_tanh_sigmoid(x_half):
    return 0.5 * (1.0 + jnp.tanh(x_half))


def _sigmoid(x):
    return _half_tanh_sigmoid(0.5 * x)


def _silu(x):
    return x * _sigmoid(x)


def _params(n_axes=1):
    return pltpu.CompilerParams(dimension_semantics=("arbitrary",) * n_axes, vmem_limit_bytes=VMEM_LIMIT_BYTES)


def _resident(shape, index_map):
    return pl.BlockSpec(shape, index_map, pipeline_mode=pl.Buffered(1))


def _mod_body(cc_ref, w_ref, b_ref, o_ref):
    s = _silu(cc_ref[...]).astype(BF16)
    o_ref[0] = jnp.dot(s, w_ref[0].astype(BF16), preferred_element_type=F32) + b_ref[0]


def _modulation(cc, ada_w, ada_b):
    depth, d, n = ada_w.shape
    tn = 1024
    return pl.pallas_call(
        _mod_body,
        grid=(depth, n // tn),
        in_specs=[
            pl.BlockSpec((2 * SUBLANES, d), lambda l, i: (0, 0)),
            pl.BlockSpec((1, d, tn), lambda l, i: (l, 0, i)),
            pl.BlockSpec((1, 1, tn), lambda l, i: (l, 0, i)),
        ],
        out_specs=pl.BlockSpec((1, 2 * SUBLANES, tn), lambda l, i: (l, 0, i)),
        out_shape=jax.ShapeDtypeStruct((depth, 2 * SUBLANES, n), F32),
        compiler_params=_params(2),
        name="adaln_mod",
    )(cc, ada_w, ada_b.reshape(depth, 1, n))


def _rows_body(x_ref, o_ref):
    bsz, tt, d = x_ref.shape
    o_ref[...] = jnp.transpose(x_ref[...], (1, 0, 2)).reshape(tt * bsz, d)


def _batch_major_to_rows(x):
    bsz, t, d = x.shape
    tt = TOKEN_TILE_ROWS // bsz
    return pl.pallas_call(
        _rows_body,
        grid=(t // tt,),
        in_specs=[pl.BlockSpec((bsz, tt, d), lambda j: (0, j, 0))],
        out_specs=pl.BlockSpec((tt * bsz, d), lambda j: (j, 0)),
        out_shape=jax.ShapeDtypeStruct((t * bsz, d), x.dtype),
        compiler_params=_params(),
        name="ctx_rows",
    )(x)


def _seq_edges(p, n_lat_tiles, n_tiles):
    is_first = jnp.logical_or(p == 0, p == n_lat_tiles)
    is_last = jnp.logical_or(p == n_lat_tiles - 1, p == n_tiles - 1)
    return is_first, is_last


def _k1_body(first, n_lat_tiles, n_tiles, *refs):
    n_tok = 3 if first else 1
    tok_refs = refs[:n_tok]
    refs = list(refs[n_tok:])
    mod_ref, g_ref, wl_ref, wf_ref, wp_ref, cd_ref, cw_ref, cb_ref, xh_ref, v_ref, upool_ref, h_ref = refs[:12]
    refs = refs[12:]
    x0_ref = refs.pop(0) if first else None
    uprev, tail, wl_s, wf_s, wp_s = refs
    j = pl.program_id(0)
    rows, d = h_ref.shape
    bsz = SUBLANES
    tt = rows // bsz
    tail_rows = tail.shape[0]

    @pl.when(j == 0)
    def _():
        uprev[...] = jnp.zeros_like(uprev)
        tail[...] = jnp.zeros_like(tail)
        wl_s[...] = wl_ref[0].astype(BF16)
        wf_s[...] = wf_ref[0].astype(BF16)
        wp_s[...] = wp_ref[0].astype(BF16)


    if first:
        x_ref, ctx_ref, pos_ref = tok_refs
        x3 = jnp.where(j < n_lat_tiles, jnp.transpose(x_ref[...] + pos_ref[...][None], (1, 0, 2)),
                       ctx_ref[...].reshape(tt, bsz, d))
        x0_ref[...] = x3.reshape(rows, d)
    else:
        x3 = tok_refs[0][...].reshape(tt, bsz, d)
    ms = jnp.mean(x3 * x3, axis=-1, keepdims=True)
    gain = g_ref[...] * (1.0 + mod_ref[0, 1])
    h = (x3 * lax.rsqrt(ms + NORM_EPS) * gain + mod_ref[0, 0]).reshape(rows, d).astype(BF16)
    h_ref[...] = h
    u = jnp.dot(h, wl_s[...], preferred_element_type=F32)
    ufft = jnp.dot(h, wf_s[...], preferred_element_type=F32).astype(BF16)
    v = jnp.dot(ufft, cd_ref[0], preferred_element_type=F32)
    w = v.shape[1] // 2
    v4 = jnp.stack([v[:, :w].reshape(tt, bsz, w), v[:, w:].reshape(tt, bsz, w)], axis=1)
    v_ref[...] = v4.reshape(1, tt, DFT_GROUP_ROWS, w).astype(BF16)
    upool_ref[...] = jnp.dot(h, wp_s[...], preferred_element_type=F32).astype(BF16)

    p_first, p_last = _seq_edges(j - 1, n_lat_tiles, n_tiles)
    up = uprev[...]
    before = jnp.where(p_first, 0.0, tail[...])
    after = jnp.where(p_last, 0.0, u[0:SUBLANES, :])
    ext = jnp.concatenate([before, up, after], axis=0)
    cwh = 0.5 * cw_ref[...]
    xh = 0.5 * cb_ref[...] + cwh[0:1] * ext[0:rows]
    for k in range(1, CONV_TAPS):
        xh = xh + cwh[k:k + 1] * ext[k * SUBLANES:k * SUBLANES + rows]
    xh_ref[...] = xh.astype(BF16)

    tail[...] = up[rows - tail_rows:rows, :]
    uprev[...] = u


def _k1(tokens, mod_k, g, w_all, layer, cd, conv_w, conv_b, first, n_all, n_lat_tiles, w_lru, w_fft, w_pool):
    d = g.shape[-1]
    bsz = SUBLANES
    r = TOKEN_TILE_ROWS
    tt = r // bsz
    n_tiles = n_all // r
    n_ctx_tiles = n_tiles - n_lat_tiles
    const2 = lambda j: (0, 0)
    off_fft, off_pool = 2 * w_lru, 2 * w_lru + 2 * w_fft
    assert off_fft % w_fft == 0 and off_pool % w_pool == 0
    clamp = lambda j: (jnp.minimum(j, n_tiles - 1), 0)

    if first:
        tok_specs = [
            pl.BlockSpec((bsz, tt, d), lambda j: (0, jnp.minimum(j, n_lat_tiles - 1), 0)),
            pl.BlockSpec((r, d), lambda j: (jnp.clip(j - n_lat_tiles, 0, n_ctx_tiles - 1), 0)),
            pl.BlockSpec((tt, d), lambda j: (jnp.minimum(j, n_lat_tiles - 1), 0)),
        ]
    else:
        tok_specs = [pl.BlockSpec((r, d), clamp)]
    in_specs = tok_specs + [
        pl.BlockSpec((1, 3, bsz, d), lambda j: (jnp.where(j < n_lat_tiles, 0, 1), 0, 0, 0)),
        pl.BlockSpec((1, d), const2),
        _resident((1, d, w_lru), lambda j: (layer, 0, 0)),
        _resident((1, d, w_fft), lambda j: (layer, 0, off_fft // w_fft)),
        _resident((1, d, w_pool), lambda j: (layer, 0, off_pool // w_pool)),
        _resident((1,) + cd.shape[1:], lambda j: (layer, 0, 0)),
        pl.BlockSpec((CONV_TAPS, w_lru), const2),
        pl.BlockSpec((1, w_lru), const2),
    ]
    out_specs = [
        pl.BlockSpec((r, w_lru), lambda j: (jnp.maximum(j - 1, 0), 0)),
        pl.BlockSpec((1, tt, DFT_GROUP_ROWS, w_fft), lambda j: (jnp.minimum(j, n_tiles - 1), 0, 0, 0)),
        pl.BlockSpec((r, w_pool), clamp),
        pl.BlockSpec((r, d), clamp),
    ]
    out_shape = [
        jax.ShapeDtypeStruct((n_all, w_lru), BF16),
        jax.ShapeDtypeStruct((n_tiles, tt, DFT_GROUP_ROWS, w_fft), BF16),
        jax.ShapeDtypeStruct((n_all, w_pool), BF16),
        jax.ShapeDtypeStruct((n_all, d), BF16),
    ]
    if first:
        out_specs.append(pl.BlockSpec((r, d), clamp))
        out_shape.append(jax.ShapeDtypeStruct((n_all, d), F32))
    return pl.pallas_call(
        functools.partial(_k1_body, first, n_lat_tiles, n_tiles),
        grid=(n_tiles + 1,),
        in_specs=in_specs,
        out_specs=out_specs,
        out_shape=out_shape,
        scratch_shapes=[pltpu.VMEM((r, w_lru), F32), pltpu.VMEM(((CONV_TAPS - 2) * SUBLANES, w_lru), F32),
                        pltpu.VMEM((d, w_lru), BF16), pltpu.VMEM((d, w_fft), BF16), pltpu.VMEM((d, w_pool), BF16)],
        compiler_params=_params(),
        name="norm_uproj",
    )(*tokens, mod_k, g, w_all, w_all, w_all, cd, conv_w, conv_b.reshape(1, w_lru))


def _merge_weight_segments(w_lru, w_fft, w_pool, d):
    return [(w_lru, w_lru), (2 * w_lru + w_fft, w_fft), (2 * w_lru + 2 * w_fft + w_pool, w_pool),
            (2 * w_lru + 2 * w_fft + 2 * w_pool, 3 * d)]


def _lru_chunk(d, j, n_lat_chunks, n_ctx_chunks):
    in_ctx = j < n_ctx_chunks
    jj = j - n_ctx_chunks
    ctx_ck = n_lat_chunks + jnp.where(d == 0, j, n_ctx_chunks - 1 - j)
    lat_ck = jnp.where(d == 0, jj, n_lat_chunks - 1 - jj)
    return jnp.where(in_ctx, ctx_ck, lat_ck)


def _lru_body(steps, xh_ref, wg_ref, ba_ref, bx_ref, lam_ref, ws_ref, o_ref, wz_ref, a_sc, b_sc, h_sc, state):
    d = pl.program_id(0)
    j = pl.program_id(1)
    cb = xh_ref.shape[1]
    wz_ref[...] = ws_ref[0].astype(BF16)

    @pl.when(j == 0)
    def _():
        state[...] = jnp.zeros_like(state)

    lam = lam_ref[0]
    half_rate = (-0.5 * LRU_POWER) * (jnp.maximum(-lam, 0.0) + jnp.log1p(jnp.exp(-jnp.abs(lam))))
    half_ba = 0.5 * ba_ref[0]
    half_bx = 0.5 * bx_ref[0]
    gw = wg_ref.shape[2]
    for g in range(cb // gw):
        sl = slice(g * gw, (g + 1) * gw)
        xb = xh_ref[:, sl]
        gates = jnp.dot(xb, wg_ref[0, g], preferred_element_type=F32)
        log_a = (1.0 + jnp.tanh(gates[:, :gw] + half_ba[:, sl])) * half_rate[:, sl]
        a = jnp.exp(log_a)
        q = jnp.tanh(log_a) * (-1.0 - a * a)
        gain = jnp.where(q > 0.0, q * lax.rsqrt(q), 0.0)
        a_sc[:, sl] = a
        b_sc[:, sl] = gain * ((1.0 + jnp.tanh(gates[:, gw:] + half_bx[:, sl])) * xb.astype(F32))

    n_blocks = steps // SCAN_UNROLL
    block_rows = SCAN_UNROLL * SUBLANES

    def scan(reverse):
        def block(i, h):
            base = pl.multiple_of((n_blocks - 1 - i if reverse else i) * block_rows, block_rows)
            for k in (range(SCAN_UNROLL - 1, -1, -1) if reverse else range(SCAN_UNROLL)):
                r0 = base + k * SUBLANES
                h = a_sc[pl.ds(r0, SUBLANES), :] * h + b_sc[pl.ds(r0, SUBLANES), :]
                h_sc[pl.ds(r0, SUBLANES), :] = h
            return h

        state[...] = lax.fori_loop(0, n_blocks, block, state[...])

    @pl.when(d == 0)
    def _():
        scan(False)

    @pl.when(d == 1)
    def _():
        scan(True)

    o_ref[0] = h_sc[...].astype(BF16)


def _lru(xh, wg, ba, bx, lam, w_in, layer, segs, n_lat_rows, n_ctx_rows, steps):
    n_all, w = xh.shape
    d_model = w_in.shape[1]
    rc = steps * SUBLANES
    n_lat_chunks = n_lat_rows // rc
    n_ctx_chunks = n_ctx_rows // rc
    n_chunks = n_lat_chunks + n_ctx_chunks
    gw = wg.shape[2]
    ck = functools.partial(_lru_chunk, n_lat_chunks=n_lat_chunks, n_ctx_chunks=n_ctx_chunks)

    total = sum(width for _, width in segs)
    slab_w = next(c for c in (LANES, 2 * LANES, 4 * LANES, 8 * LANES) if total // c <= 2 * n_chunks)
    assert all(off % slab_w == 0 and width % slab_w == 0 for off, width in segs)
    n_slabs = total // slab_w
    bounds = np.cumsum([0] + [width // slab_w for _, width in segs])
    slab = lambda d, j: jnp.minimum(d * n_chunks + j, n_slabs - 1)

    def slab_source(d, j):
        s = slab(d, j)
        blk = segs[-1][0] // slab_w + (s - int(bounds[-2]))
        for k in range(len(segs) - 2, -1, -1):
            blk = jnp.where(s < int(bounds[k + 1]), segs[k][0] // slab_w + (s - int(bounds[k])), blk)
        return blk

    return pl.pallas_call(
        functools.partial(_lru_body, steps),
        grid=(2, n_chunks),
        in_specs=[
            pl.BlockSpec((rc, w), lambda d, j: (ck(d, j), 0)),
            pl.BlockSpec((1, w // gw, gw, 2 * gw), lambda d, j: (d, 0, 0, 0)),
            pl.BlockSpec((1, 1, w), lambda d, j: (d, 0, 0)),
            pl.BlockSpec((1, 1, w), lambda d, j: (d, 0, 0)),
            pl.BlockSpec((1, 1, w), lambda d, j: (d, 0, 0)),
            pl.BlockSpec((1, d_model, slab_w), lambda d, j: (layer, 0, slab_source(d, j))),
        ],
        out_specs=[pl.BlockSpec((1, rc, w), lambda d, j: (d, ck(d, j), 0)),
                   pl.BlockSpec((d_model, slab_w), lambda d, j: (0, slab(d, j)))],
        out_shape=[jax.ShapeDtypeStruct((2, n_all, w), BF16), jax.ShapeDtypeStruct((d_model, total), BF16)],
        scratch_shapes=[
            pltpu.VMEM((rc, w), F32),
            pltpu.VMEM((rc, w), F32),
            pltpu.VMEM((rc, w), F32),
            pltpu.VMEM((SUBLANES, w), F32),
        ],
        compiler_params=_params(2),
        name="rglru_scan",
    )(xh, wg, ba.reshape(2, 1, w), bx.reshape(2, 1, w), lam.reshape(2, 1, w), w_in)


def _dft_constants(n2):
    n1 = DFT_RADIX
    n = n1 * n2
    eye = np.eye(SUBLANES)
    k2 = np.arange(n2)
    ang1 = 2.0 * np.pi * ((k2[:, None] * k2[None, :]) % n2) / n2
    c1, s1 = np.cos(ang1) / np.sqrt(n2), np.sin(ang1) / np.sqrt(n2)
    base1 = np.stack([np.stack([c1, s1], axis=-1), np.stack([-s1, c1], axis=-1)], axis=1)
    m1 = jnp.asarray(np.kron(base1.reshape(2 * n2, 2 * n2), eye), F32)
    k1 = np.arange(n1)
    ang2 = 2.0 * np.pi * ((k1[:, None] * k1[None, :]) % n1) / n1
    base2 = np.stack([np.cos(ang2), np.sin(ang2)], axis=-1) / np.sqrt(n1)
    m2 = jnp.asarray(np.kron(base2.reshape(n1, 2 * n1), eye), F32)
    angt = 2.0 * np.pi * ((k1[:, None] * k2[None, :]) % n) / n
    shape = (n1, n2, SUBLANES, LANES)

    def table(vals):
        return jnp.asarray(np.ascontiguousarray(np.broadcast_to(vals[:, :, None, None], shape)), F32)

    return m1.astype(BF16), m2.astype(BF16), table(np.cos(angt)), table(np.sin(angt))


def _dft_body(steps1, v_ref, m1_ref, twc_ref, tws_ref, m2_ref, y_ref, zs):
    s = pl.program_id(1)
    n2, a_blk, grp, w = v_ref.shape
    n1 = zs.shape[1]
    pair = grp // SUBLANES

    @pl.when(s < steps1)
    def _():
        reps = w // LANES
        for i in range(a_blk):
            xs = v_ref[:, i].reshape(n2 * grp, w)
            r = jnp.dot(m1_ref[...], xs, preferred_element_type=F32).reshape(n2, 2, SUBLANES, w)
            rr, ri = r[:, 0], r[:, 1]
            cw = jnp.concatenate([twc_ref[i]] * reps, axis=-1)
            sw = jnp.concatenate([tws_ref[i]] * reps, axis=-1)
            z = jnp.stack([rr * cw + ri * sw, ri * cw - rr * sw], axis=1)
            zs[:, s * a_blk + i] = z.reshape(n2, grp, w).astype(BF16)

    @pl.when(s >= steps1)
    def _():
        n_out = y_ref.shape[1] * pair
        c0 = (s - steps1) * n_out
        ys = [jnp.dot(m2_ref[...], zs[c0 + i].reshape(n1 * grp, w), preferred_element_type=F32)
              .reshape(n1, SUBLANES, w) for i in range(n_out)]
        y_ref[...] = jnp.stack(ys, axis=1).reshape(n1, n_out // pair, grp, w).astype(BF16)


def _position_dft(v4, n2, block0):
    m1, m2, twc, tws = _dft_constants(n2)
    n1 = DFT_RADIX
    grp, w = v4.shape[2], v4.shape[3]
    wh = min(w, DFT_LANE_BLOCK)
    a_blk = min(n1, max(1, DFT_STAGE1_ROWS // (n2 * grp)))
    steps1 = n1 // a_blk
    pair = grp // SUBLANES
    groups2 = min(DFT_STAGE2_GROUPS, n2 // pair)
    stage1_step = lambda s: jnp.minimum(s, steps1 - 1)
    return pl.pallas_call(
        functools.partial(_dft_body, steps1),
        grid=(w // wh, steps1 + n2 // (pair * groups2)),
        in_specs=[
            pl.BlockSpec((n2, a_blk, grp, wh), lambda c, s: (block0, stage1_step(s), 0, c)),
            _resident(m1.shape, lambda c, s: (0, 0)),
            pl.BlockSpec((a_blk, n2, SUBLANES, LANES), lambda c, s: (stage1_step(s), 0, 0, 0)),
            pl.BlockSpec((a_blk, n2, SUBLANES, LANES), lambda c, s: (stage1_step(s), 0, 0, 0)),
            _resident(m2.shape, lambda c, s: (0, 0)),
        ],
        out_specs=pl.BlockSpec((n1, groups2, grp, wh), lambda c, s: (0, jnp.maximum(s - steps1, 0), 0, c)),
        out_shape=jax.ShapeDtypeStruct((n1, n2 // pair, grp, w), BF16),
        scratch_shapes=[pltpu.VMEM((n2, n1, grp, wh), BF16)],
        compiler_params=_params(2),
        name="position_dft",
    )(v4, m1, twc, tws, m2)


def _pool_minus_token(ue, rows, t0, t_seq):
    halo = POOL_HALO_ROWS
    tvec = t0 + lax.broadcasted_iota(jnp.int32, (rows, LANES), 0) // SUBLANES
    parts = []
    for g, win in enumerate(POOL_WINDOW_SIZES):
        col = ue[:, g * LANES:(g + 1) * LANES]
        n = col.shape[0]
        acc = col[0:n - SUBLANES] + col[SUBLANES:n]
        e0 = SUBLANES
        span = 1
        while 2 * span < win:
            sh = span * SUBLANES
            n = acc.shape[0]
            acc = acc[0:n - 2 * sh] + acc[2 * sh:n]
            e0 += sh
            span *= 2
        wsum = acc[halo - e0:halo - e0 + rows]
        half = win // 2
        cnt = jnp.minimum(tvec + half, t_seq) - jnp.maximum(tvec - half, 0)
        parts.append(wsum / cnt.astype(F32) - col[halo:halo + rows])
    return jnp.concatenate(parts, axis=1)


def _k3_body(cfg, *refs):
    last, n_lat_tiles, n_tiles, t_lat, t_ctx = cfg
    refs = list(refs)
    x_ref, h_ref, mod_ref, hf_ref, hb_ref, yl_ref = refs[:6]
    refs = refs[6:]
    yc_ref = None if last else refs.pop(0)
    (up_ref, upp_ref, upn_ref, wzl_ref, wzf_ref, wzp_ref, wg0_ref, wg1_ref, wg2_ref, pa_ref, pb_ref, pc_ref,
     wo_ref, pw_ref, ps_ref, fg_ref, o_ref) = refs
    j = pl.program_id(0)
    is_lat = j < n_lat_tiles
    rows, d = x_ref.shape
    bsz = SUBLANES
    tt = rows // bsz
    h = h_ref[...]

    seq_first, seq_last = _seq_edges(j, n_lat_tiles, n_tiles)
    ue = jnp.concatenate([
        upp_ref[...].astype(F32) * jnp.where(seq_first, 0.0, 1.0).astype(F32),
        up_ref[...].astype(F32),
        upn_ref[...].astype(F32) * jnp.where(seq_last, 0.0, 1.0).astype(F32)], axis=0)
    t0 = jnp.where(is_lat, j, j - n_lat_tiles) * tt
    t_seq = jnp.where(is_lat, t_lat, t_ctx)
    p = _pool_minus_token(ue, rows, t0, t_seq).astype(BF16)

    z_lru = jnp.dot(h, wzl_ref[...], preferred_element_type=F32)
    y_lru = hf_ref[0].astype(F32) + hb_ref[0].astype(F32)
    ya = jnp.dot((y_lru * _silu(z_lru)).astype(BF16), pa_ref[...], preferred_element_type=F32)

    z_fft = jnp.dot(h, wzf_ref[...], preferred_element_type=F32)
    w = yl_ref.shape[-1]
    y_fft = yl_ref[...].reshape(rows, w)
    if not last:
        y_fft = jnp.where(is_lat, y_fft, yc_ref[...].reshape(rows, w))
    yb = jnp.dot((y_fft.astype(F32) * _silu(z_fft)).astype(BF16), pb_ref[...], preferred_element_type=F32)

    z_pool = jnp.dot(h, wzp_ref[...], preferred_element_type=F32)
    y_pool = jnp.dot(p, pw_ref[...], preferred_element_type=F32) * ps_ref[...]
    yc = jnp.dot((y_pool * _silu(z_pool)).astype(BF16), pc_ref[...], preferred_element_type=F32)

    m = _sigmoid(jnp.dot(h, wg0_ref[...], preferred_element_type=F32)) * ya
    m = m + _sigmoid(jnp.dot(h, wg1_ref[...], preferred_element_type=F32)) * yb
    m = m + _sigmoid(jnp.dot(h, wg2_ref[...], preferred_element_type=F32)) * yc
    out = jnp.dot(m.astype(BF16), wo_ref[...], preferred_element_type=F32)
    xn = x_ref[...].reshape(tt, bsz, d) + mod_ref[0, 2] * out.reshape(tt, bsz, d)
    if last:
        ms = jnp.mean(xn * xn, axis=-1, keepdims=True)
        xn = xn * lax.rsqrt(ms + NORM_EPS) * fg_ref[...]
        o_ref[...] = jnp.transpose(xn, (1, 0, 2))
    else:
        o_ref[...] = xn.reshape(rows, d)


def _k3(x_rows, h, mod_k, h_lru, y_lat, y_ctx, u_pool, wz, pa, pb, pc, wo, pw, ps, fg, last,
        n_lat_tiles, t_lat, t_ctx):
    n_all, d = h.shape
    bsz = SUBLANES
    r = TOKEN_TILE_ROWS
    tt = r // bsz
    n_tiles = n_all // r
    grid_tiles = n_lat_tiles if last else n_tiles
    w_lru = h_lru.shape[2]
    w_fft = y_lat.shape[-1]
    w_pool = u_pool.shape[1]
    hr = r // POOL_HALO_ROWS
    n_halo = n_all // POOL_HALO_ROWS
    cfg = (last, n_lat_tiles, n_tiles, t_lat, t_ctx)
    const2 = lambda j: (0, 0)
    off_zl, off_zf, off_zp, off_g = 0, w_lru, w_lru + w_fft, w_lru + w_fft + w_pool
    assert [width for _, width in _merge_weight_segments(w_lru, w_fft, w_pool, d)] == [w_lru, w_fft, w_pool, 3 * d]
    assert off_zf % w_fft == 0 and off_zp % w_pool == 0 and off_g % d == 0

    def dft_spec(y4, first_tile):
        n2 = y4.shape[1] * y4.shape[2] // bsz
        assert tt % n2 == 0
        blk = (tt // n2,) + y4.shape[1:]
        n_blk = y4.shape[0] // blk[0]
        return pl.BlockSpec(blk, lambda j: (jnp.clip(j - first_tile, 0, n_blk - 1), 0, 0, 0))

    row_spec = lambda width: pl.BlockSpec((r, width), lambda j: (j, 0))
    pair_spec = lambda k: pl.BlockSpec((1, r, w_lru), lambda j: (k, j, 0))
    in_specs = [
        row_spec(d), row_spec(d),
        pl.BlockSpec((1, 3, bsz, d), lambda j: (jnp.where(j < n_lat_tiles, 0, 1), 0, 0, 0)),
        pair_spec(0), pair_spec(1),
        dft_spec(y_lat, 0),
    ]
    args = [x_rows, h, mod_k, h_lru, h_lru, y_lat]
    if not last:
        in_specs.append(dft_spec(y_ctx, n_lat_tiles))
        args.append(y_ctx)
    in_specs += [
        row_spec(w_pool),
        pl.BlockSpec((POOL_HALO_ROWS, w_pool), lambda j: (jnp.maximum(j * hr - 1, 0), 0)),
        pl.BlockSpec((POOL_HALO_ROWS, w_pool), lambda j: (jnp.minimum((j + 1) * hr, n_halo - 1), 0)),
        _resident((d, w_lru), lambda j: (0, off_zl // w_lru)),
        _resident((d, w_fft), lambda j: (0, off_zf // w_fft)),
        _resident((d, w_pool), lambda j: (0, off_zp // w_pool)),
        _resident((d, d), lambda j: (0, off_g // d)),
        _resident((d, d), lambda j: (0, off_g // d + 1)),
        _resident((d, d), lambda j: (0, off_g // d + 2)),
    ]
    args += [u_pool, u_pool, u_pool] + [wz] * 6
    for wgt in (pa, pb, pc, wo, pw, ps, fg):
        in_specs.append(_resident(wgt.shape, const2))
        args.append(wgt)
    if last:
        out_spec = pl.BlockSpec((bsz, tt, d), lambda j: (0, j, 0))
        out_shape = jax.ShapeDtypeStruct((bsz, t_lat, d), F32)
    else:
        out_spec = row_spec(d)
        out_shape = jax.ShapeDtypeStruct((n_all, d), F32)
    return pl.pallas_call(
        functools.partial(_k3_body, cfg),
        grid=(grid_tiles,),
        in_specs=in_specs,
        out_specs=out_spec,
        out_shape=out_shape,
        compiler_params=_params(),
        name="merge_residual",
    )(*args)


def _position_code(n_tokens, d):
    rows = n_tokens // GRID_WIDTH
    quarter = d // 4
    omega = 1.0 / (POSITION_BASE ** (jnp.arange(quarter, dtype=F32) / quarter))

    def emb(n):
        ang = jnp.arange(n).astype(F32)[:, None] * omega[None, :]
        return jnp.concatenate([jnp.sin(ang), jnp.cos(ang)], axis=-1)

    row_code = jnp.repeat(emb(rows), GRID_WIDTH, axis=0)
    col_code = jnp.tile(emb(GRID_WIDTH), (rows, 1))
    return jnp.concatenate([row_code, col_code], axis=-1).astype(F32)


def _block_diag(w):
    g, n, _ = w.shape
    eye = jnp.eye(g, dtype=w.dtype)
    return jnp.einsum("gij,gk->gikj", w, eye).reshape(g * n, g * n)


def _channel_dft_matrix(groups, n):
    k = np.arange(n)
    ang = 2.0 * np.pi * ((k[:, None] * k[None, :]) % n) / n
    eye = np.eye(groups)
    cd = np.concatenate([np.kron(eye, np.cos(ang) / np.sqrt(n)), np.kron(eye, -np.sin(ang) / np.sqrt(n))], axis=1)
    return jnp.asarray(cd, F32)


def _fold_body(cd_ref, fw_ref, o_ref):
    w = fw_ref.shape[1]
    parts = [jnp.dot(cd_ref[:, i * w:(i + 1) * w], fw_ref[0], preferred_element_type=F32,
                     precision=lax.Precision.HIGHEST) for i in range(2)]
    o_ref[0] = jnp.concatenate(parts, axis=1).astype(BF16)


def _fold_channel_map(cd, fw):
    depth, w, _ = fw.shape
    return pl.pallas_call(
        _fold_body,
        grid=(depth,),
        in_specs=[pl.BlockSpec(cd.shape, lambda l: (0, 0)), pl.BlockSpec((1, w, w), lambda l: (l, 0, 0))],
        out_specs=pl.BlockSpec((1,) + cd.shape, lambda l: (l, 0, 0)),
        out_shape=jax.ShapeDtypeStruct((depth,) + cd.shape, BF16),
        compiler_params=_params(),
        name="fold_fft_map",
    )(cd, fw)


def kernel(x, c, ctx, c_ctx, norm_g, ada_w, ada_b, w_in, conv_w, conv_b, lru_wa, lru_ba, lru_wx, lru_bx, lru_lam,
           fft_w, pool_w, pool_scale, proj_a, proj_b, proj_c, w_out, final_g):
    bsz, t_lat, d = x.shape
    t_ctx = ctx.shape[1]
    depth = w_in.shape[0]
    w_lru = conv_w.shape[-1]
    fft_groups, fft_dim = fft_w.shape[1], fft_w.shape[2]
    w_fft = fft_groups * fft_dim
    w_pool = pool_w.shape[1] * pool_w.shape[2]
    heads, head_dim = lru_wa.shape[2], lru_wa.shape[3]
    assert bsz == SUBLANES and fft_dim == LANES and pool_w.shape[2] == LANES and conv_w.shape[1] == CONV_TAPS
    assert t_lat % DFT_RADIX == 0 and t_ctx % DFT_RADIX == 0 and (t_lat // DFT_RADIX) % (t_ctx // DFT_RADIX) == 0
    n_lat_rows, n_ctx_rows = t_lat * bsz, t_ctx * bsz
    n_all = n_lat_rows + n_ctx_rows
    n_lat_tiles = n_lat_rows // TOKEN_TILE_ROWS
    lru_steps = min(LRU_CHUNK_STEPS, t_ctx)
    heads_per_group = LRU_GATE_WIDTH // head_dim

    pos = _position_code(t_lat, d)
    cc =jnp.concatenate([c, jnp.broadcast_to(c_ctx[None, :], (bsz, d))], axis=0)
    mod = _modulation(cc, ada_w, ada_b)
    cdw = _fold_channel_map(_channel_dft_matrix(fft_groups, fft_dim), jax.vmap(_block_diag)(fft_w))
    n2_lat, n2_ctx = t_lat // DFT_RADIX, t_ctx // DFT_RADIX

    def gate_w(w):
        wd = w.reshape(2 * heads // heads_per_group, heads_per_group, head_dim, head_dim)
        return jax.vmap(_block_diag)(wd).reshape(2, heads // heads_per_group, LRU_GATE_WIDTH, LRU_GATE_WIDTH)

    tokens = (x, _batch_major_to_rows(ctx), pos)
    out = None
    for l in range(depth):
        first, last = l == 0, l == depth - 1
        mod_k = mod[l].reshape(2, bsz, 3, d).transpose(0, 2, 1, 3)
        g = norm_g[l].reshape(1, d)
        k1_out = _k1(tokens, mod_k, g, w_in, l, cdw, conv_w[l], conv_b[l], first, n_all, n_lat_tiles,
                     w_lru, w_fft, w_pool)
        xh, v4, u_pool, h = k1_out[:4]
        x_rows = k1_out[4] if first else tokens[0]

        wg = jnp.concatenate([gate_w(lru_wa[l]), gate_w(lru_wx[l])], axis=-1).astype(BF16)
        h_lru, wz = _lru(xh, wg, lru_ba[l], lru_bx[l], lru_lam[l], w_in, l,
                         _merge_weight_segments(w_lru, w_fft, w_pool, d), n_lat_rows, n_ctx_rows, lru_steps)

        y_lat = _position_dft(v4, n2_lat, 0)
        y_ctx = None if last else _position_dft(v4, n2_ctx, n2_lat // n2_ctx)

        out = _k3(x_rows, h, mod_k, h_lru, y_lat, y_ctx, u_pool, wz,
                  proj_a[l].astype(BF16), proj_b[l].astype(BF16), proj_c[l].astype(BF16), w_out[l].astype(BF16),
                  _block_diag(pool_w[l]).astype(BF16),
                  pool_scale[l].reshape(1, w_pool), final_g.reshape(1, d), last, n_lat_tiles, t_lat, t_ctx)
        tokens = (out,)
    return out
```

```python
import functools

import numpy as np
import jax
import jax.numpy as jnp
from jax import lax
from jax.experimental import pallas as pl
from jax.experimental.pallas import tpu as pltpu

F32 = jnp.float32
BF16 = jnp.bfloat16

GRID_WIDTH = 64
LRU_POWER = 8.0
NORM_EPS = 1e-6
POSITION_BASE = 10000.0
POOL_WINDOW_SIZES = (2, 4, 8, 16)
CONV_TAPS = 4
SUBLANES = 8
LANES = 128
DFT_RADIX = 64
DFT_GROUP_ROWS = 2 * SUBLANES
DFT_STAGE1_ROWS = 4096
DFT_STAGE2_GROUPS = 2
DFT_LANE_BLOCK = 256
POOL_HALO_ROWS = 64
TOKEN_TILE_ROWS = DFT_RADIX * SUBLANES
LRU_CHUNK_STEPS = 256
LRU_GATE_WIDTH = 256
SCAN_UNROLL = 8
VMEM_LIMIT_BYTES = 56 * 1024 * 1024


def _half_tanh_sigmoid(x_half):
    return 0.5 * (1.0 + jnp.tanh(x_half))


def _sigmoid(x):
    return _half_tanh_sigmoid(0.5 * x)


def _silu(x):
    return x * _sigmoid(x)


def _params(n_axes=1):
    return pltpu.CompilerParams(dimension_semantics=("arbitrary",) * n_axes, vmem_limit_bytes=VMEM_LIMIT_BYTES)


def _resident(shape, index_map):
    return pl.BlockSpec(shape, index_map, pipeline_mode=pl.Buffered(1))


def _mod_body(cc_ref, w_ref, b_ref, o_ref):
    s = _silu(cc_ref[...]).astype(BF16)
    o_ref[0] = jnp.dot(s, w_ref[0].astype(BF16), preferred_element_type=F32) + b_ref[0]


def _modulation(cc, ada_w, ada_b):
    depth, d, n = ada_w.shape
    tn = 1024
    return pl.pallas_call(
        _mod_body,
        grid=(depth, n // tn),
        in_specs=[
            pl.BlockSpec((2 * SUBLANES, d), lambda l, i: (0, 0)),
            pl.BlockSpec((1, d, tn), lambda l, i: (l, 0, i)),
            pl.BlockSpec((1, 1, tn), lambda l, i: (l, 0, i)),
        ],
        out_specs=pl.BlockSpec((1, 2 * SUBLANES, tn), lambda l, i: (l, 0, i)),
        out_shape=jax.ShapeDtypeStruct((depth, 2 * SUBLANES, n), F32),
        compiler_params=_params(2),
        name="adaln_mod",
    )(cc, ada_w, ada_b.reshape(depth, 1, n))


def _rows_body(x_ref, o_ref):
    bsz, tt, d = x_ref.shape
    o_ref[...] = jnp.transpose(x_ref[...], (1, 0, 2)).reshape(tt * bsz, d)


def _batch_major_to_rows(x):
    bsz, t, d = x.shape
    tt = TOKEN_TILE_ROWS // bsz
    return pl.pallas_call(
        _rows_body,
        grid=(t // tt,),
        in_specs=[pl.BlockSpec((bsz, tt, d), lambda j: (0, j, 0))],
        out_specs=pl.BlockSpec((tt * bsz, d), lambda j: (j, 0)),
        out_shape=jax.ShapeDtypeStruct((t * bsz, d), x.dtype),
        compiler_params=_params(),
        name="ctx_rows",
    )(x)


def _seq_edges(p, n_lat_tiles, n_tiles):
    is_first = jnp.logical_or(p == 0, p == n_lat_tiles)
    is_last = jnp.logical_or(p == n_lat_tiles - 1, p == n_tiles - 1)
    return is_first, is_last


def _k1_body(first, n_lat_tiles, n_tiles, *refs):
    n_tok = 3 if first else 1
    tok_refs = refs[:n_tok]
    refs = list(refs[n_tok:])
    mod_ref, g_ref, wl_ref, wf_ref, wp_ref, cd_ref, cw_ref, cb_ref, xh_ref, v_ref, upool_ref, h_ref = refs[:12]
    refs = refs[12:]
    x0_ref = refs.pop(0) if first else None
    uprev, tail, wl_s, wf_s, wp_s = refs
    j = pl.program_id(0)
    rows, d = h_ref.shape
    bsz = SUBLANES
    tt = rows // bsz
    tail_rows = tail.shape[0]

    @pl.when(j == 0)
    def _():
        uprev[...] = jnp.zeros_like(uprev)
        tail[...] = jnp.zeros_like(tail)
        wl_s[...] = wl_ref[0].astype(BF16)
        wf_s[...] = wf_ref[0].astype(BF16)
        wp_s[...] = wp_ref[0].astype(BF16)


    if first:
        x_ref, ctx_ref, pos_ref = tok_refs
        x3 = jnp.where(j < n_lat_tiles, jnp.transpose(x_ref[...] + pos_ref[...][None], (1, 0, 2)),
                       ctx_ref[...].reshape(tt, bsz, d))
        x0_ref[...] = x3.reshape(rows, d)
    else:
        x3 = tok_refs[0][...].reshape(tt, bsz, d)
    ms = jnp.mean(x3 * x3, axis=-1, keepdims=True)
    gain = g_ref[...] * (1.0 + mod_ref[0, 1])
    h = (x3 * lax.rsqrt(ms + NORM_EPS) * gain + mod_ref[0, 0]).reshape(rows, d).astype(BF16)
    h_ref[...] = h
    u = jnp.dot(h, wl_s[...], preferred_element_type=F32)
    ufft = jnp.dot(h, wf_s[...], preferred_element_type=F32).astype(BF16)
    v = jnp.dot(ufft, cd_ref[0], preferred_element_type=F32)
    w = v.shape[1] // 2
    v4 = jnp.stack([v[:, :w].reshape(tt, bsz, w), v[:, w:].reshape(tt, bsz, w)], axis=1)
    v_ref[...] = v4.reshape(1, tt, DFT_GROUP_ROWS, w).astype(BF16)
    upool_ref[...] = jnp.dot(h, wp_s[...], preferred_element_type=F32).astype(BF16)

    p_first, p_last = _seq_edges(j - 1, n_lat_tiles, n_tiles)
    up = uprev[...]
    before = jnp.where(p_first, 0.0, tail[...])
    after = jnp.where(p_last, 0.0, u[0:SUBLANES, :])
    ext = jnp.concatenate([before, up, after], axis=0)
    cwh = 0.5 * cw_ref[...]
    xh = 0.5 * cb_ref[...] + cwh[0:1] * ext[0:rows]
    for k in range(1, CONV_TAPS):
        xh = xh + cwh[k:k + 1] * ext[k * SUBLANES:k * SUBLANES + rows]
    xh_ref[...] = xh.astype(BF16)

    tail[...] = up[rows - tail_rows:rows, :]
    uprev[...] = u


def _k1(tokens, mod_k, g, w_all, layer, cd, conv_w, conv_b, first, n_all, n_lat_tiles, w_lru, w_fft, w_pool):
    d = g.shape[-1]
    bsz = SUBLANES
    r = TOKEN_TILE_ROWS
    tt = r // bsz
    n_tiles = n_all // r
    n_ctx_tiles = n_tiles - n_lat_tiles
    const2 = lambda j: (0, 0)
    off_fft, off_pool = 2 * w_lru, 2 * w_lru + 2 * w_fft
    assert off_fft % w_fft == 0 and off_pool % w_pool == 0
    clamp = lambda j: (jnp.minimum(j, n_tiles - 1), 0)

    if first:
        tok_specs = [
            pl.BlockSpec((bsz, tt, d), lambda j: (0, jnp.minimum(j, n_lat_tiles - 1), 0)),
            pl.BlockSpec((r, d), lambda j: (jnp.clip(j - n_lat_tiles, 0, n_ctx_tiles - 1), 0)),
            pl.BlockSpec((tt, d), lambda j: (jnp.minimum(j, n_lat_tiles - 1), 0)),
        ]
    else:
        tok_specs = [pl.BlockSpec((r, d), clamp)]
    in_specs = tok_specs + [
        pl.BlockSpec((1, 3, bsz, d), lambda j: (jnp.where(j < n_lat_tiles, 0, 1), 0, 0, 0)),
        pl.BlockSpec((1, d), const2),
        _resident((1, d, w_lru), lambda j: (layer, 0, 0)),
        _resident((1, d, w_fft), lambda j: (layer, 0, off_fft // w_fft)),
        _resident((1, d, w_pool), lambda j: (layer, 0, off_pool // w_pool)),
        _resident((1,) + cd.shape[1:], lambda j: (layer, 0, 0)),
        pl.BlockSpec((CONV_TAPS, w_lru), const2),
        pl.BlockSpec((1, w_lru), const2),
    ]
    out_specs = [
        pl.BlockSpec((r, w_lru), lambda j: (jnp.maximum(j - 1, 0), 0)),
        pl.BlockSpec((1, tt, DFT_GROUP_ROWS, w_fft), lambda j: (jnp.minimum(j, n_tiles - 1), 0, 0, 0)),
        pl.BlockSpec((r, w_pool), clamp),
        pl.BlockSpec((r, d), clamp),
    ]
    out_shape = [
        jax.ShapeDtypeStruct((n_all, w_lru), BF16),
        jax.ShapeDtypeStruct((n_tiles, tt, DFT_GROUP_ROWS, w_fft), BF16),
        jax.ShapeDtypeStruct((n_all, w_pool), BF16),
        jax.ShapeDtypeStruct((n_all, d), BF16),
    ]
    if first:
        out_specs.append(pl.BlockSpec((r, d), clamp))
        out_shape.append(jax.ShapeDtypeStruct((n_all, d), F32))
    return pl.pallas_call(
        functools.partial(_k1_body, first, n_lat_tiles, n_tiles),
        grid=(n_tiles + 1,),
        in_specs=in_specs,
        out_specs=out_specs,
        out_shape=out_shape,
        scratch_shapes=[pltpu.VMEM((r, w_lru), F32), pltpu.VMEM(((CONV_TAPS - 2) * SUBLANES, w_lru), F32),
                        pltpu.VMEM((d, w_lru), BF16), pltpu.VMEM((d, w_fft), BF16), pltpu.VMEM((d, w_pool), BF16)],
        compiler_params=_params(),
        name="norm_uproj",
    )(*tokens, mod_k, g, w_all, w_all, w_all, cd, conv_w, conv_b.reshape(1, w_lru))


def _merge_weight_segments(w_lru, w_fft, w_pool, d):
    return [(w_lru, w_lru), (2 * w_lru + w_fft, w_fft), (2 * w_lru + 2 * w_fft + w_pool, w_pool),
            (2 * w_lru + 2 * w_fft + 2 * w_pool, 3 * d)]


def _lru_chunk(d, j, n_lat_chunks, n_ctx_chunks):
    in_ctx = j < n_ctx_chunks
    jj = j - n_ctx_chunks
    ctx_ck = n_lat_chunks + jnp.where(d == 0, j, n_ctx_chunks - 1 - j)
    lat_ck = jnp.where(d == 0, jj, n_lat_chunks - 1 - jj)
    return jnp.where(in_ctx, ctx_ck, lat_ck)


def _lru_body(steps, xh_ref, wg_ref, ba_ref, bx_ref, lam_ref, ws_ref, o_ref, wz_ref, a_sc, b_sc, h_sc, state):
    d = pl.program_id(0)
    j = pl.program_id(1)
    cb = xh_ref.shape[1]
    wz_ref[...] = ws_ref[0].astype(BF16)

    @pl.when(j == 0)
    def _():
        state[...] = jnp.zeros_like(state)

    lam = lam_ref[0]
    half_rate = (-0.5 * LRU_POWER) * (jnp.maximum(-lam, 0.0) + jnp.log1p(jnp.exp(-jnp.abs(lam))))
    half_ba = 0.5 * ba_ref[0]
    half_bx = 0.5 * bx_ref[0]
    gw = wg_ref.shape[2]
    for g in range(cb // gw):
        sl = slice(g * gw, (g + 1) * gw)
        xb = xh_ref[:, sl]
        gates = jnp.dot(xb, wg_ref[0, g], preferred_element_type=F32)
        log_a = (1.0 + jnp.tanh(gates[:, :gw] + half_ba[:, sl])) * half_rate[:, sl]
        a = jnp.exp(log_a)
        q = jnp.tanh(log_a) * (-1.0 - a * a)
        gain = jnp.where(q > 0.0, q * lax.rsqrt(q), 0.0)
        a_sc[:, sl] = a
        b_sc[:, sl] = gain * ((1.0 + jnp.tanh(gates[:, gw:] + half_bx[:, sl])) * xb.astype(F32))

    n_blocks = steps // SCAN_UNROLL
    block_rows = SCAN_UNROLL * SUBLANES

    def scan(reverse):
        def block(i, h):
            base = pl.multiple_of((n_blocks - 1 - i if reverse else i) * block_rows, block_rows)
            for k in (range(SCAN_UNROLL - 1, -1, -1) if reverse else range(SCAN_UNROLL)):
                r0 = base + k * SUBLANES
                h = a_sc[pl.ds(r0, SUBLANES), :] * h + b_sc[pl.ds(r0, SUBLANES), :]
                h_sc[pl.ds(r0, SUBLANES), :] = h
            return h

        state[...] = lax.fori_loop(0, n_blocks, block, state[...])

    @pl.when(d == 0)
    def _():
        scan(False)

    @pl.when(d == 1)
    def _():
        scan(True)

    o_ref[0] = h_sc[...].astype(BF16)


def _lru(xh, wg, ba, bx, lam, w_in, layer, segs, n_lat_rows, n_ctx_rows, steps):
    n_all, w = xh.shape
    d_model = w_in.shape[1]
    rc = steps * SUBLANES
    n_lat_chunks = n_lat_rows // rc
    n_ctx_chunks = n_ctx_rows // rc
    n_chunks = n_lat_chunks + n_ctx_chunks
    gw = wg.shape[2]
    ck = functools.partial(_lru_chunk, n_lat_chunks=n_lat_chunks, n_ctx_chunks=n_ctx_chunks)

    total = sum(width for _, width in segs)
    slab_w = next(c for c in (LANES, 2 * LANES, 4 * LANES, 8 * LANES) if total // c <= 2 * n_chunks)
    assert all(off % slab_w == 0 and width % slab_w == 0 for off, width in segs)
    n_slabs = total // slab_w
    bounds = np.cumsum([0] + [width // slab_w for _, width in segs])
    slab = lambda d, j: jnp.minimum(d * n_chunks + j, n_slabs - 1)

    def slab_source(d, j):
        s = slab(d, j)
        blk = segs[-1][0] // slab_w + (s - int(bounds[-2]))
        for k in range(len(segs) - 2, -1, -1):
            blk = jnp.where(s < int(bounds[k + 1]), segs[k][0] // slab_w + (s - int(bounds[k])), blk)
        return blk

    return pl.pallas_call(
        functools.partial(_lru_body, steps),
        grid=(2, n_chunks),
        in_specs=[
            pl.BlockSpec((rc, w), lambda d, j: (ck(d, j), 0)),
            pl.BlockSpec((1, w // gw, gw, 2 * gw), lambda d, j: (d, 0, 0, 0)),
            pl.BlockSpec((1, 1, w), lambda d, j: (d, 0, 0)),
            pl.BlockSpec((1, 1, w), lambda d, j: (d, 0, 0)),
            pl.BlockSpec((1, 1, w), lambda d, j: (d, 0, 0)),
            pl.BlockSpec((1, d_model, slab_w), lambda d, j: (layer, 0, slab_source(d, j))),
        ],
        out_specs=[pl.BlockSpec((1, rc, w), lambda d, j: (d, ck(d, j), 0)),
                   pl.BlockSpec((d_model, slab_w), lambda d, j: (0, slab(d, j)))],
        out_shape=[jax.ShapeDtypeStruct((2, n_all, w), BF16), jax.ShapeDtypeStruct((d_model, total), BF16)],
        scratch_shapes=[
            pltpu.VMEM((rc, w), F32),
            pltpu.VMEM((rc, w), F32),
            pltpu.VMEM((rc, w), F32),
            pltpu.VMEM((SUBLANES, w), F32),
        ],
        compiler_params=_params(2),
        name="rglru_scan",
    )(xh, wg, ba.reshape(2, 1, w), bx.reshape(2, 1, w), lam.reshape(2, 1, w), w_in)


def _dft_constants(n2):
    n1 = DFT_RADIX
    n = n1 * n2
    eye = np.eye(SUBLANES)
    k2 = np.arange(n2)
    ang1 = 2.0 * np.pi * ((k2[:, None] * k2[None, :]) % n2) / n2
    c1, s1 = np.cos(ang1) / np.sqrt(n2), np.sin(ang1) / np.sqrt(n2)
    base1 = np.stack([np.stack([c1, s1], axis=-1), np.stack([-s1, c1], axis=-1)], axis=1)
    m1 = jnp.asarray(np.kron(base1.reshape(2 * n2, 2 * n2), eye), F32)
    k1 = np.arange(n1)
    ang2 = 2.0 * np.pi * ((k1[:, None] * k1[None, :]) % n1) / n1
    base2 = np.stack([np.cos(ang2), np.sin(ang2)], axis=-1) / np.sqrt(n1)
    m2 = jnp.asarray(np.kron(base2.reshape(n1, 2 * n1), eye), F32)
    angt = 2.0 * np.pi * ((k1[:, None] * k2[None, :]) % n) / n
    shape = (n1, n2, SUBLANES, LANES)

    def table(vals):
        return jnp.asarray(np.ascontiguousarray(np.broadcast_to(vals[:, :, None, None], shape)), F32)

    return m1.astype(BF16), m2.astype(BF16), table(np.cos(angt)), table(np.sin(angt))


def _dft_body(steps1, v_ref, m1_ref, twc_ref, tws_ref, m2_ref, y_ref, zs):
    s = pl.program_id(1)
    n2, a_blk, grp, w = v_ref.shape
    n1 = zs.shape[1]
    pair = grp // SUBLANES

    @pl.when(s < steps1)
    def _():
        reps = w // LANES
        for i in range(a_blk):
            xs = v_ref[:, i].reshape(n2 * grp, w)
            r = jnp.dot(m1_ref[...], xs, preferred_element_type=F32).reshape(n2, 2, SUBLANES, w)
            rr, ri = r[:, 0], r[:, 1]
            cw = jnp.concatenate([twc_ref[i]] * reps, axis=-1)
            sw = jnp.concatenate([tws_ref[i]] * reps, axis=-1)
            z = jnp.stack([rr * cw + ri * sw, ri * cw - rr * sw], axis=1)
            zs[:, s * a_blk + i] = z.reshape(n2, grp, w).astype(BF16)

    @pl.when(s >= steps1)
    def _():
        n_out = y_ref.shape[1] * pair
        c0 = (s - steps1) * n_out
        ys = [jnp.dot(m2_ref[...], zs[c0 + i].reshape(n1 * grp, w), preferred_element_type=F32)
              .reshape(n1, SUBLANES, w) for i in range(n_out)]
        y_ref[...] = jnp.stack(ys, axis=1).reshape(n1, n_out // pair, grp, w).astype(BF16)


def _position_dft(v4, n2, block0):
    m1, m2, twc, tws = _dft_constants(n2)
    n1 = DFT_RADIX
    grp, w = v4.shape[2], v4.shape[3]
    wh = min(w, DFT_LANE_BLOCK)
    a_blk = min(n1, max(1, DFT_STAGE1_ROWS // (n2 * grp)))
    steps1 = n1 // a_blk
    pair = grp // SUBLANES
    groups2 = min(DFT_STAGE2_GROUPS, n2 // pair)
    stage1_step = lambda s: jnp.minimum(s, steps1 - 1)
    return pl.pallas_call(
        functools.partial(_dft_body, steps1),
        grid=(w // wh, steps1 + n2 // (pair * groups2)),
        in_specs=[
            pl.BlockSpec((n2, a_blk, grp, wh), lambda c, s: (block0, stage1_step(s), 0, c)),
            _resident(m1.shape, lambda c, s: (0, 0)),
            pl.BlockSpec((a_blk, n2, SUBLANES, LANES), lambda c, s: (stage1_step(s), 0, 0, 0)),
            pl.BlockSpec((a_blk, n2, SUBLANES, LANES), lambda c, s: (stage1_step(s), 0, 0, 0)),
            _resident(m2.shape, lambda c, s: (0, 0)),
        ],
        out_specs=pl.BlockSpec((n1, groups2, grp, wh), lambda c, s: (0, jnp.maximum(s - steps1, 0), 0, c)),
        out_shape=jax.ShapeDtypeStruct((n1, n2 // pair, grp, w), BF16),
        scratch_shapes=[pltpu.VMEM((n2, n1, grp, wh), BF16)],
        compiler_params=_params(2),
        name="position_dft",
    )(v4, m1, twc, tws, m2)


def _pool_minus_token(ue, rows, t0, t_seq):
    halo = POOL_HALO_ROWS
    tvec = t0 + lax.broadcasted_iota(jnp.int32, (rows, LANES), 0) // SUBLANES
    parts = []
    for g, win in enumerate(POOL_WINDOW_SIZES):
        col = ue[:, g * LANES:(g + 1) * LANES]
        n = col.shape[0]
        acc = col[0:n - SUBLANES] + col[SUBLANES:n]
        e0 = SUBLANES
        span = 1
        while 2 * span < win:
            sh = span * SUBLANES
            n = acc.shape[0]
            acc = acc[0:n - 2 * sh] + acc[2 * sh:n]
            e0 += sh
            span *= 2
        wsum = acc[halo - e0:halo - e0 + rows]
        half = win // 2
        cnt = jnp.minimum(tvec + half, t_seq) - jnp.maximum(tvec - half, 0)
        parts.append(wsum / cnt.astype(F32) - col[halo:halo + rows])
    return jnp.concatenate(parts, axis=1)


def _k3_body(cfg, *refs):
    last, n_lat_tiles, n_tiles, t_lat, t_ctx = cfg
    refs = list(refs)
    x_ref, h_ref, mod_ref, hf_ref, hb_ref, yl_ref = refs[:6]
    refs = refs[6:]
    yc_ref = None if last else refs.pop(0)
    (up_ref, upp_ref, upn_ref, wzl_ref, wzf_ref, wzp_ref, wg0_ref, wg1_ref, wg2_ref, pa_ref, pb_ref, pc_ref,
     wo_ref, pw_ref, ps_ref, fg_ref, o_ref) = refs
    j = pl.program_id(0)
    is_lat = j < n_lat_tiles
    rows, d = x_ref.shape
    bsz = SUBLANES
    tt = rows // bsz
    h = h_ref[...]

    seq_first, seq_last = _seq_edges(j, n_lat_tiles, n_tiles)
    ue = jnp.concatenate([
        upp_ref[...].astype(F32) * jnp.where(seq_first, 0.0, 1.0).astype(F32),
        up_ref[...].astype(F32),
        upn_ref[...].astype(F32) * jnp.where(seq_last, 0.0, 1.0).astype(F32)], axis=0)
    t0 = jnp.where(is_lat, j, j - n_lat_tiles) * tt
    t_seq = jnp.where(is_lat, t_lat, t_ctx)
    p = _pool_minus_token(ue, rows, t0, t_seq).astype(BF16)

    z_lru = jnp.dot(h, wzl_ref[...], preferred_element_type=F32)
    y_lru = hf_ref[0].astype(F32) + hb_ref[0].astype(F32)
    ya = jnp.dot((y_lru * _silu(z_lru)).astype(BF16), pa_ref[...], preferred_element_type=F32)

    z_fft = jnp.dot(h, wzf_ref[...], preferred_element_type=F32)
    w = yl_ref.shape[-1]
    y_fft = yl_ref[...].reshape(rows, w)
    if not last:
        y_fft = jnp.where(is_lat, y_fft, yc_ref[...].reshape(rows, w))
    yb = jnp.dot((y_fft.astype(F32) * _silu(z_fft)).astype(BF16), pb_ref[...], preferred_element_type=F32)

    z_pool = jnp.dot(h, wzp_ref[...], preferred_element_type=F32)
    y_pool = jnp.dot(p, pw_ref[...], preferred_element_type=F32) * ps_ref[...]
    yc = jnp.dot((y_pool * _silu(z_pool)).astype(BF16), pc_ref[...], preferred_element_type=F32)

    m = _sigmoid(jnp.dot(h, wg0_ref[...], preferred_element_type=F32)) * ya
    m = m + _sigmoid(jnp.dot(h, wg1_ref[...], preferred_element_type=F32)) * yb
    m = m + _sigmoid(jnp.dot(h, wg2_ref[...], preferred_element_type=F32)) * yc
    out = jnp.dot(m.astype(BF16), wo_ref[...], preferred_element_type=F32)
    xn = x_ref[...].reshape(tt, bsz, d) + mod_ref[0, 2] * out.reshape(tt, bsz, d)
    if last:
        ms = jnp.mean(xn * xn, axis=-1, keepdims=True)
        xn = xn * lax.rsqrt(ms + NORM_EPS) * fg_ref[...]
        o_ref[...] = jnp.transpose(xn, (1, 0, 2))
    else:
        o_ref[...] = xn.reshape(rows, d)


def _k3(x_rows, h, mod_k, h_lru, y_lat, y_ctx, u_pool, wz, pa, pb, pc, wo, pw, ps, fg, last,
        n_lat_tiles, t_lat, t_ctx):
    n_all, d = h.shape
    bsz = SUBLANES
    r = TOKEN_TILE_ROWS
    tt = r // bsz
    n_tiles = n_all // r
    grid_tiles = n_lat_tiles if last else n_tiles
    w_lru = h_lru.shape[2]
    w_fft = y_lat.shape[-1]
    w_pool = u_pool.shape[1]
    hr = r // POOL_HALO_ROWS
    n_halo = n_all // POOL_HALO_ROWS
    cfg = (last, n_lat_tiles, n_tiles, t_lat, t_ctx)
    const2 = lambda j: (0, 0)
    off_zl, off_zf, off_zp, off_g = 0, w_lru, w_lru + w_fft, w_lru + w_fft + w_pool
    assert [width for _, width in _merge_weight_segments(w_lru, w_fft, w_pool, d)] == [w_lru, w_fft, w_pool, 3 * d]
    assert off_zf % w_fft == 0 and off_zp % w_pool == 0 and off_g % d == 0

    def dft_spec(y4, first_tile):
        n2 = y4.shape[1] * y4.shape[2] // bsz
        assert tt % n2 == 0
        blk = (tt // n2,) + y4.shape[1:]
        n_blk = y4.shape[0] // blk[0]
        return pl.BlockSpec(blk, lambda j: (jnp.clip(j - first_tile, 0, n_blk - 1), 0, 0, 0))

    row_spec = lambda width: pl.BlockSpec((r, width), lambda j: (j, 0))
    pair_spec = lambda k: pl.BlockSpec((1, r, w_lru), lambda j: (k, j, 0))
    in_specs = [
        row_spec(d), row_spec(d),
        pl.BlockSpec((1, 3, bsz, d), lambda j: (jnp.where(j < n_lat_tiles, 0, 1), 0, 0, 0)),
        pair_spec(0), pair_spec(1),
        dft_spec(y_lat, 0),
    ]
    args = [x_rows, h, mod_k, h_lru, h_lru, y_lat]
    if not last:
        in_specs.append(dft_spec(y_ctx, n_lat_tiles))
        args.append(y_ctx)
    in_specs += [
        row_spec(w_pool),
        pl.BlockSpec((POOL_HALO_ROWS, w_pool), lambda j: (jnp.maximum(j * hr - 1, 0), 0)),
        pl.BlockSpec((POOL_HALO_ROWS, w_pool), lambda j: (jnp.minimum((j + 1) * hr, n_halo - 1), 0)),
        _resident((d, w_lru), lambda j: (0, off_zl // w_lru)),
        _resident((d, w_fft), lambda j: (0, off_zf // w_fft)),
        _resident((d, w_pool), lambda j: (0, off_zp // w_pool)),
        _resident((d, d), lambda j: (0, off_g // d)),
        _resident((d, d), lambda j: (0, off_g // d + 1)),
        _resident((d, d), lambda j: (0, off_g // d + 2)),
    ]
    args += [u_pool, u_pool, u_pool] + [wz] * 6
    for wgt in (pa, pb, pc, wo, pw, ps, fg):
        in_specs.append(_resident(wgt.shape, const2))
        args.append(wgt)
    if last:
        out_spec = pl.BlockSpec((bsz, tt, d), lambda j: (0, j, 0))
        out_shape = jax.ShapeDtypeStruct((bsz, t_lat, d), F32)
    else:
        out_spec = row_spec(d)
        out_shape = jax.ShapeDtypeStruct((n_all, d), F32)
    return pl.pallas_call(
        functools.partial(_k3_body, cfg),
        grid=(grid_tiles,),
        in_specs=in_specs,
        out_specs=out_spec,
        out_shape=out_shape,
        compiler_params=_params(),
        name="merge_residual",
    )(*args)


def _position_code(n_tokens, d):
    rows = n_tokens // GRID_WIDTH
    quarter = d // 4
    omega = 1.0 / (POSITION_BASE ** (jnp.arange(quarter, dtype=F32) / quarter))

    def emb(n):
        ang = jnp.arange(n).astype(F32)[:, None] * omega[None, :]
        return jnp.concatenate([jnp.sin(ang), jnp.cos(ang)], axis=-1)

    row_code = jnp.repeat(emb(rows), GRID_WIDTH, axis=0)
    col_code = jnp.tile(emb(GRID_WIDTH), (rows, 1))
    return jnp.concatenate([row_code, col_code], axis=-1).astype(F32)


def _block_diag(w):
    g, n, _ = w.shape
    eye = jnp.eye(g, dtype=w.dtype)
    return jnp.einsum("gij,gk->gikj", w, eye).reshape(g * n, g * n)


def _channel_dft_matrix(n):
    k = np.arange(n)
    ang = 2.0 * np.pi * ((k[:, None] * k[None, :]) % n) / n
    return jnp.asarray(np.concatenate([np.cos(ang), -np.sin(ang)], axis=1) / np.sqrt(n), F32)


def _fold_body(cs_ref, fw_ref, o_ref):
    _, groups, n, _ = fw_ref.shape
    o_ref[...] = jnp.zeros_like(o_ref)
    for g in range(groups):
        for part in range(2):
            blk = jnp.dot(cs_ref[:, part * n:(part + 1) * n], fw_ref[0, g], preferred_element_type=F32,
                          precision=lax.Precision.HIGHEST)
            col = part * groups * n + g * n
            o_ref[0, g * n:(g + 1) * n, col:col + n] = blk.astype(BF16)


def _fold_channel_map(cs, fw):
    depth, groups, n, _ = fw.shape
    w = groups * n
    return pl.pallas_call(
        _fold_body,
        grid=(depth,),
        in_specs=[pl.BlockSpec(cs.shape, lambda l: (0, 0)), pl.BlockSpec((1, groups, n, n), lambda l: (l, 0, 0, 0))],
        out_specs=pl.BlockSpec((1, w, 2 * w), lambda l: (l, 0, 0)),
        out_shape=jax.ShapeDtypeStruct((depth, w, 2 * w), BF16),
        compiler_params=_params(),
        name="fold_fft_map",
    )(cs, fw)


def kernel(x, c, ctx, c_ctx, norm_g, ada_w, ada_b, w_in, conv_w, conv_b, lru_wa, lru_ba, lru_wx, lru_bx, lru_lam,
           fft_w, pool_w, pool_scale, proj_a, proj_b, proj_c, w_out, final_g):
    bsz, t_lat, d = x.shape
    t_ctx = ctx.shape[1]
    depth = w_in.shape[0]
    w_lru = conv_w.shape[-1]
    fft_groups, fft_dim = fft_w.shape[1], fft_w.shape[2]
    w_fft = fft_groups * fft_dim
    w_pool = pool_w.shape[1] * pool_w.shape[2]
    heads, head_dim = lru_wa.shape[2], lru_wa.shape[3]
    assert bsz == SUBLANES and fft_dim == LANES and pool_w.shape[2] == LANES and conv_w.shape[1] == CONV_TAPS
    assert t_lat % DFT_RADIX == 0 and t_ctx % DFT_RADIX == 0 and (t_lat // DFT_RADIX) % (t_ctx // DFT_RADIX) == 0
    n_lat_rows, n_ctx_rows = t_lat * bsz, t_ctx * bsz
    n_all = n_lat_rows + n_ctx_rows
    n_lat_tiles = n_lat_rows // TOKEN_TILE_ROWS
    lru_steps = min(LRU_CHUNK_STEPS, t_ctx)
    heads_per_group = LRU_GATE_WIDTH // head_dim

    pos = _position_code(t_lat, d)
    cc =jnp.concatenate([c, jnp.broadcast_to(c_ctx[None, :], (bsz, d))], axis=0)
    mod = _modulation(cc, ada_w, ada_b)
    cdw = _fold_channel_map(_channel_dft_matrix(fft_dim), fft_w)
    n2_lat, n2_ctx = t_lat // DFT_RADIX, t_ctx // DFT_RADIX

    def gate_w(w):
        wd = w.reshape(2 * heads // heads_per_group, heads_per_group, head_dim, head_dim)
        return jax.vmap(_block_diag)(wd).reshape(2, heads // heads_per_group, LRU_GATE_WIDTH, LRU_GATE_WIDTH)

    tokens = (x, _batch_major_to_rows(ctx), pos)
    out = None
    for l in range(depth):
        first, last = l == 0, l == depth - 1
        mod_k = mod[l].reshape(2, bsz, 3, d).transpose(0, 2, 1, 3)
        g = norm_g[l].reshape(1, d)
        k1_out = _k1(tokens, mod_k, g, w_in, l, cdw, conv_w[l], conv_b[l], first, n_all, n_lat_tiles,
                     w_lru, w_fft, w_pool)
        xh, v4, u_pool, h = k1_out[:4]
        x_rows = k1_out[4] if first else tokens[0]

        wg = jnp.concatenate([gate_w(lru_wa[l]), gate_w(lru_wx[l])], axis=-1).astype(BF16)
        h_lru, wz = _lru(xh, wg, lru_ba[l], lru_bx[l], lru_lam[l], w_in, l,
                         _merge_weight_segments(w_lru, w_fft, w_pool, d), n_lat_rows, n_ctx_rows, lru_steps)

        y_lat = _position_dft(v4, n2_lat, 0)
        y_ctx = None if last else _position_dft(v4, n2_ctx, n2_lat // n2_ctx)

        out = _k3(x_rows, h, mod_k, h_lru, y_lat, y_ctx, u_pool, wz,
                  proj_a[l].astype(BF16), proj_b[l].astype(BF16), proj_c[l].astype(BF16), w_out[l].astype(BF16),
                  _block_diag(pool_w[l]).astype(BF16),
                  pool_scale[l].reshape(1, w_pool), final_g.reshape(1, d), last, n_lat_tiles, t_lat, t_ctx)
        tokens = (out,)
    return out
```

```python
import functools

import numpy as np
import jax
import jax.numpy as jnp
from jax import lax
from jax.experimental import pallas as pl
from jax.experimental.pallas import tpu as pltpu

F32 = jnp.float32
BF16 = jnp.bfloat16

GRID_WIDTH = 64
LRU_POWER = 8.0
NORM_EPS = 1e-6
POSITION_BASE = 10000.0
POOL_WINDOW_SIZES = (2, 4, 8, 16)
CONV_TAPS = 4
SUBLANES = 8
LANES = 128
DFT_RADIX = 64
DFT_GROUP_ROWS = 2 * SUBLANES
DFT_STAGE1_ROWS = 4096
DFT_STAGE2_GROUPS = 2
DFT_LANE_BLOCK = 256
POOL_HALO_ROWS = 64
TOKEN_TILE_ROWS = DFT_RADIX * SUBLANES
LRU_CHUNK_STEPS = 256
LRU_GATE_WIDTH = 256
SCAN_UNROLL = 8
VMEM_LIMIT_BYTES = 56 * 1024 * 1024


def _half_tanh_sigmoid(x_half):
    return 0.5 * (1.0 + jnp.tanh(x_half))


def _sigmoid(x):
    return _half_tanh_sigmoid(0.5 * x)


def _silu(x):
    return x * _sigmoid(x)


def _params(n_axes=1):
    return pltpu.CompilerParams(dimension_semantics=("arbitrary",) * n_axes, vmem_limit_bytes=VMEM_LIMIT_BYTES)


def _resident(shape, index_map):
    return pl.BlockSpec(shape, index_map, pipeline_mode=pl.Buffered(1))


def _mod_body(cc_ref, w_ref, b_ref, o_ref):
    s = _silu(cc_ref[...]).astype(BF16)
    o_ref[0] = jnp.dot(s, w_ref[0].astype(BF16), preferred_element_type=F32) + b_ref[0]


def _modulation(cc, ada_w, ada_b):
    depth, d, n = ada_w.shape
    tn = 1024
    return pl.pallas_call(
        _mod_body,
        grid=(depth, n // tn),
        in_specs=[
            pl.BlockSpec((2 * SUBLANES, d), lambda l, i: (0, 0)),
            pl.BlockSpec((1, d, tn), lambda l, i: (l, 0, i)),
            pl.BlockSpec((1, 1, tn), lambda l, i: (l, 0, i)),
        ],
        out_specs=pl.BlockSpec((1, 2 * SUBLANES, tn), lambda l, i: (l, 0, i)),
        out_shape=jax.ShapeDtypeStruct((depth, 2 * SUBLANES, n), F32),
        compiler_params=_params(2),
        name="adaln_mod",
    )(cc, ada_w, ada_b.reshape(depth, 1, n))


def _rows_body(x_ref, o_ref):
    bsz, tt, d = x_ref.shape
    o_ref[...] = jnp.transpose(x_ref[...], (1, 0, 2)).reshape(tt * bsz, d)


def _batch_major_to_rows(x):
    bsz, t, d = x.shape
    tt = TOKEN_TILE_ROWS // bsz
    return pl.pallas_call(
        _rows_body,
        grid=(t // tt,),
        in_specs=[pl.BlockSpec((bsz, tt, d), lambda j: (0, j, 0))],
        out_specs=pl.BlockSpec((tt * bsz, d), lambda j: (j, 0)),
        out_shape=jax.ShapeDtypeStruct((t * bsz, d), x.dtype),
        compiler_params=_params(),
        name="ctx_rows",
    )(x)


def _seq_edges(p, n_lat_tiles, n_tiles):
    is_first = jnp.logical_or(p == 0, p == n_lat_tiles)
    is_last = jnp.logical_or(p == n_lat_tiles - 1, p == n_tiles - 1)
    return is_first, is_last


def _k1_body(first, n_lat_tiles, n_tiles, *refs):
    n_tok = 3 if first else 1
    tok_refs = refs[:n_tok]
    refs = list(refs[n_tok:])
    mod_ref, g_ref, wl_ref, wf_ref, wp_ref, cd_ref, cw_ref, cb_ref, xh_ref, v_ref, upool_ref, h_ref = refs[:12]
    refs = refs[12:]
    x0_ref = refs.pop(0) if first else None
    uprev, tail, wl_s, wf_s, wp_s = refs
    j = pl.program_id(0)
    rows, d = h_ref.shape
    bsz = SUBLANES
    tt = rows // bsz
    tail_rows = tail.shape[0]

    @pl.when(j == 0)
    def _():
        uprev[...] = jnp.zeros_like(uprev)
        tail[...] = jnp.zeros_like(tail)
        wl_s[...] = wl_ref[0].astype(BF16)
        wf_s[...] = wf_ref[0].astype(BF16)
        wp_s[...] = wp_ref[0].astype(BF16)


    if first:
        x_ref, ctx_ref, pos_ref = tok_refs
        x3 = jnp.where(j < n_lat_tiles, jnp.transpose(x_ref[...] + pos_ref[...][None], (1, 0, 2)),
                       ctx_ref[...].reshape(tt, bsz, d))
        x0_ref[...] = x3.reshape(rows, d)
    else:
        x3 = tok_refs[0][...].reshape(tt, bsz, d)
    ms = jnp.mean(x3 * x3, axis=-1, keepdims=True)
    gain = g_ref[...] * (1.0 + mod_ref[0, 1])
    h = (x3 * lax.rsqrt(ms + NORM_EPS) * gain + mod_ref[0, 0]).reshape(rows, d).astype(BF16)
    h_ref[...] = h
    u = jnp.dot(h, wl_s[...], preferred_element_type=F32)
    ufft = jnp.dot(h, wf_s[...], preferred_element_type=F32).astype(BF16)
    v = jnp.dot(ufft, cd_ref[0], preferred_element_type=F32)
    w = v.shape[1] // 2
    v4 = jnp.stack([v[:, :w].reshape(tt, bsz, w), v[:, w:].reshape(tt, bsz, w)], axis=1)
    v_ref[...] = v4.reshape(1, tt, DFT_GROUP_ROWS, w).astype(BF16)
    upool_ref[...] = jnp.dot(h, wp_s[...], preferred_element_type=F32).astype(BF16)

    p_first, p_last = _seq_edges(j - 1, n_lat_tiles, n_tiles)
    up = uprev[...]
    before = jnp.where(p_first, 0.0, tail[...])
    after = jnp.where(p_last, 0.0, u[0:SUBLANES, :])
    ext = jnp.concatenate([before, up, after], axis=0)
    cwh = 0.5 * cw_ref[...]
    xh = 0.5 * cb_ref[...] + cwh[0:1] * ext[0:rows]
    for k in range(1, CONV_TAPS):
        xh = xh + cwh[k:k + 1] * ext[k * SUBLANES:k * SUBLANES + rows]
    xh_ref[...] = xh.astype(BF16)

    tail[...] = up[rows - tail_rows:rows, :]
    uprev[...] = u


def _k1(tokens, mod_k, g, w_all, layer, cd, conv_w, conv_b, first, n_all, n_lat_tiles, w_lru, w_fft, w_pool):
    d = g.shape[-1]
    bsz = SUBLANES
    r = TOKEN_TILE_ROWS
    tt = r // bsz
    n_tiles = n_all // r
    n_ctx_tiles = n_tiles - n_lat_tiles
    const2 = lambda j: (0, 0)
    off_fft, off_pool = 2 * w_lru, 2 * w_lru + 2 * w_fft
    assert off_fft % w_fft == 0 and off_pool % w_pool == 0
    clamp = lambda j: (jnp.minimum(j, n_tiles - 1), 0)

    if first:
        tok_specs = [
            pl.BlockSpec((bsz, tt, d), lambda j: (0, jnp.minimum(j, n_lat_tiles - 1), 0)),
            pl.BlockSpec((r, d), lambda j: (jnp.clip(j - n_lat_tiles, 0, n_ctx_tiles - 1), 0)),
            pl.BlockSpec((tt, d), lambda j: (jnp.minimum(j, n_lat_tiles - 1), 0)),
        ]
    else:
        tok_specs = [pl.BlockSpec((r, d), clamp)]
    in_specs = tok_specs + [
        pl.BlockSpec((1, 3, bsz, d), lambda j: (jnp.where(j < n_lat_tiles, 0, 1), 0, 0, 0)),
        pl.BlockSpec((1, d), const2),
        _resident((1, d, w_lru), lambda j: (layer, 0, 0)),
        _resident((1, d, w_fft), lambda j: (layer, 0, off_fft // w_fft)),
        _resident((1, d, w_pool), lambda j: (layer, 0, off_pool // w_pool)),
        _resident((1,) + cd.shape[1:], lambda j: (layer, 0, 0)),
        pl.BlockSpec((CONV_TAPS, w_lru), const2),
        pl.BlockSpec((1, w_lru), const2),
    ]
    out_specs = [
        pl.BlockSpec((r, w_lru), lambda j: (jnp.maximum(j - 1, 0), 0)),
        pl.BlockSpec((1, tt, DFT_GROUP_ROWS, w_fft), lambda j: (jnp.minimum(j, n_tiles - 1), 0, 0, 0)),
        pl.BlockSpec((r, w_pool), clamp),
        pl.BlockSpec((r, d), clamp),
    ]
    out_shape = [
        jax.ShapeDtypeStruct((n_all, w_lru), BF16),
        jax.ShapeDtypeStruct((n_tiles, tt, DFT_GROUP_ROWS, w_fft), BF16),
        jax.ShapeDtypeStruct((n_all, w_pool), BF16),
        jax.ShapeDtypeStruct((n_all, d), BF16),
    ]
    if first:
        out_specs.append(pl.BlockSpec((r, d), clamp))
        out_shape.append(jax.ShapeDtypeStruct((n_all, d), F32))
    return pl.pallas_call(
        functools.partial(_k1_body, first, n_lat_tiles, n_tiles),
        grid=(n_tiles + 1,),
        in_specs=in_specs,
        out_specs=out_specs,
        out_shape=out_shape,
        scratch_shapes=[pltpu.VMEM((r, w_lru), F32), pltpu.VMEM(((CONV_TAPS - 2) * SUBLANES, w_lru), F32),
                        pltpu.VMEM((d, w_lru), BF16), pltpu.VMEM((d, w_fft), BF16), pltpu.VMEM((d, w_pool), BF16)],
        compiler_params=_params(),
        name="norm_uproj",
    )(*tokens, mod_k, g, w_all, w_all, w_all, cd, conv_w, conv_b.reshape(1, w_lru))


def _merge_weight_segments(w_lru, w_fft, w_pool, d):
    return [(2 * w_lru + 2 * w_fft + 2 * w_pool, 3 * d), (w_lru, w_lru), (2 * w_lru + w_fft, w_fft),
            (2 * w_lru + 2 * w_fft + w_pool, w_pool)]


def _lru_chunk(d, j, n_lat_chunks, n_ctx_chunks):
    in_ctx = j < n_ctx_chunks
    jj = j - n_ctx_chunks
    ctx_ck = n_lat_chunks + jnp.where(d == 0, j, n_ctx_chunks - 1 - j)
    lat_ck = jnp.where(d == 0, jj, n_lat_chunks - 1 - jj)
    return jnp.where(in_ctx, ctx_ck, lat_ck)


def _lru_body(steps, xh_ref, wg_ref, ba_ref, bx_ref, lam_ref, ws_ref, o_ref, wz_ref, a_sc, b_sc, h_sc, state):
    d = pl.program_id(0)
    j = pl.program_id(1)
    cb = xh_ref.shape[1]
    wz_ref[...] = ws_ref[0].astype(BF16)

    @pl.when(j == 0)
    def _():
        state[...] = jnp.zeros_like(state)

    lam = lam_ref[0]
    half_rate = (-0.5 * LRU_POWER) * (jnp.maximum(-lam, 0.0) + jnp.log1p(jnp.exp(-jnp.abs(lam))))
    half_ba = 0.5 * ba_ref[0]
    half_bx = 0.5 * bx_ref[0]
    gw = wg_ref.shape[2]
    for g in range(cb // gw):
        sl = slice(g * gw, (g + 1) * gw)
        xb = xh_ref[:, sl]
        gates = jnp.dot(xb, wg_ref[0, g], preferred_element_type=F32)
        log_a = (1.0 + jnp.tanh(gates[:, :gw] + half_ba[:, sl])) * half_rate[:, sl]
        a = jnp.exp(log_a)
        q = jnp.tanh(log_a) * (-1.0 - a * a)
        gain = jnp.where(q > 0.0, q * lax.rsqrt(q), 0.0)
        a_sc[:, sl] = a
        b_sc[:, sl] = gain * ((1.0 + jnp.tanh(gates[:, gw:] + half_bx[:, sl])) * xb.astype(F32))

    n_blocks = steps // SCAN_UNROLL
    block_rows = SCAN_UNROLL * SUBLANES

    def scan(reverse):
        def block(i, h):
            base = pl.multiple_of((n_blocks - 1 - i if reverse else i) * block_rows, block_rows)
            for k in (range(SCAN_UNROLL - 1, -1, -1) if reverse else range(SCAN_UNROLL)):
                r0 = base + k * SUBLANES
                h = a_sc[pl.ds(r0, SUBLANES), :] * h + b_sc[pl.ds(r0, SUBLANES), :]
                h_sc[pl.ds(r0, SUBLANES), :] = h
            return h

        state[...] = lax.fori_loop(0, n_blocks, block, state[...])

    @pl.when(d == 0)
    def _():
        scan(False)

    @pl.when(d == 1)
    def _():
        scan(True)

    o_ref[0] = h_sc[...].astype(BF16)


def _lru(xh, wg, ba, bx, lam, w_in, layer, segs, n_lat_rows, n_ctx_rows, steps):
    n_all, w = xh.shape
    d_model = w_in.shape[1]
    rc = steps * SUBLANES
    n_lat_chunks = n_lat_rows // rc
    n_ctx_chunks = n_ctx_rows // rc
    n_chunks = n_lat_chunks + n_ctx_chunks
    gw = wg.shape[2]
    ck = functools.partial(_lru_chunk, n_lat_chunks=n_lat_chunks, n_ctx_chunks=n_ctx_chunks)

    total = sum(width for _, width in segs)
    slab_w = next(c for c in (LANES, 2 * LANES, 4 * LANES, 8 * LANES) if total // c <= 2 * n_chunks)
    assert all(off % slab_w == 0 and width % slab_w == 0 for off, width in segs)
    n_slabs = total // slab_w
    bounds = np.cumsum([0] + [width // slab_w for _, width in segs])
    slab = lambda d, j: jnp.minimum(d * n_chunks + j, n_slabs - 1)

    def slab_source(d, j):
        s = slab(d, j)
        blk = segs[-1][0] // slab_w + (s - int(bounds[-2]))
        for k in range(len(segs) - 2, -1, -1):
            blk = jnp.where(s < int(bounds[k + 1]), segs[k][0] // slab_w + (s - int(bounds[k])), blk)
        return blk

    return pl.pallas_call(
        functools.partial(_lru_body, steps),
        grid=(2, n_chunks),
        in_specs=[
            pl.BlockSpec((rc, w), lambda d, j: (ck(d, j), 0)),
            pl.BlockSpec((1, w // gw, gw, 2 * gw), lambda d, j: (d, 0, 0, 0)),
            pl.BlockSpec((1, 1, w), lambda d, j: (d, 0, 0)),
            pl.BlockSpec((1, 1, w), lambda d, j: (d, 0, 0)),
            pl.BlockSpec((1, 1, w), lambda d, j: (d, 0, 0)),
            pl.BlockSpec((1, d_model, slab_w), lambda d, j: (layer, 0, slab_source(d, j))),
        ],
        out_specs=[pl.BlockSpec((1, rc, w), lambda d, j: (d, ck(d, j), 0)),
                   pl.BlockSpec((d_model, slab_w), lambda d, j: (0, slab(d, j)))],
        out_shape=[jax.ShapeDtypeStruct((2, n_all, w), BF16), jax.ShapeDtypeStruct((d_model, total), BF16)],
        scratch_shapes=[
            pltpu.VMEM((rc, w), F32),
            pltpu.VMEM((rc, w), F32),
            pltpu.VMEM((rc, w), F32),
            pltpu.VMEM((SUBLANES, w), F32),
        ],
        compiler_params=_params(2),
        name="rglru_scan",
    )(xh, wg, ba.reshape(2, 1, w), bx.reshape(2, 1, w), lam.reshape(2, 1, w), w_in)


def _dft_constants(n2):
    n1 = DFT_RADIX
    n = n1 * n2
    eye = np.eye(SUBLANES)
    k2 = np.arange(n2)
    ang1 = 2.0 * np.pi * ((k2[:, None] * k2[None, :]) % n2) / n2
    c1, s1 = np.cos(ang1) / np.sqrt(n2), np.sin(ang1) / np.sqrt(n2)
    base1 = np.stack([np.stack([c1, s1], axis=-1), np.stack([-s1, c1], axis=-1)], axis=1)
    m1 = jnp.asarray(np.kron(base1.reshape(2 * n2, 2 * n2), eye), F32)
    k1 = np.arange(n1)
    ang2 = 2.0 * np.pi * ((k1[:, None] * k1[None, :]) % n1) / n1
    base2 = np.stack([np.cos(ang2), np.sin(ang2)], axis=-1) / np.sqrt(n1)
    m2 = jnp.asarray(np.kron(base2.reshape(n1, 2 * n1), eye), F32)
    angt = 2.0 * np.pi * ((k1[:, None] * k2[None, :]) % n) / n
    shape = (n1, n2, SUBLANES, LANES)

    def table(vals):
        return jnp.asarray(np.ascontiguousarray(np.broadcast_to(vals[:, :, None, None], shape)), F32)

    return m1.astype(BF16), m2.astype(BF16), table(np.cos(angt)), table(np.sin(angt))


def _dft_body(steps1, v_ref, m1_ref, twc_ref, tws_ref, m2_ref, y_ref, zs):
    s = pl.program_id(1)
    n2, a_blk, grp, w = v_ref.shape
    n1 = zs.shape[1]
    pair = grp // SUBLANES

    @pl.when(s < steps1)
    def _():
        reps = w // LANES
        for i in range(a_blk):
            xs = v_ref[:, i].reshape(n2 * grp, w)
            r = jnp.dot(m1_ref[...], xs, preferred_element_type=F32).reshape(n2, 2, SUBLANES, w)
            rr, ri = r[:, 0], r[:, 1]
            cw = jnp.concatenate([twc_ref[i]] * reps, axis=-1)
            sw = jnp.concatenate([tws_ref[i]] * reps, axis=-1)
            z = jnp.stack([rr * cw + ri * sw, ri * cw - rr * sw], axis=1)
            zs[:, s * a_blk + i] = z.reshape(n2, grp, w).astype(BF16)

    @pl.when(s >= steps1)
    def _():
        n_out = y_ref.shape[1] * pair
        c0 = (s - steps1) * n_out
        ys = [jnp.dot(m2_ref[...], zs[c0 + i].reshape(n1 * grp, w), preferred_element_type=F32)
              .reshape(n1, SUBLANES, w) for i in range(n_out)]
        y_ref[...] = jnp.stack(ys, axis=1).reshape(n1, n_out // pair, grp, w).astype(BF16)


def _position_dft(v4, n2, block0):
    m1, m2, twc, tws = _dft_constants(n2)
    n1 = DFT_RADIX
    grp, w = v4.shape[2], v4.shape[3]
    wh = min(w, DFT_LANE_BLOCK)
    a_blk = min(n1, max(1, DFT_STAGE1_ROWS // (n2 * grp)))
    steps1 = n1 // a_blk
    pair = grp // SUBLANES
    groups2 = min(DFT_STAGE2_GROUPS, n2 // pair)
    stage1_step = lambda s: jnp.minimum(s, steps1 - 1)
    return pl.pallas_call(
        functools.partial(_dft_body, steps1),
        grid=(w // wh, steps1 + n2 // (pair * groups2)),
        in_specs=[
            pl.BlockSpec((n2, a_blk, grp, wh), lambda c, s: (block0, stage1_step(s), 0, c)),
            _resident(m1.shape, lambda c, s: (0, 0)),
            pl.BlockSpec((a_blk, n2, SUBLANES, LANES), lambda c, s: (stage1_step(s), 0, 0, 0)),
            pl.BlockSpec((a_blk, n2, SUBLANES, LANES), lambda c, s: (stage1_step(s), 0, 0, 0)),
            _resident(m2.shape, lambda c, s: (0, 0)),
        ],
        out_specs=pl.BlockSpec((n1, groups2, grp, wh), lambda c, s: (0, jnp.maximum(s - steps1, 0), 0, c)),
        out_shape=jax.ShapeDtypeStruct((n1, n2 // pair, grp, w), BF16),
        scratch_shapes=[pltpu.VMEM((n2, n1, grp, wh), BF16)],
        compiler_params=_params(2),
        name="position_dft",
    )(v4, m1, twc, tws, m2)


def _pool_minus_token(ue, rows, t0, t_seq):
    halo = POOL_HALO_ROWS
    tvec = t0 + lax.broadcasted_iota(jnp.int32, (rows, LANES), 0) // SUBLANES
    parts = []
    for g, win in enumerate(POOL_WINDOW_SIZES):
        col = ue[:, g * LANES:(g + 1) * LANES]
        n = col.shape[0]
        acc = col[0:n - SUBLANES] + col[SUBLANES:n]
        e0 = SUBLANES
        span = 1
        while 2 * span < win:
            sh = span * SUBLANES
            n = acc.shape[0]
            acc = acc[0:n - 2 * sh] + acc[2 * sh:n]
            e0 += sh
            span *= 2
        wsum = acc[halo - e0:halo - e0 + rows]
        half = win // 2
        cnt = jnp.minimum(tvec + half, t_seq) - jnp.maximum(tvec - half, 0)
        parts.append(wsum / cnt.astype(F32) - col[halo:halo + rows])
    return jnp.concatenate(parts, axis=1)


def _k3_body(cfg, *refs):
    last, n_lat_tiles, n_tiles, t_lat, t_ctx = cfg
    refs = list(refs)
    x_ref, h_ref, mod_ref, hf_ref, hb_ref, yl_ref = refs[:6]
    refs = refs[6:]
    yc_ref = None if last else refs.pop(0)
    up_ref, upp_ref, upn_ref, wz_ref, pa_ref, pb_ref, pc_ref, wo_ref, pw_ref, ps_ref, fg_ref, o_ref = refs
    j = pl.program_id(0)
    is_lat = j < n_lat_tiles
    rows, d = x_ref.shape
    bsz = SUBLANES
    tt = rows // bsz
    h = h_ref[...]

    seq_first, seq_last = _seq_edges(j, n_lat_tiles, n_tiles)
    ue = jnp.concatenate([
        upp_ref[...].astype(F32) * jnp.where(seq_first, 0.0, 1.0).astype(F32),
        up_ref[...].astype(F32),
        upn_ref[...].astype(F32) * jnp.where(seq_last, 0.0, 1.0).astype(F32)], axis=0)
    t0 = jnp.where(is_lat, j, j - n_lat_tiles) * tt
    t_seq = jnp.where(is_lat, t_lat, t_ctx)
    p = _pool_minus_token(ue, rows, t0, t_seq).astype(BF16)

    w = yl_ref.shape[-1]
    z_all = jnp.dot(h, wz_ref[...], preferred_element_type=F32)
    c0 = 3 * d
    w_lru = hf_ref.shape[-1]
    z_lru, z_fft, z_pool = z_all[:, c0:c0 + w_lru], z_all[:, c0 + w_lru:c0 + w_lru + w], z_all[:, c0 + w_lru + w:]

    y_lru = hf_ref[0].astype(F32) + hb_ref[0].astype(F32)
    ya = jnp.dot((y_lru * _silu(z_lru)).astype(BF16), pa_ref[...], preferred_element_type=F32)

    y_fft = yl_ref[...].reshape(rows, w)
    if not last:
        y_fft = jnp.where(is_lat, y_fft, yc_ref[...].reshape(rows, w))
    yb = jnp.dot((y_fft.astype(F32) * _silu(z_fft)).astype(BF16), pb_ref[...], preferred_element_type=F32)

    y_pool = jnp.dot(p, pw_ref[...], preferred_element_type=F32) * ps_ref[...]
    yc = jnp.dot((y_pool * _silu(z_pool)).astype(BF16), pc_ref[...], preferred_element_type=F32)

    gates = _sigmoid(z_all[:, :c0])
    m = gates[:, :d] * ya + gates[:, d:2 * d] * yb + gates[:, 2 * d:] * yc
    out = jnp.dot(m.astype(BF16), wo_ref[...], preferred_element_type=F32)
    xn = x_ref[...].reshape(tt, bsz, d) + mod_ref[0, 2] * out.reshape(tt, bsz, d)
    if last:
        ms = jnp.mean(xn * xn, axis=-1, keepdims=True)
        xn = xn * lax.rsqrt(ms + NORM_EPS) * fg_ref[...]
        o_ref[...] = jnp.transpose(xn, (1, 0, 2))
    else:
        o_ref[...] = xn.reshape(rows, d)


def _k3(x_rows, h, mod_k, h_lru, y_lat, y_ctx, u_pool, wz, pa, pb, pc, wo, pw, ps, fg, last,
        n_lat_tiles, t_lat, t_ctx):
    n_all, d = h.shape
    bsz = SUBLANES
    r = TOKEN_TILE_ROWS
    tt = r // bsz
    n_tiles = n_all // r
    grid_tiles = n_lat_tiles if last else n_tiles
    w_lru = h_lru.shape[2]
    w_fft = y_lat.shape[-1]
    w_pool = u_pool.shape[1]
    hr = r // POOL_HALO_ROWS
    n_halo = n_all // POOL_HALO_ROWS
    cfg = (last, n_lat_tiles, n_tiles, t_lat, t_ctx)
    const2 = lambda j: (0, 0)
    assert [width for _, width in _merge_weight_segments(w_lru, w_fft, w_pool, d)] == [3 * d, w_lru, w_fft, w_pool]

    def dft_spec(y4, first_tile):
        n2 = y4.shape[1] * y4.shape[2] // bsz
        assert tt % n2 == 0
        blk = (tt // n2,) + y4.shape[1:]
        n_blk = y4.shape[0] // blk[0]
        return pl.BlockSpec(blk, lambda j: (jnp.clip(j - first_tile, 0, n_blk - 1), 0, 0, 0))

    row_spec = lambda width: pl.BlockSpec((r, width), lambda j: (j, 0))
    pair_spec = lambda k: pl.BlockSpec((1, r, w_lru), lambda j: (k, j, 0))
    in_specs = [
        row_spec(d), row_spec(d),
        pl.BlockSpec((1, 3, bsz, d), lambda j: (jnp.where(j < n_lat_tiles, 0, 1), 0, 0, 0)),
        pair_spec(0), pair_spec(1),
        dft_spec(y_lat, 0),
    ]
    args = [x_rows, h, mod_k, h_lru, h_lru, y_lat]
    if not last:
        in_specs.append(dft_spec(y_ctx, n_lat_tiles))
        args.append(y_ctx)
    in_specs += [
        row_spec(w_pool),
        pl.BlockSpec((POOL_HALO_ROWS, w_pool), lambda j: (jnp.maximum(j * hr - 1, 0), 0)),
        pl.BlockSpec((POOL_HALO_ROWS, w_pool), lambda j: (jnp.minimum((j + 1) * hr, n_halo - 1), 0)),
        _resident(wz.shape, const2),
    ]
    args += [u_pool, u_pool, u_pool, wz]
    for wgt in (pa, pb, pc, wo, pw, ps, fg):
        in_specs.append(_resident(wgt.shape, const2))
        args.append(wgt)
    if last:
        out_spec = pl.BlockSpec((bsz, tt, d), lambda j: (0, j, 0))
        out_shape = jax.ShapeDtypeStruct((bsz, t_lat, d), F32)
    else:
        out_spec = row_spec(d)
        out_shape = jax.ShapeDtypeStruct((n_all, d), F32)
    return pl.pallas_call(
        functools.partial(_k3_body, cfg),
        grid=(grid_tiles,),
        in_specs=in_specs,
        out_specs=out_spec,
        out_shape=out_shape,
        compiler_params=_params(),
        name="merge_residual",
    )(*args)


def _position_code(n_tokens, d):
    rows = n_tokens // GRID_WIDTH
    quarter = d // 4
    omega = 1.0 / (POSITION_BASE ** (jnp.arange(quarter, dtype=F32) / quarter))

    def emb(n):
        ang = jnp.arange(n).astype(F32)[:, None] * omega[None, :]
        return jnp.concatenate([jnp.sin(ang), jnp.cos(ang)], axis=-1)

    row_code = jnp.repeat(emb(rows), GRID_WIDTH, axis=0)
    col_code = jnp.tile(emb(GRID_WIDTH), (rows, 1))
    return jnp.concatenate([row_code, col_code], axis=-1).astype(F32)


def _block_diag(w):
    g, n, _ = w.shape
    eye = jnp.eye(g, dtype=w.dtype)
    return jnp.einsum("gij,gk->gikj", w, eye).reshape(g * n, g * n)


def _channel_dft_matrix(n):
    k = np.arange(n)
    ang = 2.0 * np.pi * ((k[:, None] * k[None, :]) % n) / n
    return jnp.asarray(np.concatenate([np.cos(ang), -np.sin(ang)], axis=1) / np.sqrt(n), F32)


def _fold_body(cs_ref, fw_ref, o_ref):
    _, groups, n, _ = fw_ref.shape
    o_ref[...] = jnp.zeros_like(o_ref)
    for g in range(groups):
        for part in range(2):
            blk = jnp.dot(cs_ref[:, part * n:(part + 1) * n], fw_ref[0, g], preferred_element_type=F32,
                          precision=lax.Precision.HIGHEST)
            col = part * groups * n + g * n
            o_ref[0, g * n:(g + 1) * n, col:col + n] = blk.astype(BF16)


def _fold_channel_map(cs, fw):
    depth, groups, n, _ = fw.shape
    w = groups * n
    return pl.pallas_call(
        _fold_body,
        grid=(depth,),
        in_specs=[pl.BlockSpec(cs.shape, lambda l: (0, 0)), pl.BlockSpec((1, groups, n, n), lambda l: (l, 0, 0, 0))],
        out_specs=pl.BlockSpec((1, w, 2 * w), lambda l: (l, 0, 0)),
        out_shape=jax.ShapeDtypeStruct((depth, w, 2 * w), BF16),
        compiler_params=_params(),
        name="fold_fft_map",
    )(cs, fw)


def kernel(x, c, ctx, c_ctx, norm_g, ada_w, ada_b, w_in, conv_w, conv_b, lru_wa, lru_ba, lru_wx, lru_bx, lru_lam,
           fft_w, pool_w, pool_scale, proj_a, proj_b, proj_c, w_out, final_g):
    bsz, t_lat, d = x.shape
    t_ctx = ctx.shape[1]
    depth = w_in.shape[0]
    w_lru = conv_w.shape[-1]
    fft_groups, fft_dim = fft_w.shape[1], fft_w.shape[2]
    w_fft = fft_groups * fft_dim
    w_pool = pool_w.shape[1] * pool_w.shape[2]
    heads, head_dim = lru_wa.shape[2], lru_wa.shape[3]
    assert bsz == SUBLANES and fft_dim == LANES and pool_w.shape[2] == LANES and conv_w.shape[1] == CONV_TAPS
    assert t_lat % DFT_RADIX == 0 and t_ctx % DFT_RADIX == 0 and (t_lat // DFT_RADIX) % (t_ctx // DFT_RADIX) == 0
    n_lat_rows, n_ctx_rows = t_lat * bsz, t_ctx * bsz
    n_all = n_lat_rows + n_ctx_rows
    n_lat_tiles = n_lat_rows // TOKEN_TILE_ROWS
    lru_steps = min(LRU_CHUNK_STEPS, t_ctx)
    heads_per_group = LRU_GATE_WIDTH // head_dim

    pos = _position_code(t_lat, d)
    cc =jnp.concatenate([c, jnp.broadcast_to(c_ctx[None, :], (bsz, d))], axis=0)
    mod = _modulation(cc, ada_w, ada_b)
    cdw = _fold_channel_map(_channel_dft_matrix(fft_dim), fft_w)
    n2_lat, n2_ctx = t_lat // DFT_RADIX, t_ctx // DFT_RADIX

    def gate_w(w):
        wd = w.reshape(2 * heads // heads_per_group, heads_per_group, head_dim, head_dim)
        return jax.vmap(_block_diag)(wd).reshape(2, heads // heads_per_group, LRU_GATE_WIDTH, LRU_GATE_WIDTH)

    tokens = (x, _batch_major_to_rows(ctx), pos)
    out = None
    for l in range(depth):
        first, last = l == 0, l == depth - 1
        mod_k = mod[l].reshape(2, bsz, 3, d).transpose(0, 2, 1, 3)
        g = norm_g[l].reshape(1, d)
        k1_out = _k1(tokens, mod_k, g, w_in, l, cdw, conv_w[l], conv_b[l], first, n_all, n_lat_tiles,
                     w_lru, w_fft, w_pool)
        xh, v4, u_pool, h = k1_out[:4]
        x_rows = k1_out[4] if first else tokens[0]

        wg = jnp.concatenate([gate_w(lru_wa[l]), gate_w(lru_wx[l])], axis=-1).astype(BF16)
        h_lru, wz = _lru(xh, wg, lru_ba[l], lru_bx[l], lru_lam[l], w_in, l,
                         _merge_weight_segments(w_lru, w_fft, w_pool, d), n_lat_rows, n_ctx_rows, lru_steps)

        y_lat = _position_dft(v4, n2_lat, 0)
        y_ctx = None if last else _position_dft(v4, n2_ctx, n2_lat // n2_ctx)

        out = _k3(x_rows, h, mod_k, h_lru, y_lat, y_ctx, u_pool, wz,
                  proj_a[l].astype(BF16), proj_b[l].astype(BF16), proj_c[l].astype(BF16), w_out[l].astype(BF16),
                  _block_diag(pool_w[l]).astype(BF16),
                  pool_scale[l].reshape(1, w_pool), final_g.reshape(1, d), last, n_lat_tiles, t_lat, t_ctx)
        tokens = (out,)
    return out
```

```python
import functools

import numpy as np
import jax
import jax.numpy as jnp
from jax import lax
from jax.experimental import pallas as pl
from jax.experimental.pallas import tpu as pltpu

F32 = jnp.float32
BF16 = jnp.bfloat16

GRID_WIDTH = 64
LRU_POWER = 8.0
NORM_EPS = 1e-6
POSITION_BASE = 10000.0
POOL_WINDOW_SIZES = (2, 4, 8, 16)
CONV_TAPS = 4
SUBLANES = 8
LANES = 128
DFT_RADIX = 64
DFT_GROUP_ROWS = 2 * SUBLANES
DFT_STAGE1_ROWS = 4096
DFT_STAGE2_GROUPS = 2
DFT_LANE_BLOCK = 256
POOL_HALO_ROWS = 64
TOKEN_TILE_ROWS = DFT_RADIX * SUBLANES
LRU_CHUNK_STEPS = 256
LRU_GATE_WIDTH = 256
SCAN_UNROLL = 8
VMEM_LIMIT_BYTES = 56 * 1024 * 1024


def _half_tanh_sigmoid(x_half):
    return 0.5 * (1.0 + jnp.tanh(x_half))


def _sigmoid(x):
    return _half_tanh_sigmoid(0.5 * x)


def _silu(x):
    return x * _sigmoid(x)


def _params(n_axes=1):
    return pltpu.CompilerParams(dimension_semantics=("arbitrary",) * n_axes, vmem_limit_bytes=VMEM_LIMIT_BYTES)


def _resident(shape, index_map):
    return pl.BlockSpec(shape, index_map, pipeline_mode=pl.Buffered(1))


def _mod_body(cc_ref, w_ref, b_ref, o_ref):
    s = _silu(cc_ref[...]).astype(BF16)
    o_ref[0] = jnp.dot(s, w_ref[0].astype(BF16), preferred_element_type=F32) + b_ref[0]


def _modulation(cc, ada_w, ada_b):
    depth, d, n = ada_w.shape
    tn = 1024
    return pl.pallas_call(
        _mod_body,
        grid=(depth, n // tn),
        in_specs=[
            pl.BlockSpec((2 * SUBLANES, d), lambda l, i: (0, 0)),
            pl.BlockSpec((1, d, tn), lambda l, i: (l, 0, i)),
            pl.BlockSpec((1, 1, tn), lambda l, i: (l, 0, i)),
        ],
        out_specs=pl.BlockSpec((1, 2 * SUBLANES, tn), lambda l, i: (l, 0, i)),
        out_shape=jax.ShapeDtypeStruct((depth, 2 * SUBLANES, n), F32),
        compiler_params=_params(2),
        name="adaln_mod",
    )(cc, ada_w, ada_b.reshape(depth, 1, n))


def _rows_body(x_ref, o_ref):
    bsz, tt, d = x_ref.shape
    o_ref[...] = jnp.transpose(x_ref[...], (1, 0, 2)).reshape(tt * bsz, d)


def _batch_major_to_rows(x):
    bsz, t, d = x.shape
    tt = TOKEN_TILE_ROWS // bsz
    return pl.pallas_call(
        _rows_body,
        grid=(t // tt,),
        in_specs=[pl.BlockSpec((bsz, tt, d), lambda j: (0, j, 0))],
        out_specs=pl.BlockSpec((tt * bsz, d), lambda j: (j, 0)),
        out_shape=jax.ShapeDtypeStruct((t * bsz, d), x.dtype),
        compiler_params=_params(),
        name="ctx_rows",
    )(x)


def _seq_edges(p, n_lat_tiles, n_tiles):
    is_first = jnp.logical_or(p == 0, p == n_lat_tiles)
    is_last = jnp.logical_or(p == n_lat_tiles - 1, p == n_tiles - 1)
    return is_first, is_last


def _k1_body(first, n_lat_tiles, n_tiles, *refs):
    n_tok = 3 if first else 1
    tok_refs = refs[:n_tok]
    refs = list(refs[n_tok:])
    mod_ref, g_ref, wl_ref, wf_ref, wp_ref, cd_ref, cw_ref, cb_ref, xh_ref, v_ref, upool_ref, h_ref = refs[:12]
    refs = refs[12:]
    x0_ref = refs.pop(0) if first else None
    uprev, tail, wu_s = refs
    w_lru, w_fft = wl_ref.shape[2], wf_ref.shape[2]
    j = pl.program_id(0)
    rows, d = h_ref.shape
    bsz = SUBLANES
    tt = rows // bsz
    tail_rows = tail.shape[0]

    @pl.when(j == 0)
    def _():
        uprev[...] = jnp.zeros_like(uprev)
        tail[...] = jnp.zeros_like(tail)
        wu_s[:, :w_lru] = wl_ref[0].astype(BF16)
        wu_s[:, w_lru:w_lru + w_fft] = wf_ref[0].astype(BF16)
        wu_s[:, w_lru + w_fft:] = wp_ref[0].astype(BF16)

    if first:
        x_ref, ctx_ref, code_ref = tok_refs
        grid_row = jnp.minimum(j, n_lat_tiles - 1)
        row_code = jnp.broadcast_to(code_ref[pl.ds(grid_row, 1), :], (tt, d // 2))
        pos = jnp.concatenate([row_code, code_ref[0:tt, :]], axis=-1)
        x3 = jnp.where(j < n_lat_tiles, jnp.transpose(x_ref[...] + pos[None], (1, 0, 2)),
                       ctx_ref[...].reshape(tt, bsz, d))
        x0_ref[...] = x3.reshape(rows, d)
    else:
        x3 = tok_refs[0][...].reshape(tt, bsz, d)
    ms = jnp.mean(x3 * x3, axis=-1, keepdims=True)
    gain = g_ref[...] * (1.0 + mod_ref[0, 1])
    h = (x3 * lax.rsqrt(ms + NORM_EPS) * gain + mod_ref[0, 0]).reshape(rows, d).astype(BF16)
    h_ref[...] = h
    u_all = jnp.dot(h, wu_s[...], preferred_element_type=F32)
    u = u_all[:, :w_lru]
    ufft = u_all[:, w_lru:w_lru + w_fft].astype(BF16)
    v = jnp.dot(ufft, cd_ref[0], preferred_element_type=F32)
    w = v.shape[1] // 2
    v4 = jnp.stack([v[:, :w].reshape(tt, bsz, w), v[:, w:].reshape(tt, bsz, w)], axis=1)
    v_ref[...] = v4.reshape(1, tt, DFT_GROUP_ROWS, w).astype(BF16)
    upool_ref[...] = u_all[:, w_lru + w_fft:].astype(BF16)

    p_first, p_last = _seq_edges(j - 1, n_lat_tiles, n_tiles)
    up = uprev[...]
    before = jnp.where(p_first, 0.0, tail[...])
    after = jnp.where(p_last, 0.0, u[0:SUBLANES, :])
    ext = jnp.concatenate([before, up, after], axis=0)
    cwh = 0.5 * cw_ref[...]
    xh = 0.5 * cb_ref[...] + cwh[0:1] * ext[0:rows]
    for k in range(1, CONV_TAPS):
        xh = xh + cwh[k:k + 1] * ext[k * SUBLANES:k * SUBLANES + rows]
    xh_ref[...] = xh.astype(BF16)

    tail[...] = up[rows - tail_rows:rows, :]
    uprev[...] = u


def _k1(tokens, mod_k, g, w_all, layer, cd, conv_w, conv_b, first, n_all, n_lat_tiles, w_lru, w_fft, w_pool):
    d = g.shape[-1]
    bsz = SUBLANES
    r = TOKEN_TILE_ROWS
    tt = r // bsz
    n_tiles = n_all // r
    n_ctx_tiles = n_tiles - n_lat_tiles
    const2 = lambda j: (0, 0)
    off_fft, off_pool = 2 * w_lru, 2 * w_lru + 2 * w_fft
    assert off_fft % w_fft == 0 and off_pool % w_pool == 0
    clamp = lambda j: (jnp.minimum(j, n_tiles - 1), 0)

    if first:
        tok_specs = [
            pl.BlockSpec((bsz, tt, d), lambda j: (0, jnp.minimum(j, n_lat_tiles - 1), 0)),
            pl.BlockSpec((r, d), lambda j: (jnp.clip(j - n_lat_tiles, 0, n_ctx_tiles - 1), 0)),
            pl.BlockSpec(tokens[2].shape, const2),
        ]
    else:
        tok_specs = [pl.BlockSpec((r, d), clamp)]
    in_specs = tok_specs + [
        pl.BlockSpec((1, 3, bsz, d), lambda j: (jnp.where(j < n_lat_tiles, 0, 1), 0, 0, 0)),
        pl.BlockSpec((1, d), const2),
        _resident((1, d, w_lru), lambda j: (layer, 0, 0)),
        _resident((1, d, w_fft), lambda j: (layer, 0, off_fft // w_fft)),
        _resident((1, d, w_pool), lambda j: (layer, 0, off_pool // w_pool)),
        _resident((1,) + cd.shape[1:], lambda j: (layer, 0, 0)),
        pl.BlockSpec((CONV_TAPS, w_lru), const2),
        pl.BlockSpec((1, w_lru), const2),
    ]
    out_specs = [
        pl.BlockSpec((r, w_lru), lambda j: (jnp.maximum(j - 1, 0), 0)),
        pl.BlockSpec((1, tt, DFT_GROUP_ROWS, w_fft), lambda j: (jnp.minimum(j, n_tiles - 1), 0, 0, 0)),
        pl.BlockSpec((r, w_pool), clamp),
        pl.BlockSpec((r, d), clamp),
    ]
    out_shape = [
        jax.ShapeDtypeStruct((n_all, w_lru), BF16),
        jax.ShapeDtypeStruct((n_tiles, tt, DFT_GROUP_ROWS, w_fft), BF16),
        jax.ShapeDtypeStruct((n_all, w_pool), BF16),
        jax.ShapeDtypeStruct((n_all, d), BF16),
    ]
    if first:
        out_specs.append(pl.BlockSpec((r, d), clamp))
        out_shape.append(jax.ShapeDtypeStruct((n_all, d), F32))
    return pl.pallas_call(
        functools.partial(_k1_body, first, n_lat_tiles, n_tiles),
        grid=(n_tiles + 1,),
        in_specs=in_specs,
        out_specs=out_specs,
        out_shape=out_shape,
        scratch_shapes=[pltpu.VMEM((r, w_lru), F32), pltpu.VMEM(((CONV_TAPS - 2) * SUBLANES, w_lru), F32),
                        pltpu.VMEM((d, w_lru + w_fft + w_pool), BF16)],
        compiler_params=_params(),
        name="norm_uproj",
    )(*tokens, mod_k, g, w_all, w_all, w_all, cd, conv_w, conv_b.reshape(1, w_lru))


def _merge_weight_segments(w_lru, w_fft, w_pool, d):
    return [(2 * w_lru + 2 * w_fft + 2 * w_pool, 3 * d), (w_lru, w_lru), (2 * w_lru + w_fft, w_fft),
            (2 * w_lru + 2 * w_fft + w_pool, w_pool)]


def _lru_chunk(d, j, n_lat_chunks, n_ctx_chunks):
    in_ctx = j < n_ctx_chunks
    jj = j - n_ctx_chunks
    ctx_ck = n_lat_chunks + jnp.where(d == 0, j, n_ctx_chunks - 1 - j)
    lat_ck = jnp.where(d == 0, jj, n_lat_chunks - 1 - jj)
    return jnp.where(in_ctx, ctx_ck, lat_ck)


def _lru_body(steps, xh_ref, wg_ref, ba_ref, bx_ref, lam_ref, ws_ref, o_ref, wz_ref, a_sc, b_sc, h_sc, state):
    d = pl.program_id(0)
    j = pl.program_id(1)
    cb = xh_ref.shape[1]
    wz_ref[...] = ws_ref[0].astype(BF16)

    @pl.when(j == 0)
    def _():
        state[...] = jnp.zeros_like(state)

    lam = lam_ref[0]
    half_rate = (-0.5 * LRU_POWER) * (jnp.maximum(-lam, 0.0) + jnp.log1p(jnp.exp(-jnp.abs(lam))))
    half_ba = 0.5 * ba_ref[0]
    half_bx = 0.5 * bx_ref[0]
    gw = wg_ref.shape[2]
    for g in range(cb // gw):
        sl = slice(g * gw, (g + 1) * gw)
        xb = xh_ref[:, sl]
        gates = jnp.dot(xb, wg_ref[0, g], preferred_element_type=F32)
        log_a = (1.0 + jnp.tanh(gates[:, :gw] + half_ba[:, sl])) * half_rate[:, sl]
        a = jnp.exp(log_a)
        q = jnp.tanh(log_a) * (-1.0 - a * a)
        gain = jnp.where(q > 0.0, q * lax.rsqrt(q), 0.0)
        a_sc[:, sl] = a
        b_sc[:, sl] = gain * ((1.0 + jnp.tanh(gates[:, gw:] + half_bx[:, sl])) * xb.astype(F32))

    n_blocks = steps // SCAN_UNROLL
    block_rows = SCAN_UNROLL * SUBLANES

    def scan(reverse):
        def block(i, h):
            base = pl.multiple_of((n_blocks - 1 - i if reverse else i) * block_rows, block_rows)
            for k in (range(SCAN_UNROLL - 1, -1, -1) if reverse else range(SCAN_UNROLL)):
                r0 = base + k * SUBLANES
                h = a_sc[pl.ds(r0, SUBLANES), :] * h + b_sc[pl.ds(r0, SUBLANES), :]
                h_sc[pl.ds(r0, SUBLANES), :] = h
            return h

        state[...] = lax.fori_loop(0, n_blocks, block, state[...])

    @pl.when(d == 0)
    def _():
        scan(False)

    @pl.when(d == 1)
    def _():
        scan(True)

    o_ref[0] = h_sc[...].astype(BF16)


def _lru(xh, wg, ba, bx, lam, w_in, layer, segs, n_lat_rows, n_ctx_rows, steps):
    n_all, w = xh.shape
    d_model = w_in.shape[1]
    rc = steps * SUBLANES
    n_lat_chunks = n_lat_rows // rc
    n_ctx_chunks = n_ctx_rows // rc
    n_chunks = n_lat_chunks + n_ctx_chunks
    gw = wg.shape[2]
    ck = functools.partial(_lru_chunk, n_lat_chunks=n_lat_chunks, n_ctx_chunks=n_ctx_chunks)

    total = sum(width for _, width in segs)
    slab_w = next(c for c in (LANES, 2 * LANES, 4 * LANES, 8 * LANES) if total // c <= 2 * n_chunks)
    assert all(off % slab_w == 0 and width % slab_w == 0 for off, width in segs)
    n_slabs = total // slab_w
    bounds = np.cumsum([0] + [width // slab_w for _, width in segs])
    slab = lambda d, j: jnp.minimum(d * n_chunks + j, n_slabs - 1)

    def slab_source(d, j):
        s = slab(d, j)
        blk = segs[-1][0] // slab_w + (s - int(bounds[-2]))
        for k in range(len(segs) - 2, -1, -1):
            blk = jnp.where(s < int(bounds[k + 1]), segs[k][0] // slab_w + (s - int(bounds[k])), blk)
        return blk

    return pl.pallas_call(
        functools.partial(_lru_body, steps),
        grid=(2, n_chunks),
        in_specs=[
            pl.BlockSpec((rc, w), lambda d, j: (ck(d, j), 0)),
            pl.BlockSpec((1, w // gw, gw, 2 * gw), lambda d, j: (d, 0, 0, 0)),
            pl.BlockSpec((1, 1, w), lambda d, j: (d, 0, 0)),
            pl.BlockSpec((1, 1, w), lambda d, j: (d, 0, 0)),
            pl.BlockSpec((1, 1, w), lambda d, j: (d, 0, 0)),
            pl.BlockSpec((1, d_model, slab_w), lambda d, j: (layer, 0, slab_source(d, j))),
        ],
        out_specs=[pl.BlockSpec((1, rc, w), lambda d, j: (d, ck(d, j), 0)),
                   pl.BlockSpec((d_model, slab_w), lambda d, j: (0, slab(d, j)))],
        out_shape=[jax.ShapeDtypeStruct((2, n_all, w), BF16), jax.ShapeDtypeStruct((d_model, total), BF16)],
        scratch_shapes=[
            pltpu.VMEM((rc, w), F32),
            pltpu.VMEM((rc, w), F32),
            pltpu.VMEM((rc, w), F32),
            pltpu.VMEM((SUBLANES, w), F32),
        ],
        compiler_params=_params(2),
        name="rglru_scan",
    )(xh, wg, ba.reshape(2, 1, w), bx.reshape(2, 1, w), lam.reshape(2, 1, w), w_in)


def _dft_constants(n2):
    n1 = DFT_RADIX
    n = n1 * n2
    eye = np.eye(SUBLANES)
    k2 = np.arange(n2)
    ang1 = 2.0 * np.pi * ((k2[:, None] * k2[None, :]) % n2) / n2
    c1, s1 = np.cos(ang1) / np.sqrt(n2), np.sin(ang1) / np.sqrt(n2)
    base1 = np.stack([np.stack([c1, s1], axis=-1), np.stack([-s1, c1], axis=-1)], axis=1)
    m1 = jnp.asarray(np.kron(base1.reshape(2 * n2, 2 * n2), eye), F32)
    k1 = np.arange(n1)
    ang2 = 2.0 * np.pi * ((k1[:, None] * k1[None, :]) % n1) / n1
    base2 = np.stack([np.cos(ang2), np.sin(ang2)], axis=-1) / np.sqrt(n1)
    m2 = jnp.asarray(np.kron(base2.reshape(n1, 2 * n1), eye), F32)
    angt = 2.0 * np.pi * ((k1[:, None] * k2[None, :]) % n) / n
    shape = (n1, n2, SUBLANES, LANES)

    def table(vals):
        return jnp.asarray(np.ascontiguousarray(np.broadcast_to(vals[:, :, None, None], shape)), F32)

    return m1.astype(BF16), m2.astype(BF16), table(np.cos(angt)), table(np.sin(angt))


def _dft_body(steps1, v_ref, m1_ref, twc_ref, tws_ref, m2_ref, y_ref, zs):
    s = pl.program_id(1)
    n2, a_blk, grp, w = v_ref.shape
    n1 = zs.shape[1]
    pair = grp // SUBLANES

    @pl.when(s < steps1)
    def _():
        reps = w // LANES
        for i in range(a_blk):
            xs = v_ref[:, i].reshape(n2 * grp, w)
            r = jnp.dot(m1_ref[...], xs, preferred_element_type=F32).reshape(n2, 2, SUBLANES, w)
            rr, ri = r[:, 0], r[:, 1]
            cw = jnp.concatenate([twc_ref[i]] * reps, axis=-1)
            sw = jnp.concatenate([tws_ref[i]] * reps, axis=-1)
            z = jnp.stack([rr * cw + ri * sw, ri * cw - rr * sw], axis=1)
            zs[:, s * a_blk + i] = z.reshape(n2, grp, w).astype(BF16)

    @pl.when(s >= steps1)
    def _():
        n_out = y_ref.shape[1] * pair
        c0 = (s - steps1) * n_out
        ys = [jnp.dot(m2_ref[...], zs[c0 + i].reshape(n1 * grp, w), preferred_element_type=F32)
              .reshape(n1, SUBLANES, w) for i in range(n_out)]
        y_ref[...] = jnp.stack(ys, axis=1).reshape(n1, n_out // pair, grp, w).astype(BF16)


def _position_dft(v4, n2, block0):
    m1, m2, twc, tws = _dft_constants(n2)
    n1 = DFT_RADIX
    grp, w = v4.shape[2], v4.shape[3]
    wh = min(w, DFT_LANE_BLOCK)
    a_blk = min(n1, max(1, DFT_STAGE1_ROWS // (n2 * grp)))
    steps1 = n1 // a_blk
    pair = grp // SUBLANES
    groups2 = min(DFT_STAGE2_GROUPS, n2 // pair)
    stage1_step = lambda s: jnp.minimum(s, steps1 - 1)
    return pl.pallas_call(
        functools.partial(_dft_body, steps1),
        grid=(w // wh, steps1 + n2 // (pair * groups2)),
        in_specs=[
            pl.BlockSpec((n2, a_blk, grp, wh), lambda c, s: (block0, stage1_step(s), 0, c)),
            _resident(m1.shape, lambda c, s: (0, 0)),
            pl.BlockSpec((a_blk, n2, SUBLANES, LANES), lambda c, s: (stage1_step(s), 0, 0, 0)),
            pl.BlockSpec((a_blk, n2, SUBLANES, LANES), lambda c, s: (stage1_step(s), 0, 0, 0)),
            _resident(m2.shape, lambda c, s: (0, 0)),
        ],
        out_specs=pl.BlockSpec((n1, groups2, grp, wh), lambda c, s: (0, jnp.maximum(s - steps1, 0), 0, c)),
        out_shape=jax.ShapeDtypeStruct((n1, n2 // pair, grp, w), BF16),
        scratch_shapes=[pltpu.VMEM((n2, n1, grp, wh), BF16)],
        compiler_params=_params(2),
        name="position_dft",
    )(v4, m1, twc, tws, m2)


def _pool_minus_token(ue, rows, t0, t_seq):
    halo = POOL_HALO_ROWS
    tvec = t0 + lax.broadcasted_iota(jnp.int32, (rows, LANES), 0) // SUBLANES
    parts = []
    for g, win in enumerate(POOL_WINDOW_SIZES):
        col = ue[:, g * LANES:(g + 1) * LANES]
        n = col.shape[0]
        acc = col[0:n - SUBLANES] + col[SUBLANES:n]
        e0 = SUBLANES
        span = 1
        while 2 * span < win:
            sh = span * SUBLANES
            n = acc.shape[0]
            acc = acc[0:n - 2 * sh] + acc[2 * sh:n]
            e0 += sh
            span *= 2
        wsum = acc[halo - e0:halo - e0 + rows]
        half = win // 2
        cnt = jnp.minimum(tvec + half, t_seq) - jnp.maximum(tvec - half, 0)
        parts.append(wsum / cnt.astype(F32) - col[halo:halo + rows])
    return jnp.concatenate(parts, axis=1)


def _k3_body(cfg, *refs):
    last, n_lat_tiles, n_tiles, t_lat, t_ctx = cfg
    refs = list(refs)
    x_ref, h_ref, mod_ref, hf_ref, hb_ref, yl_ref = refs[:6]
    refs = refs[6:]
    yc_ref = None if last else refs.pop(0)
    up_ref, upp_ref, upn_ref, wz_ref, pa_ref, pb_ref, pc_ref, wo_ref, pw_ref, ps_ref, fg_ref, o_ref = refs
    j = pl.program_id(0)
    is_lat = j < n_lat_tiles
    rows, d = x_ref.shape
    bsz = SUBLANES
    tt = rows // bsz
    h = h_ref[...]

    seq_first, seq_last = _seq_edges(j, n_lat_tiles, n_tiles)
    ue = jnp.concatenate([
        upp_ref[...].astype(F32) * jnp.where(seq_first, 0.0, 1.0).astype(F32),
        up_ref[...].astype(F32),
        upn_ref[...].astype(F32) * jnp.where(seq_last, 0.0, 1.0).astype(F32)], axis=0)
    t0 = jnp.where(is_lat, j, j - n_lat_tiles) * tt
    t_seq = jnp.where(is_lat, t_lat, t_ctx)
    p = _pool_minus_token(ue, rows, t0, t_seq).astype(BF16)

    w = yl_ref.shape[-1]
    z_all = jnp.dot(h, wz_ref[...], preferred_element_type=F32)
    c0 = 3 * d
    w_lru = hf_ref.shape[-1]
    z_lru, z_fft, z_pool = z_all[:, c0:c0 + w_lru], z_all[:, c0 + w_lru:c0 + w_lru + w], z_all[:, c0 + w_lru + w:]

    y_lru = hf_ref[0].astype(F32) + hb_ref[0].astype(F32)
    ya = jnp.dot((y_lru * _silu(z_lru)).astype(BF16), pa_ref[...], preferred_element_type=F32)

    y_fft = yl_ref[...].reshape(rows, w)
    if not last:
        y_fft = jnp.where(is_lat, y_fft, yc_ref[...].reshape(rows, w))
    yb = jnp.dot((y_fft.astype(F32) * _silu(z_fft)).astype(BF16), pb_ref[...], preferred_element_type=F32)

    y_pool = jnp.dot(p, pw_ref[...], preferred_element_type=F32) * ps_ref[...]
    yc = jnp.dot((y_pool * _silu(z_pool)).astype(BF16), pc_ref[...], preferred_element_type=F32)

    gates = _sigmoid(z_all[:, :c0])
    m = gates[:, :d] * ya + gates[:, d:2 * d] * yb + gates[:, 2 * d:] * yc
    out = jnp.dot(m.astype(BF16), wo_ref[...], preferred_element_type=F32)
    xn = x_ref[...].reshape(tt, bsz, d) + mod_ref[0, 2] * out.reshape(tt, bsz, d)
    if last:
        ms = jnp.mean(xn * xn, axis=-1, keepdims=True)
        xn = xn * lax.rsqrt(ms + NORM_EPS) * fg_ref[...]
        o_ref[...] = jnp.transpose(xn, (1, 0, 2))
    else:
        o_ref[...] = xn.reshape(rows, d)


def _k3(x_rows, h, mod_k, h_lru, y_lat, y_ctx, u_pool, wz, pa, pb, pc, wo, pw, ps, fg, last,
        n_lat_tiles, t_lat, t_ctx):
    n_all, d = h.shape
    bsz = SUBLANES
    r = TOKEN_TILE_ROWS
    tt = r // bsz
    n_tiles = n_all // r
    grid_tiles = n_lat_tiles if last else n_tiles
    w_lru = h_lru.shape[2]
    w_fft = y_lat.shape[-1]
    w_pool = u_pool.shape[1]
    hr = r // POOL_HALO_ROWS
    n_halo = n_all // POOL_HALO_ROWS
    cfg = (last, n_lat_tiles, n_tiles, t_lat, t_ctx)
    const2 = lambda j: (0, 0)
    assert [width for _, width in _merge_weight_segments(w_lru, w_fft, w_pool, d)] == [3 * d, w_lru, w_fft, w_pool]

    def dft_spec(y4, first_tile):
        n2 = y4.shape[1] * y4.shape[2] // bsz
        assert tt % n2 == 0
        blk = (tt // n2,) + y4.shape[1:]
        n_blk = y4.shape[0] // blk[0]
        return pl.BlockSpec(blk, lambda j: (jnp.clip(j - first_tile, 0, n_blk - 1), 0, 0, 0))

    row_spec = lambda width: pl.BlockSpec((r, width), lambda j: (j, 0))
    pair_spec = lambda k: pl.BlockSpec((1, r, w_lru), lambda j: (k, j, 0))
    in_specs = [
        row_spec(d), row_spec(d),
        pl.BlockSpec((1, 3, bsz, d), lambda j: (jnp.where(j < n_lat_tiles, 0, 1), 0, 0, 0)),
        pair_spec(0), pair_spec(1),
        dft_spec(y_lat, 0),
    ]
    args = [x_rows, h, mod_k, h_lru, h_lru, y_lat]
    if not last:
        in_specs.append(dft_spec(y_ctx, n_lat_tiles))
        args.append(y_ctx)
    in_specs += [
        row_spec(w_pool),
        pl.BlockSpec((POOL_HALO_ROWS, w_pool), lambda j: (jnp.maximum(j * hr - 1, 0), 0)),
        pl.BlockSpec((POOL_HALO_ROWS, w_pool), lambda j: (jnp.minimum((j + 1) * hr, n_halo - 1), 0)),
        _resident(wz.shape, const2),
    ]
    args += [u_pool, u_pool, u_pool, wz]
    for wgt in (pa, pb, pc, wo, pw, ps, fg):
        in_specs.append(_resident(wgt.shape, const2))
        args.append(wgt)
    if last:
        out_spec = pl.BlockSpec((bsz, tt, d), lambda j: (0, j, 0))
        out_shape = jax.ShapeDtypeStruct((bsz, t_lat, d), F32)
    else:
        out_spec = row_spec(d)
        out_shape = jax.ShapeDtypeStruct((n_all, d), F32)
    return pl.pallas_call(
        functools.partial(_k3_body, cfg),
        grid=(grid_tiles,),
        in_specs=in_specs,
        out_specs=out_spec,
        out_shape=out_shape,
        compiler_params=_params(),
        name="merge_residual",
    )(*args)


def _coordinate_code(n, d):
    quarter = d // 4
    omega = 1.0 / (POSITION_BASE ** (jnp.arange(quarter, dtype=F32) / quarter))
    ang = jnp.arange(n).astype(F32)[:, None] * omega[None, :]
    return jnp.concatenate([jnp.sin(ang), jnp.cos(ang)], axis=-1).astype(F32)


def _block_diag(w):
    g, n, _ = w.shape
    eye = jnp.eye(g, dtype=w.dtype)
    return jnp.einsum("gij,gk->gikj", w, eye).reshape(g * n, g * n)


def _channel_dft_matrix(n):
    k = np.arange(n)
    ang = 2.0 * np.pi * ((k[:, None] * k[None, :]) % n) / n
    return jnp.asarray(np.concatenate([np.cos(ang), -np.sin(ang)], axis=1) / np.sqrt(n), F32)


def _fold_body(cs_ref, fw_ref, o_ref):
    _, groups, n, _ = fw_ref.shape
    o_ref[...] = jnp.zeros_like(o_ref)
    for g in range(groups):
        for part in range(2):
            blk = jnp.dot(cs_ref[:, part * n:(part + 1) * n], fw_ref[0, g], preferred_element_type=F32,
                          precision=lax.Precision.HIGHEST)
            col = part * groups * n + g * n
            o_ref[0, g * n:(g + 1) * n, col:col + n] = blk.astype(BF16)


def _fold_channel_map(cs, fw):
    depth, groups, n, _ = fw.shape
    w = groups * n
    return pl.pallas_call(
        _fold_body,
        grid=(depth,),
        in_specs=[pl.BlockSpec(cs.shape, lambda l: (0, 0)), pl.BlockSpec((1, groups, n, n), lambda l: (l, 0, 0, 0))],
        out_specs=pl.BlockSpec((1, w, 2 * w), lambda l: (l, 0, 0)),
        out_shape=jax.ShapeDtypeStruct((depth, w, 2 * w), BF16),
        compiler_params=_params(),
        name="fold_fft_map",
    )(cs, fw)


def kernel(x, c, ctx, c_ctx, norm_g, ada_w, ada_b, w_in, conv_w, conv_b, lru_wa, lru_ba, lru_wx, lru_bx, lru_lam,
           fft_w, pool_w, pool_scale, proj_a, proj_b, proj_c, w_out, final_g):
    bsz, t_lat, d = x.shape
    t_ctx = ctx.shape[1]
    depth = w_in.shape[0]
    w_lru = conv_w.shape[-1]
    fft_groups, fft_dim = fft_w.shape[1], fft_w.shape[2]
    w_fft = fft_groups * fft_dim
    w_pool = pool_w.shape[1] * pool_w.shape[2]
    heads, head_dim = lru_wa.shape[2], lru_wa.shape[3]
    assert bsz == SUBLANES and fft_dim == LANES and pool_w.shape[2] == LANES and conv_w.shape[1] == CONV_TAPS
    assert t_lat % DFT_RADIX == 0 and t_ctx % DFT_RADIX == 0 and (t_lat // DFT_RADIX) % (t_ctx // DFT_RADIX) == 0
    n_lat_rows, n_ctx_rows = t_lat * bsz, t_ctx * bsz
    n_all = n_lat_rows + n_ctx_rows
    n_lat_tiles = n_lat_rows // TOKEN_TILE_ROWS
    lru_steps = min(LRU_CHUNK_STEPS, t_ctx)
    heads_per_group = LRU_GATE_WIDTH // head_dim

    assert TOKEN_TILE_ROWS // bsz == GRID_WIDTH
    code = _coordinate_code(max(t_lat // GRID_WIDTH, GRID_WIDTH), d)
    cc =jnp.concatenate([c, jnp.broadcast_to(c_ctx[None, :], (bsz, d))], axis=0)
    mod = _modulation(cc, ada_w, ada_b)
    cdw = _fold_channel_map(_channel_dft_matrix(fft_dim), fft_w)
    n2_lat, n2_ctx = t_lat // DFT_RADIX, t_ctx // DFT_RADIX

    def gate_w(w):
        wd = w.reshape(2 * heads // heads_per_group, heads_per_group, head_dim, head_dim)
        return jax.vmap(_block_diag)(wd).reshape(2, heads // heads_per_group, LRU_GATE_WIDTH, LRU_GATE_WIDTH)

    tokens = (x, _batch_major_to_rows(ctx), code)
    out = None
    for l in range(depth):
        first, last = l == 0, l == depth - 1
        mod_k = mod[l].reshape(2, bsz, 3, d).transpose(0, 2, 1, 3)
        g = norm_g[l].reshape(1, d)
        k1_out = _k1(tokens, mod_k, g, w_in, l, cdw, conv_w[l], conv_b[l], first, n_all, n_lat_tiles,
                     w_lru, w_fft, w_pool)
        xh, v4, u_pool, h = k1_out[:4]
        x_rows = k1_out[4] if first else tokens[0]

        wg = jnp.concatenate([gate_w(lru_wa[l]), gate_w(lru_wx[l])], axis=-1).astype(BF16)
        h_lru, wz = _lru(xh, wg, lru_ba[l], lru_bx[l], lru_lam[l], w_in, l,
                         _merge_weight_segments(w_lru, w_fft, w_pool, d), n_lat_rows, n_ctx_rows, lru_steps)

        y_lat = _position_dft(v4, n2_lat, 0)
        y_ctx = None if last else _position_dft(v4, n2_ctx, n2_lat // n2_ctx)

        out = _k3(x_rows, h, mod_k, h_lru, y_lat, y_ctx, u_pool, wz,
                  proj_a[l].astype(BF16), proj_b[l].astype(BF16), proj_c[l].astype(BF16), w_out[l].astype(BF16),
                  _block_diag(pool_w[l]).astype(BF16),
                  pool_scale[l].reshape(1, w_pool), final_g.reshape(1, d), last, n_lat_tiles, t_lat, t_ctx)
        tokens = (out,)
    return out
```

```python
import functools

import numpy as np
import jax
import jax.numpy as jnp
from jax import lax
from jax.experimental import pallas as pl
from jax.experimental.pallas import tpu as pltpu

F32 = jnp.float32
BF16 = jnp.bfloat16

GRID_WIDTH = 64
LRU_POWER = 8.0
NORM_EPS = 1e-6
POSITION_BASE = 10000.0
POOL_WINDOW_SIZES = (2, 4, 8, 16)
CONV_TAPS = 4
SUBLANES = 8
LANES = 128
DFT_RADIX = 64
DFT_GROUP_ROWS = 2 * SUBLANES
DFT_STAGE1_ROWS = 4096
DFT_STAGE2_GROUPS = 2
DFT_LANE_BLOCK = 256
POOL_HALO_ROWS = 64
TOKEN_TILE_ROWS = DFT_RADIX * SUBLANES
LRU_CHUNK_STEPS = 256
LRU_GATE_WIDTH = 256
SCAN_UNROLL = 8
VMEM_LIMIT_BYTES = 56 * 1024 * 1024


def _half_tanh_sigmoid(x_half):
    return 0.5 * (1.0 + jnp.tanh(x_half))


def _sigmoid(x):
    return _half_tanh_sigmoid(0.5 * x)


def _silu(x):
    return x * _sigmoid(x)


def _params(n_axes=1):
    return pltpu.CompilerParams(dimension_semantics=("arbitrary",) * n_axes, vmem_limit_bytes=VMEM_LIMIT_BYTES)


def _resident(shape, index_map):
    return pl.BlockSpec(shape, index_map, pipeline_mode=pl.Buffered(1))


def _mod_body(cc_ref, w_ref, b_ref, o_ref):
    s = _silu(cc_ref[...]).astype(BF16)
    o_ref[0] = jnp.dot(s, w_ref[0].astype(BF16), preferred_element_type=F32) + b_ref[0]


def _modulation(cc, ada_w, ada_b):
    depth, d, n = ada_w.shape
    tn = 1024
    return pl.pallas_call(
        _mod_body,
        grid=(depth, n // tn),
        in_specs=[
            pl.BlockSpec((2 * SUBLANES, d), lambda l, i: (0, 0)),
            pl.BlockSpec((1, d, tn), lambda l, i: (l, 0, i)),
            pl.BlockSpec((1, 1, tn), lambda l, i: (l, 0, i)),
        ],
        out_specs=pl.BlockSpec((1, 2 * SUBLANES, tn), lambda l, i: (l, 0, i)),
        out_shape=jax.ShapeDtypeStruct((depth, 2 * SUBLANES, n), F32),
        compiler_params=_params(2),
        name="adaln_mod",
    )(cc, ada_w, ada_b.reshape(depth, 1, n))


def _rows_body(x_ref, o_ref):
    bsz, tt, d = x_ref.shape
    o_ref[...] = jnp.transpose(x_ref[...], (1, 0, 2)).reshape(tt * bsz, d)


def _batch_major_to_rows(x):
    bsz, t, d = x.shape
    tt = TOKEN_TILE_ROWS // bsz
    return pl.pallas_call(
        _rows_body,
        grid=(t // tt,),
        in_specs=[pl.BlockSpec((bsz, tt, d), lambda j: (0, j, 0))],
        out_specs=pl.BlockSpec((tt * bsz, d), lambda j: (j, 0)),
        out_shape=jax.ShapeDtypeStruct((t * bsz, d), x.dtype),
        compiler_params=_params(),
        name="ctx_rows",
    )(x)


def _seq_edges(p, n_lat_tiles, n_tiles):
    is_first = jnp.logical_or(p == 0, p == n_lat_tiles)
    is_last = jnp.logical_or(p == n_lat_tiles - 1, p == n_tiles - 1)
    return is_first, is_last


def _k1_body(first, n_lat_tiles, n_tiles, *refs):
    n_tok = 3 if first else 1
    tok_refs = refs[:n_tok]
    refs = list(refs[n_tok:])
    mod_ref, g_ref, wl_ref, wf_ref, wp_ref, cd_ref, cw_ref, cb_ref, xh_ref, v_ref, upool_ref, h_ref = refs[:12]
    refs = refs[12:]
    x0_ref = refs.pop(0) if first else None
    uprev, tail, wl_s, wf_s, wp_s = refs
    j = pl.program_id(0)
    rows, d = h_ref.shape
    bsz = SUBLANES
    tt = rows // bsz
    tail_rows = tail.shape[0]

    @pl.when(j == 0)
    def _():
        uprev[...] = jnp.zeros_like(uprev)
        tail[...] = jnp.zeros_like(tail)
        wl_s[...] = wl_ref[0].astype(BF16)
        wf_s[...] = wf_ref[0].astype(BF16)
        wp_s[...] = wp_ref[0].astype(BF16)

    if first:
        x_ref, ctx_ref, code_ref = tok_refs
        grid_row = jnp.minimum(j, n_lat_tiles - 1)
        row_code = jnp.broadcast_to(code_ref[pl.ds(grid_row, 1), :], (tt, d // 2))
        pos = jnp.concatenate([row_code, code_ref[0:tt, :]], axis=-1)
        x3 = jnp.where(j < n_lat_tiles, jnp.transpose(x_ref[...] + pos[None], (1, 0, 2)),
                       ctx_ref[...].reshape(tt, bsz, d))
        x0_ref[...] = x3.reshape(rows, d)
    else:
        x3 = tok_refs[0][...].reshape(tt, bsz, d)
    ms = jnp.mean(x3 * x3, axis=-1, keepdims=True)
    gain = g_ref[...] * (1.0 + mod_ref[0, 1])
    h = (x3 * lax.rsqrt(ms + NORM_EPS) * gain + mod_ref[0, 0]).reshape(rows, d).astype(BF16)
    h_ref[...] = h
    u = jnp.dot(h, wl_s[...], preferred_element_type=F32)
    ufft = jnp.dot(h, wf_s[...], preferred_element_type=F32).astype(BF16)
    v = jnp.dot(ufft, cd_ref[0], preferred_element_type=F32)
    w = v.shape[1] // 2
    v4 = jnp.stack([v[:, :w].reshape(tt, bsz, w), v[:, w:].reshape(tt, bsz, w)], axis=1)
    v_ref[...] = v4.reshape(1, tt, DFT_GROUP_ROWS, w).astype(BF16)
    upool_ref[...] = jnp.dot(h, wp_s[...], preferred_element_type=F32).astype(BF16)

    p_first, p_last = _seq_edges(j - 1, n_lat_tiles, n_tiles)
    up = uprev[...]
    before = jnp.where(p_first, 0.0, tail[...])
    after = jnp.where(p_last, 0.0, u[0:SUBLANES, :])
    ext = jnp.concatenate([before, up, after], axis=0)
    cwh = 0.5 * cw_ref[...]
    xh = 0.5 * cb_ref[...] + cwh[0:1] * ext[0:rows]
    for k in range(1, CONV_TAPS):
        xh = xh + cwh[k:k + 1] * ext[k * SUBLANES:k * SUBLANES + rows]
    xh_ref[...] = xh.astype(BF16)

    tail[...] = up[rows - tail_rows:rows, :]
    uprev[...] = u


def _k1(tokens, mod_k, g, w_all, layer, cd, conv_w, conv_b, first, n_all, n_lat_tiles, w_lru, w_fft, w_pool):
    d = g.shape[-1]
    bsz = SUBLANES
    r = TOKEN_TILE_ROWS
    tt = r // bsz
    n_tiles = n_all // r
    n_ctx_tiles = n_tiles - n_lat_tiles
    const2 = lambda j: (0, 0)
    off_fft, off_pool = 2 * w_lru, 2 * w_lru + 2 * w_fft
    assert off_fft % w_fft == 0 and off_pool % w_pool == 0
    clamp = lambda j: (jnp.minimum(j, n_tiles - 1), 0)

    if first:
        tok_specs = [
            pl.BlockSpec((bsz, tt, d), lambda j: (0, jnp.minimum(j, n_lat_tiles - 1), 0)),
            pl.BlockSpec((r, d), lambda j: (jnp.clip(j - n_lat_tiles, 0, n_ctx_tiles - 1), 0)),
            pl.BlockSpec(tokens[2].shape, const2),
        ]
    else:
        tok_specs = [pl.BlockSpec((r, d), clamp)]
    in_specs = tok_specs + [
        pl.BlockSpec((1, 3, bsz, d), lambda j: (jnp.where(j < n_lat_tiles, 0, 1), 0, 0, 0)),
        pl.BlockSpec((1, d), const2),
        _resident((1, d, w_lru), lambda j: (layer, 0, 0)),
        _resident((1, d, w_fft), lambda j: (layer, 0, off_fft // w_fft)),
        _resident((1, d, w_pool), lambda j: (layer, 0, off_pool // w_pool)),
        _resident((1,) + cd.shape[1:], lambda j: (layer, 0, 0)),
        pl.BlockSpec((CONV_TAPS, w_lru), const2),
        pl.BlockSpec((1, w_lru), const2),
    ]
    out_specs = [
        pl.BlockSpec((r, w_lru), lambda j: (jnp.maximum(j - 1, 0), 0)),
        pl.BlockSpec((1, tt, DFT_GROUP_ROWS, w_fft), lambda j: (jnp.minimum(j, n_tiles - 1), 0, 0, 0)),
        pl.BlockSpec((r, w_pool), clamp),
        pl.BlockSpec((r, d), clamp),
    ]
    out_shape = [
        jax.ShapeDtypeStruct((n_all, w_lru), BF16),
        jax.ShapeDtypeStruct((n_tiles, tt, DFT_GROUP_ROWS, w_fft), BF16),
        jax.ShapeDtypeStruct((n_all, w_pool), BF16),
        jax.ShapeDtypeStruct((n_all, d), BF16),
    ]
    if first:
        out_specs.append(pl.BlockSpec((r, d), clamp))
        out_shape.append(jax.ShapeDtypeStruct((n_all, d), F32))
    return pl.pallas_call(
        functools.partial(_k1_body, first, n_lat_tiles, n_tiles),
        grid=(n_tiles + 1,),
        in_specs=in_specs,
        out_specs=out_specs,
        out_shape=out_shape,
        scratch_shapes=[pltpu.VMEM((r, w_lru), F32), pltpu.VMEM(((CONV_TAPS - 2) * SUBLANES, w_lru), F32),
                        pltpu.VMEM((d, w_lru), BF16), pltpu.VMEM((d, w_fft), BF16), pltpu.VMEM((d, w_pool), BF16)],
        compiler_params=_params(),
        name="norm_uproj",
    )(*tokens, mod_k, g, w_all, w_all, w_all, cd, conv_w, conv_b.reshape(1, w_lru))


def _merge_weight_segments(w_lru, w_fft, w_pool, d):
    return [(2 * w_lru + 2 * w_fft + 2 * w_pool, 3 * d), (w_lru, w_lru), (2 * w_lru + w_fft, w_fft),
            (2 * w_lru + 2 * w_fft + w_pool, w_pool)]


def _lru_chunk(d, j, n_lat_chunks, n_ctx_chunks):
    in_ctx = j < n_ctx_chunks
    jj = j - n_ctx_chunks
    ctx_ck = n_lat_chunks + jnp.where(d == 0, j, n_ctx_chunks - 1 - j)
    lat_ck = jnp.where(d == 0, jj, n_lat_chunks - 1 - jj)
    return jnp.where(in_ctx, ctx_ck, lat_ck)


def _lru_body(steps, xh_ref, wg_ref, ba_ref, bx_ref, lam_ref, ws_ref, o_ref, wz_ref, a_sc, b_sc, h_sc, state):
    d = pl.program_id(0)
    j = pl.program_id(1)
    cb = xh_ref.shape[1]
    wz_ref[...] = ws_ref[0].astype(BF16)

    @pl.when(j == 0)
    def _():
        state[...] = jnp.zeros_like(state)

    lam = lam_ref[0]
    half_rate = (-0.5 * LRU_POWER) * (jnp.maximum(-lam, 0.0) + jnp.log1p(jnp.exp(-jnp.abs(lam))))
    half_ba = 0.5 * ba_ref[0]
    half_bx = 0.5 * bx_ref[0]
    gw = wg_ref.shape[2]
    for g in range(cb // gw):
        sl = slice(g * gw, (g + 1) * gw)
        xb = xh_ref[:, sl]
        gates = jnp.dot(xb, wg_ref[0, g], preferred_element_type=F32)
        log_a = (1.0 + jnp.tanh(gates[:, :gw] + half_ba[:, sl])) * half_rate[:, sl]
        a = jnp.exp(log_a)
        q = jnp.tanh(log_a) * (-1.0 - a * a)
        gain = jnp.where(q > 0.0, q * lax.rsqrt(q), 0.0)
        a_sc[:, sl] = a
        b_sc[:, sl] = gain * ((1.0 + jnp.tanh(gates[:, gw:] + half_bx[:, sl])) * xb.astype(F32))

    n_blocks = steps // SCAN_UNROLL
    block_rows = SCAN_UNROLL * SUBLANES

    def scan(reverse):
        def block(i, h):
            base = pl.multiple_of((n_blocks - 1 - i if reverse else i) * block_rows, block_rows)
            for k in (range(SCAN_UNROLL - 1, -1, -1) if reverse else range(SCAN_UNROLL)):
                r0 = base + k * SUBLANES
                h = a_sc[pl.ds(r0, SUBLANES), :] * h + b_sc[pl.ds(r0, SUBLANES), :]
                h_sc[pl.ds(r0, SUBLANES), :] = h
            return h

        state[...] = lax.fori_loop(0, n_blocks, block, state[...])

    @pl.when(d == 0)
    def _():
        scan(False)

    @pl.when(d == 1)
    def _():
        scan(True)

    o_ref[0] = h_sc[...].astype(BF16)


def _lru(xh, wg, ba, bx, lam, w_in, layer, segs, n_lat_rows, n_ctx_rows, steps):
    n_all, w = xh.shape
    d_model = w_in.shape[1]
    rc = steps * SUBLANES
    n_lat_chunks = n_lat_rows // rc
    n_ctx_chunks = n_ctx_rows // rc
    n_chunks = n_lat_chunks + n_ctx_chunks
    gw = wg.shape[2]
    ck = functools.partial(_lru_chunk, n_lat_chunks=n_lat_chunks, n_ctx_chunks=n_ctx_chunks)

    total = sum(width for _, width in segs)
    slab_w = next(c for c in (LANES, 2 * LANES, 4 * LANES, 8 * LANES) if total // c <= 2 * n_chunks)
    assert all(off % slab_w == 0 and width % slab_w == 0 for off, width in segs)
    n_slabs = total // slab_w
    bounds = np.cumsum([0] + [width // slab_w for _, width in segs])
    slab = lambda d, j: jnp.minimum(d * n_chunks + j, n_slabs - 1)

    def slab_source(d, j):
        s = slab(d, j)
        blk = segs[-1][0] // slab_w + (s - int(bounds[-2]))
        for k in range(len(segs) - 2, -1, -1):
            blk = jnp.where(s < int(bounds[k + 1]), segs[k][0] // slab_w + (s - int(bounds[k])), blk)
        return blk

    return pl.pallas_call(
        functools.partial(_lru_body, steps),
        grid=(2, n_chunks),
        in_specs=[
            pl.BlockSpec((rc, w), lambda d, j: (ck(d, j), 0)),
            pl.BlockSpec((1, w // gw, gw, 2 * gw), lambda d, j: (d, 0, 0, 0)),
            pl.BlockSpec((1, 1, w), lambda d, j: (d, 0, 0)),
            pl.BlockSpec((1, 1, w), lambda d, j: (d, 0, 0)),
            pl.BlockSpec((1, 1, w), lambda d, j: (d, 0, 0)),
            pl.BlockSpec((1, d_model, slab_w), lambda d, j: (layer, 0, slab_source(d, j))),
        ],
        out_specs=[pl.BlockSpec((1, rc, w), lambda d, j: (d, ck(d, j), 0)),
                   pl.BlockSpec((d_model, slab_w), lambda d, j: (0, slab(d, j)))],
        out_shape=[jax.ShapeDtypeStruct((2, n_all, w), BF16), jax.ShapeDtypeStruct((d_model, total), BF16)],
        scratch_shapes=[
            pltpu.VMEM((rc, w), F32),
            pltpu.VMEM((rc, w), F32),
            pltpu.VMEM((rc, w), F32),
            pltpu.VMEM((SUBLANES, w), F32),
        ],
        compiler_params=_params(2),
        name="rglru_scan",
    )(xh, wg, ba.reshape(2, 1, w), bx.reshape(2, 1, w), lam.reshape(2, 1, w), w_in)


def _dft_constants(n2):
    n1 = DFT_RADIX
    n = n1 * n2
    eye = np.eye(SUBLANES)
    k2 = np.arange(n2)
    ang1 = 2.0 * np.pi * ((k2[:, None] * k2[None, :]) % n2) / n2
    c1, s1 = np.cos(ang1) / np.sqrt(n2), np.sin(ang1) / np.sqrt(n2)
    base1 = np.stack([np.stack([c1, s1], axis=-1), np.stack([-s1, c1], axis=-1)], axis=1)
    m1 = jnp.asarray(np.kron(base1.reshape(2 * n2, 2 * n2), eye), F32)
    k1 = np.arange(n1)
    ang2 = 2.0 * np.pi * ((k1[:, None] * k1[None, :]) % n1) / n1
    base2 = np.stack([np.cos(ang2), np.sin(ang2)], axis=-1) / np.sqrt(n1)
    m2 = jnp.asarray(np.kron(base2.reshape(n1, 2 * n1), eye), F32)
    angt = 2.0 * np.pi * ((k1[:, None] * k2[None, :]) % n) / n
    shape = (n1, n2, SUBLANES, LANES)

    def table(vals):
        return jnp.asarray(np.ascontiguousarray(np.broadcast_to(vals[:, :, None, None], shape)), F32)

    return m1.astype(BF16), m2.astype(BF16), table(np.cos(angt)), table(np.sin(angt))


def _dft_body(steps1, v_ref, m1_ref, twc_ref, tws_ref, m2_ref, y_ref, zs):
    s = pl.program_id(1)
    n2, a_blk, grp, w = v_ref.shape
    n1 = zs.shape[1]
    pair = grp // SUBLANES

    @pl.when(s < steps1)
    def _():
        reps = w // LANES
        for i in range(a_blk):
            xs = v_ref[:, i].reshape(n2 * grp, w)
            r = jnp.dot(m1_ref[...], xs, preferred_element_type=F32).reshape(n2, 2, SUBLANES, w)
            rr, ri = r[:, 0], r[:, 1]
            cw = jnp.concatenate([twc_ref[i]] * reps, axis=-1)
            sw = jnp.concatenate([tws_ref[i]] * reps, axis=-1)
            z = jnp.stack([rr * cw + ri * sw, ri * cw - rr * sw], axis=1)
            zs[:, s * a_blk + i] = z.reshape(n2, grp, w).astype(BF16)

    @pl.when(s >= steps1)
    def _():
        n_out = y_ref.shape[1] * pair
        c0 = (s - steps1) * n_out
        ys = [jnp.dot(m2_ref[...], zs[c0 + i].reshape(n1 * grp, w), preferred_element_type=F32)
              .reshape(n1, SUBLANES, w) for i in range(n_out)]
        y_ref[...] = jnp.stack(ys, axis=1).reshape(n1, n_out // pair, grp, w).astype(BF16)


def _position_dft(v4, n2, block0):
    m1, m2, twc, tws = _dft_constants(n2)
    n1 = DFT_RADIX
    grp, w = v4.shape[2], v4.shape[3]
    wh = min(w, DFT_LANE_BLOCK)
    a_blk = min(n1, max(1, DFT_STAGE1_ROWS // (n2 * grp)))
    steps1 = n1 // a_blk
    pair = grp // SUBLANES
    groups2 = min(DFT_STAGE2_GROUPS, n2 // pair)
    stage1_step = lambda s: jnp.minimum(s, steps1 - 1)
    return pl.pallas_call(
        functools.partial(_dft_body, steps1),
        grid=(w // wh, steps1 + n2 // (pair * groups2)),
        in_specs=[
            pl.BlockSpec((n2, a_blk, grp, wh), lambda c, s: (block0, stage1_step(s), 0, c)),
            _resident(m1.shape, lambda c, s: (0, 0)),
            pl.BlockSpec((a_blk, n2, SUBLANES, LANES), lambda c, s: (stage1_step(s), 0, 0, 0)),
            pl.BlockSpec((a_blk, n2, SUBLANES, LANES), lambda c, s: (stage1_step(s), 0, 0, 0)),
            _resident(m2.shape, lambda c, s: (0, 0)),
        ],
        out_specs=pl.BlockSpec((n1, groups2, grp, wh), lambda c, s: (0, jnp.maximum(s - steps1, 0), 0, c)),
        out_shape=jax.ShapeDtypeStruct((n1, n2 // pair, grp, w), BF16),
        scratch_shapes=[pltpu.VMEM((n2, n1, grp, wh), BF16)],
        compiler_params=_params(2),
        name="position_dft",
    )(v4, m1, twc, tws, m2)


def _pool_minus_token(ue, rows, t0, t_seq):
    halo = POOL_HALO_ROWS
    tvec = t0 + lax.broadcasted_iota(jnp.int32, (rows, LANES), 0) // SUBLANES
    parts = []
    for g, win in enumerate(POOL_WINDOW_SIZES):
        col = ue[:, g * LANES:(g + 1) * LANES]
        n = col.shape[0]
        acc = col[0:n - SUBLANES] + col[SUBLANES:n]
        e0 = SUBLANES
        span = 1
        while 2 * span < win:
            sh = span * SUBLANES
            n = acc.shape[0]
            acc = acc[0:n - 2 * sh] + acc[2 * sh:n]
            e0 += sh
            span *= 2
        wsum = acc[halo - e0:halo - e0 + rows]
        half = win // 2
        cnt = jnp.minimum(tvec + half, t_seq) - jnp.maximum(tvec - half, 0)
        parts.append(wsum / cnt.astype(F32) - col[halo:halo + rows])
    return jnp.concatenate(parts, axis=1)


def _k3_body(cfg, *refs):
    last, n_lat_tiles, n_tiles, t_lat, t_ctx = cfg
    refs = list(refs)
    x_ref, h_ref, mod_ref, hf_ref, hb_ref, yl_ref = refs[:6]
    refs = refs[6:]
    yc_ref = None if last else refs.pop(0)
    up_ref, upp_ref, upn_ref, wz_ref, pa_ref, pb_ref, pc_ref, wo_ref, pw_ref, ps_ref, fg_ref, o_ref = refs
    j = pl.program_id(0)
    is_lat = j < n_lat_tiles
    rows, d = x_ref.shape
    bsz = SUBLANES
    tt = rows // bsz
    h = h_ref[...]

    seq_first, seq_last = _seq_edges(j, n_lat_tiles, n_tiles)
    ue = jnp.concatenate([
        upp_ref[...].astype(F32) * jnp.where(seq_first, 0.0, 1.0).astype(F32),
        up_ref[...].astype(F32),
        upn_ref[...].astype(F32) * jnp.where(seq_last, 0.0, 1.0).astype(F32)], axis=0)
    t0 = jnp.where(is_lat, j, j - n_lat_tiles) * tt
    t_seq = jnp.where(is_lat, t_lat, t_ctx)
    p = _pool_minus_token(ue, rows, t0, t_seq).astype(BF16)

    w = yl_ref.shape[-1]
    z_all = jnp.dot(h, wz_ref[...], preferred_element_type=F32)
    c0 = 3 * d
    w_lru = hf_ref.shape[-1]
    z_lru, z_fft, z_pool = z_all[:, c0:c0 + w_lru], z_all[:, c0 + w_lru:c0 + w_lru + w], z_all[:, c0 + w_lru + w:]

    y_lru = hf_ref[0].astype(F32) + hb_ref[0].astype(F32)
    ya = jnp.dot((y_lru * _silu(z_lru)).astype(BF16), pa_ref[...], preferred_element_type=F32)

    y_fft = yl_ref[...].reshape(rows, w)
    if not last:
        y_fft = jnp.where(is_lat, y_fft, yc_ref[...].reshape(rows, w))
    yb = jnp.dot((y_fft.astype(F32) * _silu(z_fft)).astype(BF16), pb_ref[...], preferred_element_type=F32)

    y_pool = jnp.dot(p, pw_ref[...], preferred_element_type=F32) * ps_ref[...]
    yc = jnp.dot((y_pool * _silu(z_pool)).astype(BF16), pc_ref[...], preferred_element_type=F32)

    gates = _sigmoid(z_all[:, :c0])
    m = gates[:, :d] * ya + gates[:, d:2 * d] * yb + gates[:, 2 * d:] * yc
    out = jnp.dot(m.astype(BF16), wo_ref[...], preferred_element_type=F32)
    xn = x_ref[...].reshape(tt, bsz, d) + mod_ref[0, 2] * out.reshape(tt, bsz, d)
    if last:
        ms = jnp.mean(xn * xn, axis=-1, keepdims=True)
        xn = xn * lax.rsqrt(ms + NORM_EPS) * fg_ref[...]
        o_ref[...] = jnp.transpose(xn, (1, 0, 2))
    else:
        o_ref[...] = xn.reshape(rows, d)


def _k3(x_rows, h, mod_k, h_lru, y_lat, y_ctx, u_pool, wz, pa, pb, pc, wo, pw, ps, fg, last,
        n_lat_tiles, t_lat, t_ctx):
    n_all, d = h.shape
    bsz = SUBLANES
    r = TOKEN_TILE_ROWS
    tt = r // bsz
    n_tiles = n_all // r
    grid_tiles = n_lat_tiles if last else n_tiles
    w_lru = h_lru.shape[2]
    w_fft = y_lat.shape[-1]
    w_pool = u_pool.shape[1]
    hr = r // POOL_HALO_ROWS
    n_halo = n_all // POOL_HALO_ROWS
    cfg = (last, n_lat_tiles, n_tiles, t_lat, t_ctx)
    const2 = lambda j: (0, 0)
    assert [width for _, width in _merge_weight_segments(w_lru, w_fft, w_pool, d)] == [3 * d, w_lru, w_fft, w_pool]

    def dft_spec(y4, first_tile):
        n2 = y4.shape[1] * y4.shape[2] // bsz
        assert tt % n2 == 0
        blk = (tt // n2,) + y4.shape[1:]
        n_blk = y4.shape[0] // blk[0]
        return pl.BlockSpec(blk, lambda j: (jnp.clip(j - first_tile, 0, n_blk - 1), 0, 0, 0))

    row_spec = lambda width: pl.BlockSpec((r, width), lambda j: (j, 0))
    pair_spec = lambda k: pl.BlockSpec((1, r, w_lru), lambda j: (k, j, 0))
    in_specs = [
        row_spec(d), row_spec(d),
        pl.BlockSpec((1, 3, bsz, d), lambda j: (jnp.where(j < n_lat_tiles, 0, 1), 0, 0, 0)),
        pair_spec(0), pair_spec(1),
        dft_spec(y_lat, 0),
    ]
    args = [x_rows, h, mod_k, h_lru, h_lru, y_lat]
    if not last:
        in_specs.append(dft_spec(y_ctx, n_lat_tiles))
        args.append(y_ctx)
    in_specs += [
        row_spec(w_pool),
        pl.BlockSpec((POOL_HALO_ROWS, w_pool), lambda j: (jnp.maximum(j * hr - 1, 0), 0)),
        pl.BlockSpec((POOL_HALO_ROWS, w_pool), lambda j: (jnp.minimum((j + 1) * hr, n_halo - 1), 0)),
        _resident(wz.shape, const2),
    ]
    args += [u_pool, u_pool, u_pool, wz]
    for wgt in (pa, pb, pc, wo, pw, ps, fg):
        in_specs.append(_resident(wgt.shape, const2))
        args.append(wgt)
    if last:
        out_spec = pl.BlockSpec((bsz, tt, d), lambda j: (0, j, 0))
        out_shape = jax.ShapeDtypeStruct((bsz, t_lat, d), F32)
    else:
        out_spec = row_spec(d)
        out_shape = jax.ShapeDtypeStruct((n_all, d), F32)
    return pl.pallas_call(
        functools.partial(_k3_body, cfg),
        grid=(grid_tiles,),
        in_specs=in_specs,
        out_specs=out_spec,
        out_shape=out_shape,
        compiler_params=_params(),
        name="merge_residual",
    )(*args)


def _coordinate_code(n, d):
    quarter = d // 4
    omega = 1.0 / (POSITION_BASE ** (jnp.arange(quarter, dtype=F32) / quarter))
    ang = jnp.arange(n).astype(F32)[:, None] * omega[None, :]
    return jnp.concatenate([jnp.sin(ang), jnp.cos(ang)], axis=-1).astype(F32)


def _block_diag(w):
    g, n, _ = w.shape
    eye = jnp.eye(g, dtype=w.dtype)
    return jnp.einsum("gij,gk->gikj", w, eye).reshape(g * n, g * n)


def _channel_dft_matrix(n):
    k = np.arange(n)
    ang = 2.0 * np.pi * ((k[:, None] * k[None, :]) % n) / n
    return jnp.asarray(np.concatenate([np.cos(ang), -np.sin(ang)], axis=1) / np.sqrt(n), F32)


def _fold_body(cs_ref, fw_ref, o_ref):
    _, groups, n, _ = fw_ref.shape
    o_ref[...] = jnp.zeros_like(o_ref)
    for g in range(groups):
        for part in range(2):
            blk = jnp.dot(cs_ref[:, part * n:(part + 1) * n], fw_ref[0, g], preferred_element_type=F32,
                          precision=lax.Precision.HIGHEST)
            col = part * groups * n + g * n
            o_ref[0, g * n:(g + 1) * n, col:col + n] = blk.astype(BF16)


def _fold_channel_map(cs, fw):
    depth, groups, n, _ = fw.shape
    w = groups * n
    return pl.pallas_call(
        _fold_body,
        grid=(depth,),
        in_specs=[pl.BlockSpec(cs.shape, lambda l: (0, 0)), pl.BlockSpec((1, groups, n, n), lambda l: (l, 0, 0, 0))],
        out_specs=pl.BlockSpec((1, w, 2 * w), lambda l: (l, 0, 0)),
        out_shape=jax.ShapeDtypeStruct((depth, w, 2 * w), BF16),
        compiler_params=_params(),
        name="fold_fft_map",
    )(cs, fw)


def kernel(x, c, ctx, c_ctx, norm_g, ada_w, ada_b, w_in, conv_w, conv_b, lru_wa, lru_ba, lru_wx, lru_bx, lru_lam,
           fft_w, pool_w, pool_scale, proj_a, proj_b, proj_c, w_out, final_g):
    bsz, t_lat, d = x.shape
    t_ctx = ctx.shape[1]
    depth = w_in.shape[0]
    w_lru = conv_w.shape[-1]
    fft_groups, fft_dim = fft_w.shape[1], fft_w.shape[2]
    w_fft = fft_groups * fft_dim
    w_pool = pool_w.shape[1] * pool_w.shape[2]
    heads, head_dim = lru_wa.shape[2], lru_wa.shape[3]
    assert bsz == SUBLANES and fft_dim == LANES and pool_w.shape[2] == LANES and conv_w.shape[1] == CONV_TAPS
    assert t_lat % DFT_RADIX == 0 and t_ctx % DFT_RADIX == 0 and (t_lat // DFT_RADIX) % (t_ctx // DFT_RADIX) == 0
    n_lat_rows, n_ctx_rows = t_lat * bsz, t_ctx * bsz
    n_all = n_lat_rows + n_ctx_rows
    n_lat_tiles = n_lat_rows // TOKEN_TILE_ROWS
    lru_steps = min(LRU_CHUNK_STEPS, t_ctx)
    heads_per_group = LRU_GATE_WIDTH // head_dim

    assert TOKEN_TILE_ROWS // bsz == GRID_WIDTH
    code = _coordinate_code(max(t_lat // GRID_WIDTH, GRID_WIDTH), d)
    cc =jnp.concatenate([c, jnp.broadcast_to(c_ctx[None, :], (bsz, d))], axis=0)
    mod = _modulation(cc, ada_w, ada_b)
    cdw = _fold_channel_map(_channel_dft_matrix(fft_dim), fft_w)
    n2_lat, n2_ctx = t_lat // DFT_RADIX, t_ctx // DFT_RADIX

    def gate_w(w):
        wd = w.reshape(2 * heads // heads_per_group, heads_per_group, head_dim, head_dim)
        return jax.vmap(_block_diag)(wd).reshape(2, heads // heads_per_group, LRU_GATE_WIDTH, LRU_GATE_WIDTH)

    tokens = (x, _batch_major_to_rows(ctx), code)
    out = None
    for l in range(depth):
        first, last = l == 0, l == depth - 1
        mod_k = mod[l].reshape(2, bsz, 3, d).transpose(0, 2, 1, 3)
        g = norm_g[l].reshape(1, d)
        k1_out = _k1(tokens, mod_k, g, w_in, l, cdw, conv_w[l], conv_b[l], first, n_all, n_lat_tiles,
                     w_lru, w_fft, w_pool)
        xh, v4, u_pool, h = k1_out[:4]
        x_rows = k1_out[4] if first else tokens[0]

        wg = jnp.concatenate([gate_w(lru_wa[l]), gate_w(lru_wx[l])], axis=-1).astype(BF16)
        h_lru, wz = _lru(xh, wg, lru_ba[l], lru_bx[l], lru_lam[l], w_in, l,
                         _merge_weight_segments(w_lru, w_fft, w_pool, d), n_lat_rows, n_ctx_rows, lru_steps)

        y_lat = _position_dft(v4, n2_lat, 0)
        y_ctx = None if last else _position_dft(v4, n2_ctx, n2_lat // n2_ctx)

        out = _k3(x_rows, h, mod_k, h_lru, y_lat, y_ctx, u_pool, wz,
                  proj_a[l].astype(BF16), proj_b[l].astype(BF16), proj_c[l].astype(BF16), w_out[l].astype(BF16),
                  _block_diag(pool_w[l]).astype(BF16),
                  pool_scale[l].reshape(1, w_pool), final_g.reshape(1, d), last, n_lat_tiles, t_lat, t_ctx)
        tokens = (out,)
    return out
```

```python
import functools

import numpy as np
import jax
import jax.numpy as jnp
from jax import lax
from jax.experimental import pallas as pl
from jax.experimental.pallas import tpu as pltpu

F32 = jnp.float32
BF16 = jnp.bfloat16

GRID_WIDTH = 64
LRU_POWER = 8.0
NORM_EPS = 1e-6
POSITION_BASE = 10000.0
POOL_WINDOW_SIZES = (2, 4, 8, 16)
CONV_TAPS = 4
SUBLANES = 8
LANES = 128
DFT_RADIX = 64
DFT_GROUP_ROWS = 2 * SUBLANES
DFT_STAGE1_ROWS = 4096
DFT_STAGE2_GROUPS = 2
DFT_LANE_BLOCK = 256
POOL_HALO_ROWS = 64
TOKEN_TILE_ROWS = DFT_RADIX * SUBLANES
LRU_CHUNK_STEPS = 256
LRU_GATE_WIDTH = 256
SCAN_UNROLL = 8
VMEM_LIMIT_BYTES = 56 * 1024 * 1024


def _half_tanh_sigmoid(x_half):
    return 0.5 * (1.0 + jnp.tanh(x_half))


def _sigmoid(x):
    return _half_tanh_sigmoid(0.5 * x)


def _silu(x):
    return x * _sigmoid(x)


def _params(n_axes=1):
    return pltpu.CompilerParams(dimension_semantics=("arbitrary",) * n_axes, vmem_limit_bytes=VMEM_LIMIT_BYTES)


def _resident(shape, index_map):
    return pl.BlockSpec(shape, index_map, pipeline_mode=pl.Buffered(1))


def _mod_body(cc_ref, w_ref, b_ref, o_ref):
    s = _silu(cc_ref[...]).astype(BF16)
    o_ref[0] = jnp.dot(s, w_ref[0].astype(BF16), preferred_element_type=F32) + b_ref[0]


def _modulation(cc, ada_w, ada_b):
    depth, d, n = ada_w.shape
    tn = 1024
    return pl.pallas_call(
        _mod_body,
        grid=(depth, n // tn),
        in_specs=[
            pl.BlockSpec((2 * SUBLANES, d), lambda l, i: (0, 0)),
            pl.BlockSpec((1, d, tn), lambda l, i: (l, 0, i)),
            pl.BlockSpec((1, 1, tn), lambda l, i: (l, 0, i)),
        ],
        out_specs=pl.BlockSpec((1, 2 * SUBLANES, tn), lambda l, i: (l, 0, i)),
        out_shape=jax.ShapeDtypeStruct((depth, 2 * SUBLANES, n), F32),
        compiler_params=_params(2),
        name="adaln_mod",
    )(cc, ada_w, ada_b.reshape(depth, 1, n))


def _rows_body(x_ref, o_ref):
    bsz, tt, d = x_ref.shape
    o_ref[...] = jnp.transpose(x_ref[...], (1, 0, 2)).reshape(tt * bsz, d)


def _batch_major_to_rows(x):
    bsz, t, d = x.shape
    tt = TOKEN_TILE_ROWS // bsz
    return pl.pallas_call(
        _rows_body,
        grid=(t // tt,),
        in_specs=[pl.BlockSpec((bsz, tt, d), lambda j: (0, j, 0))],
        out_specs=pl.BlockSpec((tt * bsz, d), lambda j: (j, 0)),
        out_shape=jax.ShapeDtypeStruct((t * bsz, d), x.dtype),
        compiler_params=_params(),
        name="ctx_rows",
    )(x)


def _seq_edges(p, n_lat_tiles, n_tiles):
    is_first = jnp.logical_or(p == 0, p == n_lat_tiles)
    is_last = jnp.logical_or(p == n_lat_tiles - 1, p == n_tiles - 1)
    return is_first, is_last


def _k1_body(first, n_lat_tiles, n_tiles, *refs):
    n_tok = 3 if first else 1
    tok_refs = refs[:n_tok]
    refs = list(refs[n_tok:])
    mod_ref, g_ref, wl_ref, wf_ref, wp_ref, cd_ref, cw_ref, cb_ref, xh_ref, v_ref, upool_ref, h_ref = refs[:12]
    refs = refs[12:]
    x0_ref = refs.pop(0) if first else None
    uprev, tail, wl_s, wf_s, wp_s = refs
    j = pl.program_id(0)
    rows, d = h_ref.shape
    bsz = SUBLANES
    tt = rows // bsz
    tail_rows = tail.shape[0]

    @pl.when(j == 0)
    def _():
        uprev[...] = jnp.zeros_like(uprev)
        tail[...] = jnp.zeros_like(tail)
        wl_s[...] = wl_ref[0].astype(BF16)
        wf_s[...] = wf_ref[0].astype(BF16)
        wp_s[...] = wp_ref[0].astype(BF16)

    if first:
        x_ref, ctx_ref, code_ref = tok_refs
        grid_row = jnp.minimum(j, n_lat_tiles - 1)
        row_code = jnp.broadcast_to(code_ref[pl.ds(grid_row, 1), :], (tt, d // 2))
        pos = jnp.concatenate([row_code, code_ref[0:tt, :]], axis=-1)
        x3 = jnp.where(j < n_lat_tiles, jnp.transpose(x_ref[...] + pos[None], (1, 0, 2)),
                       ctx_ref[...].reshape(tt, bsz, d))
        x0_ref[...] = x3.reshape(rows, d)
    else:
        x3 = tok_refs[0][...].reshape(tt, bsz, d)
    ms = jnp.mean(x3 * x3, axis=-1, keepdims=True)
    gain = g_ref[...] * (1.0 + mod_ref[0, 1])
    h = (x3 * lax.rsqrt(ms + NORM_EPS) * gain + mod_ref[0, 0]).reshape(rows, d).astype(BF16)
    h_ref[...] = h
    u = jnp.dot(h, wl_s[...], preferred_element_type=F32)
    ufft = jnp.dot(h, wf_s[...], preferred_element_type=F32).astype(BF16)
    v = jnp.dot(ufft, cd_ref[0], preferred_element_type=F32)
    w = v.shape[1] // 2
    v4 = jnp.stack([v[:, :w].reshape(tt, bsz, w), v[:, w:].reshape(tt, bsz, w)], axis=1)
    v_ref[...] = v4.reshape(1, tt, DFT_GROUP_ROWS, w).astype(BF16)
    upool_ref[...] = jnp.dot(h, wp_s[...], preferred_element_type=F32).astype(BF16)

    p_first, p_last = _seq_edges(j - 1, n_lat_tiles, n_tiles)
    up = uprev[...]
    before = jnp.where(p_first, 0.0, tail[...])
    after = jnp.where(p_last, 0.0, u[0:SUBLANES, :])
    ext = jnp.concatenate([before, up, after], axis=0)
    cwh = 0.5 * cw_ref[...]
    xh = 0.5 * cb_ref[...] + cwh[0:1] * ext[0:rows]
    for k in range(1, CONV_TAPS):
        xh = xh + cwh[k:k + 1] * ext[k * SUBLANES:k * SUBLANES + rows]
    xh_ref[...] = xh.astype(BF16)

    tail[...] = up[rows - tail_rows:rows, :]
    uprev[...] = u


def _k1(tokens, mod_k, g, w_all, layer, cd, conv_w, conv_b, first, n_all, n_lat_tiles, w_lru, w_fft, w_pool):
    d = g.shape[-1]
    bsz = SUBLANES
    r = TOKEN_TILE_ROWS
    tt = r // bsz
    n_tiles = n_all // r
    n_ctx_tiles = n_tiles - n_lat_tiles
    const2 = lambda j: (0, 0)
    off_fft, off_pool = 2 * w_lru, 2 * w_lru + 2 * w_fft
    assert off_fft % w_fft == 0 and off_pool % w_pool == 0
    clamp = lambda j: (jnp.minimum(j, n_tiles - 1), 0)

    if first:
        tok_specs = [
            pl.BlockSpec((bsz, tt, d), lambda j: (0, jnp.minimum(j, n_lat_tiles - 1), 0)),
            pl.BlockSpec((r, d), lambda j: (jnp.clip(j - n_lat_tiles, 0, n_ctx_tiles - 1), 0)),
            pl.BlockSpec(tokens[2].shape, const2),
        ]
    else:
        tok_specs = [pl.BlockSpec((r, d), clamp)]
    in_specs = tok_specs + [
        pl.BlockSpec((1, 3, bsz, d), lambda j: (jnp.where(j < n_lat_tiles, 0, 1), 0, 0, 0)),
        pl.BlockSpec((1, d), const2),
        _resident((1, d, w_lru), lambda j: (layer, 0, 0)),
        _resident((1, d, w_fft), lambda j: (layer, 0, off_fft // w_fft)),
        _resident((1, d, w_pool), lambda j: (layer, 0, off_pool // w_pool)),
        _resident((1,) + cd.shape[1:], lambda j: (layer, 0, 0)),
        pl.BlockSpec((CONV_TAPS, w_lru), const2),
        pl.BlockSpec((1, w_lru), const2),
    ]
    out_specs = [
        pl.BlockSpec((r, w_lru), lambda j: (jnp.maximum(j - 1, 0), 0)),
        pl.BlockSpec((1, tt, DFT_GROUP_ROWS, w_fft), lambda j: (jnp.minimum(j, n_tiles - 1), 0, 0, 0)),
        pl.BlockSpec((r, w_pool), clamp),
        pl.BlockSpec((r, d), clamp),
    ]
    out_shape = [
        jax.ShapeDtypeStruct((n_all, w_lru), BF16),
        jax.ShapeDtypeStruct((n_tiles, tt, DFT_GROUP_ROWS, w_fft), BF16),
        jax.ShapeDtypeStruct((n_all, w_pool), BF16),
        jax.ShapeDtypeStruct((n_all, d), BF16),
    ]
    if first:
        out_specs.append(pl.BlockSpec((r, d), clamp))
        out_shape.append(jax.ShapeDtypeStruct((n_all, d), F32))
    return pl.pallas_call(
        functools.partial(_k1_body, first, n_lat_tiles, n_tiles),
        grid=(n_tiles + 1,),
        in_specs=in_specs,
        out_specs=out_specs,
        out_shape=out_shape,
        scratch_shapes=[pltpu.VMEM((r, w_lru), F32), pltpu.VMEM(((CONV_TAPS - 2) * SUBLANES, w_lru), F32),
                        pltpu.VMEM((d, w_lru), BF16), pltpu.VMEM((d, w_fft), BF16), pltpu.VMEM((d, w_pool), BF16)],
        compiler_params=_params(),
        name="norm_uproj",
    )(*tokens, mod_k, g, w_all, w_all, w_all, cd, conv_w, conv_b.reshape(1, w_lru))


def _merge_weight_segments(w_lru, w_fft, w_pool, d):
    return [(2 * w_lru + 2 * w_fft + 2 * w_pool, 3 * d), (w_lru, w_lru), (2 * w_lru + w_fft, w_fft),
            (2 * w_lru + 2 * w_fft + w_pool, w_pool)]


def _lru_chunk(d, j, n_lat_chunks, n_ctx_chunks):
    in_ctx = j < n_ctx_chunks
    jj = j - n_ctx_chunks
    ctx_ck = n_lat_chunks + jnp.where(d == 0, j, n_ctx_chunks - 1 - j)
    lat_ck = jnp.where(d == 0, jj, n_lat_chunks - 1 - jj)
    return jnp.where(in_ctx, ctx_ck, lat_ck)


def _lru_body(steps, xh_ref, wg_ref, ba_ref, bx_ref, lam_ref, ws_ref, o_ref, wz_ref, a_sc, b_sc, h_sc, state):
    d = pl.program_id(0)
    j = pl.program_id(1)
    cb = xh_ref.shape[1]
    wz_ref[...] = ws_ref[0].astype(BF16)

    @pl.when(j == 0)
    def _():
        state[...] = jnp.zeros_like(state)

    lam = lam_ref[0]
    half_rate = (-0.5 * LRU_POWER) * (jnp.maximum(-lam, 0.0) + jnp.log1p(jnp.exp(-jnp.abs(lam))))
    half_ba = 0.5 * ba_ref[0]
    half_bx = 0.5 * bx_ref[0]
    gw = wg_ref.shape[2]
    for g in range(cb // gw):
        sl = slice(g * gw, (g + 1) * gw)
        xb = xh_ref[:, sl]
        gates = jnp.dot(xb, wg_ref[0, g], preferred_element_type=F32)
        log_a = (1.0 + jnp.tanh(gates[:, :gw] + half_ba[:, sl])) * half_rate[:, sl]
        a = jnp.exp(log_a)
        q = jnp.tanh(log_a) * (-1.0 - a * a)
        gain = jnp.where(q > 0.0, q * lax.rsqrt(q), 0.0)
        a_sc[:, sl] = a
        gx = gates[:, gw:].astype(BF16) + half_bx[:, sl].astype(BF16)
        b_sc[:, sl] = gain * ((1.0 + jnp.tanh(gx)) * xb).astype(F32)

    n_blocks = steps // SCAN_UNROLL
    block_rows = SCAN_UNROLL * SUBLANES

    def scan(reverse):
        def block(i, h):
            base = pl.multiple_of((n_blocks - 1 - i if reverse else i) * block_rows, block_rows)
            for k in (range(SCAN_UNROLL - 1, -1, -1) if reverse else range(SCAN_UNROLL)):
                r0 = base + k * SUBLANES
                h = a_sc[pl.ds(r0, SUBLANES), :] * h + b_sc[pl.ds(r0, SUBLANES), :]
                h_sc[pl.ds(r0, SUBLANES), :] = h
            return h

        state[...] = lax.fori_loop(0, n_blocks, block, state[...])

    @pl.when(d == 0)
    def _():
        scan(False)

    @pl.when(d == 1)
    def _():
        scan(True)

    o_ref[0] = h_sc[...].astype(BF16)


def _lru(xh, wg, ba, bx, lam, w_in, layer, segs, n_lat_rows, n_ctx_rows, steps):
    n_all, w = xh.shape
    d_model = w_in.shape[1]
    rc = steps * SUBLANES
    n_lat_chunks = n_lat_rows // rc
    n_ctx_chunks = n_ctx_rows // rc
    n_chunks = n_lat_chunks + n_ctx_chunks
    gw = wg.shape[2]
    ck = functools.partial(_lru_chunk, n_lat_chunks=n_lat_chunks, n_ctx_chunks=n_ctx_chunks)

    total = sum(width for _, width in segs)
    slab_w = next(c for c in (LANES, 2 * LANES, 4 * LANES, 8 * LANES) if total // c <= 2 * n_chunks)
    assert all(off % slab_w == 0 and width % slab_w == 0 for off, width in segs)
    n_slabs = total // slab_w
    bounds = np.cumsum([0] + [width // slab_w for _, width in segs])
    slab = lambda d, j: jnp.minimum(d * n_chunks + j, n_slabs - 1)

    def slab_source(d, j):
        s = slab(d, j)
        blk = segs[-1][0] // slab_w + (s - int(bounds[-2]))
        for k in range(len(segs) - 2, -1, -1):
            blk = jnp.where(s < int(bounds[k + 1]), segs[k][0] // slab_w + (s - int(bounds[k])), blk)
        return blk

    return pl.pallas_call(
        functools.partial(_lru_body, steps),
        grid=(2, n_chunks),
        in_specs=[
            pl.BlockSpec((rc, w), lambda d, j: (ck(d, j), 0)),
            pl.BlockSpec((1, w // gw, gw, 2 * gw), lambda d, j: (d, 0, 0, 0)),
            pl.BlockSpec((1, 1, w), lambda d, j: (d, 0, 0)),
            pl.BlockSpec((1, 1, w), lambda d, j: (d, 0, 0)),
            pl.BlockSpec((1, 1, w), lambda d, j: (d, 0, 0)),
            pl.BlockSpec((1, d_model, slab_w), lambda d, j: (layer, 0, slab_source(d, j))),
        ],
        out_specs=[pl.BlockSpec((1, rc, w), lambda d, j: (d, ck(d, j), 0)),
                   pl.BlockSpec((d_model, slab_w), lambda d, j: (0, slab(d, j)))],
        out_shape=[jax.ShapeDtypeStruct((2, n_all, w), BF16), jax.ShapeDtypeStruct((d_model, total), BF16)],
        scratch_shapes=[
            pltpu.VMEM((rc, w), F32),
            pltpu.VMEM((rc, w), F32),
            pltpu.VMEM((rc, w), F32),
            pltpu.VMEM((SUBLANES, w), F32),
        ],
        compiler_params=_params(2),
        name="rglru_scan",
    )(xh, wg, ba.reshape(2, 1, w), bx.reshape(2, 1, w), lam.reshape(2, 1, w), w_in)


def _dft_constants(n2):
    n1 = DFT_RADIX
    n = n1 * n2
    eye = np.eye(SUBLANES)
    k2 = np.arange(n2)
    ang1 = 2.0 * np.pi * ((k2[:, None] * k2[None, :]) % n2) / n2
    c1, s1 = np.cos(ang1) / np.sqrt(n2), np.sin(ang1) / np.sqrt(n2)
    base1 = np.stack([np.stack([c1, s1], axis=-1), np.stack([-s1, c1], axis=-1)], axis=1)
    m1 = jnp.asarray(np.kron(base1.reshape(2 * n2, 2 * n2), eye), F32)
    k1 = np.arange(n1)
    ang2 = 2.0 * np.pi * ((k1[:, None] * k1[None, :]) % n1) / n1
    base2 = np.stack([np.cos(ang2), np.sin(ang2)], axis=-1) / np.sqrt(n1)
    m2 = jnp.asarray(np.kron(base2.reshape(n1, 2 * n1), eye), F32)
    angt = 2.0 * np.pi * ((k1[:, None] * k2[None, :]) % n) / n
    shape = (n1, n2, SUBLANES, LANES)

    def table(vals):
        return jnp.asarray(np.ascontiguousarray(np.broadcast_to(vals[:, :, None, None], shape)), F32)

    return m1.astype(BF16), m2.astype(BF16), table(np.cos(angt)), table(np.sin(angt))


def _dft_body(steps1, v_ref, m1_ref, twc_ref, tws_ref, m2_ref, y_ref, zs):
    s = pl.program_id(1)
    n2, a_blk, grp, w = v_ref.shape
    n1 = zs.shape[1]
    pair = grp // SUBLANES

    @pl.when(s < steps1)
    def _():
        reps = w // LANES
        for i in range(a_blk):
            xs = v_ref[:, i].reshape(n2 * grp, w)
            r = jnp.dot(m1_ref[...], xs, preferred_element_type=F32).reshape(n2, 2, SUBLANES, w)
            rr, ri = r[:, 0], r[:, 1]
            cw = jnp.concatenate([twc_ref[i]] * reps, axis=-1)
            sw = jnp.concatenate([tws_ref[i]] * reps, axis=-1)
            z = jnp.stack([rr * cw + ri * sw, ri * cw - rr * sw], axis=1)
            zs[:, s * a_blk + i] = z.reshape(n2, grp, w).astype(BF16)

    @pl.when(s >= steps1)
    def _():
        n_out = y_ref.shape[1] * pair
        c0 = (s - steps1) * n_out
        ys = [jnp.dot(m2_ref[...], zs[c0 + i].reshape(n1 * grp, w), preferred_element_type=F32)
              .reshape(n1, SUBLANES, w) for i in range(n_out)]
        y_ref[...] = jnp.stack(ys, axis=1).reshape(n1, n_out // pair, grp, w).astype(BF16)


def _position_dft(v4, n2, block0):
    m1, m2, twc, tws = _dft_constants(n2)
    n1 = DFT_RADIX
    grp, w = v4.shape[2], v4.shape[3]
    wh = min(w, DFT_LANE_BLOCK)
    a_blk = min(n1, max(1, DFT_STAGE1_ROWS // (n2 * grp)))
    steps1 = n1 // a_blk
    pair = grp // SUBLANES
    groups2 = min(DFT_STAGE2_GROUPS, n2 // pair)
    stage1_step = lambda s: jnp.minimum(s, steps1 - 1)
    return pl.pallas_call(
        functools.partial(_dft_body, steps1),
        grid=(w // wh, steps1 + n2 // (pair * groups2)),
        in_specs=[
            pl.BlockSpec((n2, a_blk, grp, wh), lambda c, s: (block0, stage1_step(s), 0, c)),
            _resident(m1.shape, lambda c, s: (0, 0)),
            pl.BlockSpec((a_blk, n2, SUBLANES, LANES), lambda c, s: (stage1_step(s), 0, 0, 0)),
            pl.BlockSpec((a_blk, n2, SUBLANES, LANES), lambda c, s: (stage1_step(s), 0, 0, 0)),
            _resident(m2.shape, lambda c, s: (0, 0)),
        ],
        out_specs=pl.BlockSpec((n1, groups2, grp, wh), lambda c, s: (0, jnp.maximum(s - steps1, 0), 0, c)),
        out_shape=jax.ShapeDtypeStruct((n1, n2 // pair, grp, w), BF16),
        scratch_shapes=[pltpu.VMEM((n2, n1, grp, wh), BF16)],
        compiler_params=_params(2),
        name="position_dft",
    )(v4, m1, twc, tws, m2)


def _pool_minus_token(ue, rows, t0, t_seq):
    halo = POOL_HALO_ROWS
    tvec = t0 + lax.broadcasted_iota(jnp.int32, (rows, LANES), 0) // SUBLANES
    parts = []
    for g, win in enumerate(POOL_WINDOW_SIZES):
        col = ue[:, g * LANES:(g + 1) * LANES]
        n = col.shape[0]
        acc = col[0:n - SUBLANES] + col[SUBLANES:n]
        e0 = SUBLANES
        span = 1
        while 2 * span < win:
            sh = span * SUBLANES
            n = acc.shape[0]
            acc = acc[0:n - 2 * sh] + acc[2 * sh:n]
            e0 += sh
            span *= 2
        wsum = acc[halo - e0:halo - e0 + rows]
        half = win // 2
        cnt = jnp.minimum(tvec + half, t_seq) - jnp.maximum(tvec - half, 0)
        parts.append(wsum / cnt.astype(F32) - col[halo:halo + rows])
    return jnp.concatenate(parts, axis=1)


def _k3_body(cfg, *refs):
    last, n_lat_tiles, n_tiles, t_lat, t_ctx = cfg
    refs = list(refs)
    x_ref, h_ref, mod_ref, hf_ref, hb_ref, yl_ref = refs[:6]
    refs = refs[6:]
    yc_ref = None if last else refs.pop(0)
    up_ref, upp_ref, upn_ref, wz_ref, pa_ref, pb_ref, pc_ref, wo_ref, pw_ref, ps_ref, fg_ref, o_ref = refs
    j = pl.program_id(0)
    is_lat = j < n_lat_tiles
    rows, d = x_ref.shape
    bsz = SUBLANES
    tt = rows // bsz
    h = h_ref[...]

    seq_first, seq_last = _seq_edges(j, n_lat_tiles, n_tiles)
    ue = jnp.concatenate([
        upp_ref[...].astype(F32) * jnp.where(seq_first, 0.0, 1.0).astype(F32),
        up_ref[...].astype(F32),
        upn_ref[...].astype(F32) * jnp.where(seq_last, 0.0, 1.0).astype(F32)], axis=0)
    t0 = jnp.where(is_lat, j, j - n_lat_tiles) * tt
    t_seq = jnp.where(is_lat, t_lat, t_ctx)
    p = _pool_minus_token(ue, rows, t0, t_seq).astype(BF16)

    w = yl_ref.shape[-1]
    z_all = jnp.dot(h, wz_ref[...], preferred_element_type=F32)
    c0 = 3 * d
    w_lru = hf_ref.shape[-1]
    z_lru, z_fft, z_pool = z_all[:, c0:c0 + w_lru], z_all[:, c0 + w_lru:c0 + w_lru + w], z_all[:, c0 + w_lru + w:]

    y_lru = hf_ref[0].astype(F32) + hb_ref[0].astype(F32)
    ya = jnp.dot((y_lru * _silu(z_lru)).astype(BF16), pa_ref[...], preferred_element_type=F32)

    y_fft = yl_ref[...].reshape(rows, w)
    if not last:
        y_fft = jnp.where(is_lat, y_fft, yc_ref[...].reshape(rows, w))
    yb = jnp.dot((y_fft.astype(F32) * _silu(z_fft)).astype(BF16), pb_ref[...], preferred_element_type=F32)

    y_pool = jnp.dot(p, pw_ref[...], preferred_element_type=F32) * ps_ref[...]
    yc = jnp.dot((y_pool * _silu(z_pool)).astype(BF16), pc_ref[...], preferred_element_type=F32)

    gates = _sigmoid(z_all[:, :c0])
    m = gates[:, :d] * ya + gates[:, d:2 * d] * yb + gates[:, 2 * d:] * yc
    out = jnp.dot(m.astype(BF16), wo_ref[...], preferred_element_type=F32)
    xn = x_ref[...].reshape(tt, bsz, d) + mod_ref[0, 2] * out.reshape(tt, bsz, d)
    if last:
        ms = jnp.mean(xn * xn, axis=-1, keepdims=True)
        xn = xn * lax.rsqrt(ms + NORM_EPS) * fg_ref[...]
        o_ref[...] = jnp.transpose(xn, (1, 0, 2))
    else:
        o_ref[...] = xn.reshape(rows, d)


def _k3(x_rows, h, mod_k, h_lru, y_lat, y_ctx, u_pool, wz, pa, pb, pc, wo, pw, ps, fg, last,
        n_lat_tiles, t_lat, t_ctx):
    n_all, d = h.shape
    bsz = SUBLANES
    r = TOKEN_TILE_ROWS
    tt = r // bsz
    n_tiles = n_all // r
    grid_tiles = n_lat_tiles if last else n_tiles
    w_lru = h_lru.shape[2]
    w_fft = y_lat.shape[-1]
    w_pool = u_pool.shape[1]
    hr = r // POOL_HALO_ROWS
    n_halo = n_all // POOL_HALO_ROWS
    cfg = (last, n_lat_tiles, n_tiles, t_lat, t_ctx)
    const2 = lambda j: (0, 0)
    assert [width for _, width in _merge_weight_segments(w_lru, w_fft, w_pool, d)] == [3 * d, w_lru, w_fft, w_pool]

    def dft_spec(y4, first_tile):
        n2 = y4.shape[1] * y4.shape[2] // bsz
        assert tt % n2 == 0
        blk = (tt // n2,) + y4.shape[1:]
        n_blk = y4.shape[0] // blk[0]
        return pl.BlockSpec(blk, lambda j: (jnp.clip(j - first_tile, 0, n_blk - 1), 0, 0, 0))

    row_spec = lambda width: pl.BlockSpec((r, width), lambda j: (j, 0))
    pair_spec = lambda k: pl.BlockSpec((1, r, w_lru), lambda j: (k, j, 0))
    in_specs = [
        row_spec(d), row_spec(d),
        pl.BlockSpec((1, 3, bsz, d), lambda j: (jnp.where(j < n_lat_tiles, 0, 1), 0, 0, 0)),
        pair_spec(0), pair_spec(1),
        dft_spec(y_lat, 0),
    ]
    args = [x_rows, h, mod_k, h_lru, h_lru, y_lat]
    if not last:
        in_specs.append(dft_spec(y_ctx, n_lat_tiles))
        args.append(y_ctx)
    in_specs += [
        row_spec(w_pool),
        pl.BlockSpec((POOL_HALO_ROWS, w_pool), lambda j: (jnp.maximum(j * hr - 1, 0), 0)),
        pl.BlockSpec((POOL_HALO_ROWS, w_pool), lambda j: (jnp.minimum((j + 1) * hr, n_halo - 1), 0)),
        _resident(wz.shape, const2),
    ]
    args += [u_pool, u_pool, u_pool, wz]
    for wgt in (pa, pb, pc, wo, pw, ps, fg):
        in_specs.append(_resident(wgt.shape, const2))
        args.append(wgt)
    if last:
        out_spec = pl.BlockSpec((bsz, tt, d), lambda j: (0, j, 0))
        out_shape = jax.ShapeDtypeStruct((bsz, t_lat, d), F32)
    else:
        out_spec = row_spec(d)
        out_shape = jax.ShapeDtypeStruct((n_all, d), F32)
    return pl.pallas_call(
        functools.partial(_k3_body, cfg),
        grid=(grid_tiles,),
        in_specs=in_specs,
        out_specs=out_spec,
        out_shape=out_shape,
        compiler_params=_params(),
        name="merge_residual",
    )(*args)


def _coordinate_code(n, d):
    quarter = d // 4
    omega = 1.0 / (POSITION_BASE ** (jnp.arange(quarter, dtype=F32) / quarter))
    ang = jnp.arange(n).astype(F32)[:, None] * omega[None, :]
    return jnp.concatenate([jnp.sin(ang), jnp.cos(ang)], axis=-1).astype(F32)


def _block_diag(w):
    g, n, _ = w.shape
    eye = jnp.eye(g, dtype=w.dtype)
    return jnp.einsum("gij,gk->gikj", w, eye).reshape(g * n, g * n)


def _channel_dft_matrix(n):
    k = np.arange(n)
    ang = 2.0 * np.pi * ((k[:, None] * k[None, :]) % n) / n
    return jnp.asarray(np.concatenate([np.cos(ang), -np.sin(ang)], axis=1) / np.sqrt(n), F32)


def _fold_body(cs_ref, fw_ref, o_ref):
    _, groups, n, _ = fw_ref.shape
    o_ref[...] = jnp.zeros_like(o_ref)
    for g in range(groups):
        for part in range(2):
            blk = jnp.dot(cs_ref[:, part * n:(part + 1) * n], fw_ref[0, g], preferred_element_type=F32,
                          precision=lax.Precision.HIGHEST)
            col = part * groups * n + g * n
            o_ref[0, g * n:(g + 1) * n, col:col + n] = blk.astype(BF16)


def _fold_channel_map(cs, fw):
    depth, groups, n, _ = fw.shape
    w = groups * n
    return pl.pallas_call(
        _fold_body,
        grid=(depth,),
        in_specs=[pl.BlockSpec(cs.shape, lambda l: (0, 0)), pl.BlockSpec((1, groups, n, n), lambda l: (l, 0, 0, 0))],
        out_specs=pl.BlockSpec((1, w, 2 * w), lambda l: (l, 0, 0)),
        out_shape=jax.ShapeDtypeStruct((depth, w, 2 * w), BF16),
        compiler_params=_params(),
        name="fold_fft_map",
    )(cs, fw)


def kernel(x, c, ctx, c_ctx, norm_g, ada_w, ada_b, w_in, conv_w, conv_b, lru_wa, lru_ba, lru_wx, lru_bx, lru_lam,
           fft_w, pool_w, pool_scale, proj_a, proj_b, proj_c, w_out, final_g):
    bsz, t_lat, d = x.shape
    t_ctx = ctx.shape[1]
    depth = w_in.shape[0]
    w_lru = conv_w.shape[-1]
    fft_groups, fft_dim = fft_w.shape[1], fft_w.shape[2]
    w_fft = fft_groups * fft_dim
    w_pool = pool_w.shape[1] * pool_w.shape[2]
    heads, head_dim = lru_wa.shape[2], lru_wa.shape[3]
    assert bsz == SUBLANES and fft_dim == LANES and pool_w.shape[2] == LANES and conv_w.shape[1] == CONV_TAPS
    assert t_lat % DFT_RADIX == 0 and t_ctx % DFT_RADIX == 0 and (t_lat // DFT_RADIX) % (t_ctx // DFT_RADIX) == 0
    n_lat_rows, n_ctx_rows = t_lat * bsz, t_ctx * bsz
    n_all = n_lat_rows + n_ctx_rows
    n_lat_tiles = n_lat_rows // TOKEN_TILE_ROWS
    lru_steps = min(LRU_CHUNK_STEPS, t_ctx)
    heads_per_group = LRU_GATE_WIDTH // head_dim

    assert TOKEN_TILE_ROWS // bsz == GRID_WIDTH
    code = _coordinate_code(max(t_lat // GRID_WIDTH, GRID_WIDTH), d)
    cc =jnp.concatenate([c, jnp.broadcast_to(c_ctx[None, :], (bsz, d))], axis=0)
    mod = _modulation(cc, ada_w, ada_b)
    cdw = _fold_channel_map(_channel_dft_matrix(fft_dim), fft_w)
    n2_lat, n2_ctx = t_lat // DFT_RADIX, t_ctx // DFT_RADIX

    def gate_w(w):
        wd = w.reshape(2 * heads // heads_per_group, heads_per_group, head_dim, head_dim)
        return jax.vmap(_block_diag)(wd).reshape(2, heads // heads_per_group, LRU_GATE_WIDTH, LRU_GATE_WIDTH)

    tokens = (x, _batch_major_to_rows(ctx), code)
    out = None
    for l in range(depth):
        first, last = l == 0, l == depth - 1
        mod_k = mod[l].reshape(2, bsz, 3, d).transpose(0, 2, 1, 3)
        g = norm_g[l].reshape(1, d)
        k1_out = _k1(tokens, mod_k, g, w_in, l, cdw, conv_w[l], conv_b[l], first, n_all, n_lat_tiles,
                     w_lru, w_fft, w_pool)
        xh, v4, u_pool, h = k1_out[:4]
        x_rows = k1_out[4] if first else tokens[0]

        wg = jnp.concatenate([gate_w(lru_wa[l]), gate_w(lru_wx[l])], axis=-1).astype(BF16)
        h_lru, wz = _lru(xh, wg, lru_ba[l], lru_bx[l], lru_lam[l], w_in, l,
                         _merge_weight_segments(w_lru, w_fft, w_pool, d), n_lat_rows, n_ctx_rows, lru_steps)

        y_lat = _position_dft(v4, n2_lat, 0)
        y_ctx = None if last else _position_dft(v4, n2_ctx, n2_lat // n2_ctx)

        out = _k3(x_rows, h, mod_k, h_lru, y_lat, y_ctx, u_pool, wz,
                  proj_a[l].astype(BF16), proj_b[l].astype(BF16), proj_c[l].astype(BF16), w_out[l].astype(BF16),
                  _block_diag(pool_w[l]).astype(BF16),
                  pool_scale[l].reshape(1, w_pool), final_g.reshape(1, d), last, n_lat_tiles, t_lat, t_ctx)
        tokens = (out,)
    return out
```
